```python
import jax, jax.numpy as jnp
from jax import lax
import numpy as np

D_MODEL = 1024
BATCH = 16
SEQ = 2048
DEPTH = 1

N_META = 16
D_MIX = D_MODEL
MLA_HEADS = 4
QK_NOPE_DIM = 128
QK_ROPE_DIM = 64
V_HEAD_DIM = 128
MLA_WIDTH = MLA_HEADS * V_HEAD_DIM
Q_LORA_RANK = 256
KV_LORA_RANK = 128
ROPE_THETA = 10000.0
ATTN_SCALE = (QK_NOPE_DIM + QK_ROPE_DIM) ** -0.5
Q_BLOCK = 128
NEG_INF = -1e30
CONV_WIDTH = D_MIX - MLA_WIDTH
CONV_GROUPS = 8
CONV_KSIZE = 3
IN_SPLITS = (Q_LORA_RANK, KV_LORA_RANK, QK_ROPE_DIM, MLA_WIDTH,
             CONV_WIDTH, CONV_WIDTH, CONV_WIDTH, CONV_WIDTH)
IN_PROJ_DIM = sum(IN_SPLITS)
EPS = 1e-6

kernel_name = "hymba_mla_shortconv_hybrid"


def rms_norm(x, g):
    xf = x.astype(jnp.float32)
    y = xf * lax.rsqrt(jnp.mean(xf * xf, axis=-1, keepdims=True) + EPS)
    return (y * g.astype(jnp.float32)).astype(x.dtype)


def apply_rope(x, pos):
    half = x.shape[-1] // 2
    inv_freq = 1.0 / (ROPE_THETA ** (jnp.arange(half, dtype=jnp.float32) / half))
    ang = pos.astype(jnp.float32)[:, None] * inv_freq[None, :]
    cos = jnp.cos(ang)[None, :, None, :]
    sin = jnp.sin(ang)[None, :, None, :]
    xf = x.astype(jnp.float32)
    x1, x2 = xf[..., :half], xf[..., half:]
    return jnp.concatenate([x1 * cos - x2 * sin, x2 * cos + x1 * sin], axis=-1).astype(x.dtype)


def _attend_block(q_blk, q_pos, k, v, k_pos):
    s = jnp.einsum('bqhd,bkhd->bhqk', q_blk, k, preferred_element_type=jnp.float32) * ATTN_SCALE
    mask = k_pos[None, :] <= q_pos[:, None]
    s = jnp.where(mask[None, None], s, NEG_INF)
    p = jax.nn.softmax(s, axis=-1)
    return jnp.einsum('bhqk,bkhd->bqhd', p.astype(v.dtype), v)


def causal_attention(q, k, v):
    B, T, H, _ = q.shape
    pos = jnp.arange(T)
    o_meta = _attend_block(q[:, :N_META], pos[:N_META], k[:, :N_META], v[:, :N_META], pos[:N_META])
    n_blk = (T - N_META) // Q_BLOCK
    q_real = q[:, N_META:].reshape(B, n_blk, Q_BLOCK, H, q.shape[-1]).transpose(1, 0, 2, 3, 4)
    q_pos = pos[N_META:].reshape(n_blk, Q_BLOCK)
    o_real = lax.map(lambda a: _attend_block(a[0], a[1], k, v, pos), (q_real, q_pos))
    o_real = o_real.transpose(1, 0, 2, 3, 4).reshape(B, T - N_META, H, v.shape[-1])
    return jnp.concatenate([o_meta, o_real], axis=1)


def causal_dwconv(u, w):
    C = u.shape[-1]
    return lax.conv_general_dilated(
        u, w[:, None, :].astype(u.dtype), window_strides=(1,), padding=[(CONV_KSIZE - 1, 0)],
        dimension_numbers=('NWC', 'WIO', 'NWC'), feature_group_count=C)


def hybrid_layer(h, norm_g, w_in, q_norm_g, w_q_up, kv_norm_g, w_kv_up, conv_w,
                 attn_out_g, conv_out_g, w_out):
    B, T, _ = h.shape
    pos = jnp.arange(T)
    u = rms_norm(h, norm_g)
    p = u @ w_in
    idx = np.cumsum(IN_SPLITS)[:-1].tolist()
    c_q, c_kv, k_rope, z_attn, conv_b, conv_c, conv_h, z_conv = jnp.split(p, idx, axis=-1)

    q = (rms_norm(c_q, q_norm_g) @ w_q_up).reshape(B, T, MLA_HEADS, QK_NOPE_DIM + QK_ROPE_DIM)
    q_nope, q_pe = q[..., :QK_NOPE_DIM], q[..., QK_NOPE_DIM:]
    q_pe = apply_rope(q_pe, pos)
    kv = (rms_norm(c_kv, kv_norm_g) @ w_kv_up).reshape(B, T, MLA_HEADS, QK_NOPE_DIM + V_HEAD_DIM)
    k_nope, v = kv[..., :QK_NOPE_DIM], kv[..., QK_NOPE_DIM:]
    k_pe = apply_rope(k_rope[:, :, None, :], pos)
    k_pe = jnp.broadcast_to(k_pe, (B, T, MLA_HEADS, QK_ROPE_DIM))
    q_full = jnp.concatenate([q_nope, q_pe], axis=-1)
    k_full = jnp.concatenate([k_nope, k_pe], axis=-1)
    o = causal_attention(q_full, k_full, v)
    o = rms_norm(o, attn_out_g.reshape(MLA_HEADS, V_HEAD_DIM)).reshape(B, T, MLA_WIDTH)
    y_attn = o * jax.nn.silu(z_attn)

    yc = conv_b * causal_dwconv(conv_c * conv_h, conv_w)
    yc = rms_norm(yc.reshape(B, T, CONV_GROUPS, CONV_WIDTH // CONV_GROUPS),
                  conv_out_g.reshape(CONV_GROUPS, CONV_WIDTH // CONV_GROUPS)).reshape(B, T, CONV_WIDTH)
    y_conv = yc * jax.nn.silu(z_conv)

    mix = jnp.concatenate([y_attn, y_conv], axis=-1) @ w_out
    return h + mix


def setup_inputs(seed: int = 0) -> dict:
    key = jax.random.key(seed)
    ks = jax.random.split(key, 16)
    f32 = jnp.float32

    def w(k, shape, fan_in):
        return jax.random.normal(k, shape, f32) * fan_in ** -0.5

    def gain(k, shape):
        return 1.0 + 0.02 * jax.random.normal(k, shape, f32)

    return {
        "x": jax.random.normal(ks[0], (BATCH, SEQ, D_MODEL), f32),
        "meta_tokens": jax.random.normal(ks[1], (N_META, D_MODEL), f32),
        "norm_g": gain(ks[2], (DEPTH, D_MODEL)),
        "w_in": w(ks[3], (DEPTH, D_MODEL, IN_PROJ_DIM), D_MODEL),
        "q_norm_g": gain(ks[4], (DEPTH, Q_LORA_RANK)),
        "w_q_up": w(ks[5], (DEPTH, Q_LORA_RANK, MLA_HEADS * (QK_NOPE_DIM + QK_ROPE_DIM)), Q_LORA_RANK),
        "kv_norm_g": gain(ks[6], (DEPTH, KV_LORA_RANK)),
        "w_kv_up": w(ks[7], (DEPTH, KV_LORA_RANK, MLA_HEADS * (QK_NOPE_DIM + V_HEAD_DIM)), KV_LORA_RANK),
        "conv_w": w(ks[8], (DEPTH, CONV_KSIZE, CONV_WIDTH), CONV_KSIZE),
        "attn_out_g": gain(ks[9], (DEPTH, MLA_WIDTH)),
        "conv_out_g": gain(ks[10], (DEPTH, CONV_WIDTH)),
        "w_out": w(ks[11], (DEPTH, D_MIX, D_MODEL), D_MIX),
        "final_norm_g": gain(ks[12], (D_MODEL,)),
    }


def reference(x, meta_tokens, norm_g, w_in, q_norm_g, w_q_up, kv_norm_g, w_kv_up, conv_w,
              attn_out_g, conv_out_g, w_out, final_norm_g):
    B = x.shape[0]
    meta = jnp.broadcast_to(meta_tokens[None].astype(x.dtype), (B, N_META, x.shape[-1]))
    h = jnp.concatenate([meta, x], axis=1)
    for l in range(DEPTH):
        h = hybrid_layer(h, norm_g[l], w_in[l], q_norm_g[l], w_q_up[l], kv_norm_g[l], w_kv_up[l],
                         conv_w[l], attn_out_g[l], conv_out_g[l], w_out[l])
    return rms_norm(h, final_norm_g)[:, N_META:]
```

```python
import functools
import math

import jax
import jax.numpy as jnp
import numpy as np
from jax import lax
from jax.experimental import pallas as pl
from jax.experimental.pallas import tpu as pltpu

F32 = jnp.float32
BF16 = jnp.bfloat16

D_MODEL = 1024
N_META = 16
HEADS = 4
NOPE = 128
ROPE = 64
VDIM = 128
Q_LORA = 256
KV_LORA = 128
CONV_W = 512
CONV_GROUP = 64
ROPE_THETA = 10000.0
ATTN_SCALE = (NOPE + ROPE) ** -0.5
NEG_INF = -1e30
EPS = 1e-6

LANES = 128
HEAD_PAD = 2 * LANES
P_COLS = 3072

PROJ_TILE = 512
Q_TILE = 256
KV_TILE = 256


def _rms(x, g):
    ms = jnp.mean(x * x, axis=-1, keepdims=True)
    return x * lax.rsqrt(ms + EPS) * g


def _silu(x):
    return x * (1.0 / (1.0 + jnp.exp(-x)))


def _rope128(x, cos, sin):
    return x * cos + pltpu.roll(x, ROPE, 1) * sin


def _in_proj(x, norm_g, w_in):
    u = _rms(x, norm_g).astype(BF16)
    return jnp.dot(u, w_in, preferred_element_type=F32)


def _keys_values(p, cos, sin, kv_g, w_kv):
    c_kv = p[:, Q_LORA:Q_LORA + KV_LORA]
    kv = jnp.dot(_rms(c_kv, kv_g).astype(BF16), w_kv, preferred_element_type=F32)
    k_pe = _rope128(p[:, 384:512], cos, sin).astype(BF16)
    ks, vs = [], []
    for h in range(HEADS):
        ks.append(kv[:, h * 256:h * 256 + NOPE].astype(BF16))
        ks.append(k_pe)
        vs.append(kv[:, h * 256 + NOPE:(h + 1) * 256].astype(BF16))
    return ks, vs


def _meta_kernel(x_ref, cos_ref, sin_ref, norm_g_ref, w_in_ref, kv_g_ref, w_kv_ref,
                 k_out, v_out, g_out):
    p = _in_proj(x_ref[...], norm_g_ref[...], w_in_ref[...])
    ks, vs = _keys_values(p, cos_ref[...], sin_ref[...], kv_g_ref[...], w_kv_ref[...])
    for i, kk in enumerate(ks):
        k_out[:, i * LANES:(i + 1) * LANES] = kk
    for h, vv in enumerate(vs):
        v_out[:, h * VDIM:(h + 1) * VDIM] = vv
    g_out[...] = p[:, 2048:2560] * p[:, 1536:2048]


def _proj_kernel(x_ref, cos_ref, sin_ref, ginit_ref, norm_g_ref, w_in_ref, q_g_ref, w_q_ref,
                 kv_g_ref, w_kv_ref, conv_w_ref, conv_g_ref, gmat_ref,
                 q_out, k_out, v_out, ga_out, yc_out, gbuf):
    tm = x_ref.shape[1]
    cos = cos_ref[...]
    sin = sin_ref[...]
    p = _in_proj(x_ref[0], norm_g_ref[...], w_in_ref[...])

    q = jnp.dot(_rms(p[:, 0:Q_LORA], q_g_ref[...]).astype(BF16), w_q_ref[...],
                preferred_element_type=F32)
    for h in range(HEADS):
        q_out[0, :, h * 256:h * 256 + NOPE] = (q[:, h * 256:h * 256 + NOPE] * ATTN_SCALE).astype(BF16)
        q_pe = _rope128(q[:, h * 256 + NOPE:(h + 1) * 256], cos, sin)
        q_out[0, :, h * 256 + NOPE:(h + 1) * 256] = (q_pe * ATTN_SCALE).astype(BF16)

    ks, vs = _keys_values(p, cos, sin, kv_g_ref[...], w_kv_ref[...])
    for i, kk in enumerate(ks):
        k_out[0, :, i * LANES:(i + 1) * LANES] = kk
    for h, vv in enumerate(vs):
        v_out[0, :, h * VDIM:(h + 1) * VDIM] = vv

    ga_out[0] = _silu(p[:, 512:1024]).astype(BF16)

    @pl.when(pl.program_id(1) == 0)
    def _():
        gbuf[0:8, :] = ginit_ref[8:16, :]

    g = p[:, 1536:2048] * p[:, 2048:2560]
    gbuf[8:8 + tm, :] = g
    g1 = gbuf[7:7 + tm, :]
    g2 = gbuf[6:6 + tm, :]
    cw = conv_w_ref[...]
    conv = cw[0:1, :] * g2 + cw[1:2, :] * g1 + cw[2:3, :] * g
    gbuf[0:8, :] = gbuf[tm:tm + 8, :]
    yc = p[:, 1024:1536] * conv
    sq = yc * yc
    hi = sq.astype(BF16)
    lo = (sq - hi.astype(F32)).astype(BF16)
    gm = gmat_ref[...]
    ssum = (jnp.dot(hi, gm, preferred_element_type=F32) + jnp.dot(lo, gm, preferred_element_type=F32))
    ycn = yc * lax.rsqrt(ssum * (1.0 / CONV_GROUP) + EPS) * conv_g_ref[...]
    yc_out[0] = (ycn * _silu(p[:, 2560:3072])).astype(BF16)


def _dot_nt(a, b):
    return lax.dot_general(a, b, (((1,), (1,)), ((), ())), preferred_element_type=F32)


def _attn_kernel(q_ref, k_ref, v_ref, km_ref, vm_ref, ga_ref, yc_ref, x_ref, w_out_ref,
                 attn_g_ref, final_g_ref, o_ref, y_scr):
    i = pl.program_id(1)
    tq = q_ref.shape[1]
    tk = KV_TILE
    row = lax.broadcasted_iota(jnp.int32, (tq, tk), 0)
    col = lax.broadcasted_iota(jnp.int32, (tq, tk), 1)
    causal = col <= row

    for h in range(HEADS):
        q = q_ref[0, :, h * 256:(h + 1) * 256]
        s = _dot_nt(q, km_ref[:, h * 256:(h + 1) * 256])
        m = jnp.max(s, axis=-1, keepdims=True)
        pm = jnp.exp(s - m)
        l = jnp.sum(pm, axis=-1, keepdims=True)
        acc = jnp.dot(pm.astype(BF16), vm_ref[:, h * VDIM:(h + 1) * VDIM],
                      preferred_element_type=F32)

        def step(j, carry, masked):
            m, l, acc = carry
            start = pl.multiple_of(j * tk, tk)
            k = k_ref[0, pl.ds(start, tk), h * 256:(h + 1) * 256]
            v = v_ref[0, pl.ds(start, tk), h * VDIM:(h + 1) * VDIM]
            s = _dot_nt(q, k)
            if masked:
                s = jnp.where(causal, s, NEG_INF)
            m_new = jnp.maximum(m, jnp.max(s, axis=-1, keepdims=True))
            alpha = jnp.exp(m - m_new)
            pr = jnp.exp(s - m_new)
            l = alpha * l + jnp.sum(pr, axis=-1, keepdims=True)
            acc = alpha * acc + jnp.dot(pr.astype(BF16), v, preferred_element_type=F32)
            return m_new, l, acc

        carry = lax.fori_loop(0, i, functools.partial(step, masked=False), (m, l, acc))
        m, l, acc = step(i, carry, masked=True)

        o = acc * (1.0 / l)
        o = _rms(o, attn_g_ref[:, h * VDIM:(h + 1) * VDIM])
        y_scr[:, h * VDIM:(h + 1) * VDIM] = (o * ga_ref[0, :, h * VDIM:(h + 1) * VDIM].astype(F32)).astype(BF16)

    y_scr[:, HEADS * VDIM:] = yc_ref[0]
    mix = jnp.dot(y_scr[...], w_out_ref[...], preferred_element_type=F32)
    o_ref[0] = _rms(x_ref[0] + mix, final_g_ref[...])


def _rope_tables(pos):
    half = ROPE // 2
    inv_freq = 1.0 / (ROPE_THETA ** (jnp.arange(half, dtype=F32) / half))
    ang = pos.astype(F32)[:, None] * inv_freq[None, :]
    c, s = jnp.cos(ang), jnp.sin(ang)
    z = jnp.zeros((pos.shape[0], LANES - ROPE), F32)
    return jnp.concatenate([c, c, z], axis=-1), jnp.concatenate([-s, s, z], axis=-1)


def _swap_halves(w):
    half = w.shape[-1] // 2
    return jnp.concatenate([w[..., half:], w[..., :half]], axis=-1)


def _full(shape):
    return pl.BlockSpec(shape, lambda *_: (0,) * len(shape))


def _layer(x, meta_tokens, norm_g, w_in, q_norm_g, w_q_up, kv_norm_g, w_kv_up, conv_w,
           attn_out_g, conv_out_g, w_out, final_norm_g):
    B, S, D = x.shape
    assert D == D_MODEL and S % PROJ_TILE == 0 and S % Q_TILE == 0 and Q_TILE == KV_TILE
    assert meta_tokens.shape == (N_META, D_MODEL)

    k_rope_w = w_in[:, 384:448]
    w_in_p = jnp.concatenate([w_in[:, :448], _swap_halves(k_rope_w), w_in[:, 448:]], axis=1).astype(BF16)
    wq = w_q_up.reshape(Q_LORA, HEADS, NOPE + ROPE)
    wq_p = jnp.concatenate([wq, _swap_halves(wq[..., NOPE:])], axis=-1).reshape(Q_LORA, HEADS * HEAD_PAD).astype(BF16)
    w_kv_b = w_kv_up.astype(BF16)
    w_out_b = w_out.astype(BF16)
    gid = np.arange(CONV_W) // CONV_GROUP
    gmat = jnp.asarray((gid[:, None] == gid[None, :]).astype(np.float32), dtype=BF16)
    row = lambda v: v.reshape(1, -1).astype(F32)

    cos_m, sin_m = _rope_tables(jnp.arange(N_META))
    cos_r, sin_r = _rope_tables(N_META + jnp.arange(S))

    k_meta, v_meta, g_meta = pl.pallas_call(
        _meta_kernel,
        out_shape=(jax.ShapeDtypeStruct((N_META, HEADS * HEAD_PAD), BF16),
                   jax.ShapeDtypeStruct((N_META, HEADS * VDIM), BF16),
                   jax.ShapeDtypeStruct((N_META, CONV_W), F32)),
        name="meta_proj",
    )(meta_tokens.astype(F32), cos_m, sin_m, row(norm_g), w_in_p, row(kv_norm_g), w_kv_b)

    tm = PROJ_TILE
    tok = lambda w: pl.BlockSpec((1, tm, w), lambda b, t: (b, t, 0))
    q_s, k_s, v_s, ga_s, yc_s = pl.pallas_call(
        _proj_kernel,
        grid=(B, S // tm),
        in_specs=[tok(D_MODEL),
                  pl.BlockSpec((tm, LANES), lambda b, t: (t, 0)),
                  pl.BlockSpec((tm, LANES), lambda b, t: (t, 0)),
                  _full((N_META, CONV_W)), _full((1, D_MODEL)), _full((D_MODEL, P_COLS)),
                  _full((1, Q_LORA)), _full((Q_LORA, HEADS * HEAD_PAD)),
                  _full((1, KV_LORA)), _full((KV_LORA, HEADS * 256)),
                  _full((3, CONV_W)), _full((1, CONV_W)), _full((CONV_W, CONV_W))],
        out_specs=[tok(HEADS * HEAD_PAD), tok(HEADS * HEAD_PAD), tok(HEADS * VDIM),
                   tok(HEADS * VDIM), tok(CONV_W)],
        out_shape=[jax.ShapeDtypeStruct((B, S, HEADS * HEAD_PAD), BF16),
                   jax.ShapeDtypeStruct((B, S, HEADS * HEAD_PAD), BF16),
                   jax.ShapeDtypeStruct((B, S, HEADS * VDIM), BF16),
                   jax.ShapeDtypeStruct((B, S, HEADS * VDIM), BF16),
                   jax.ShapeDtypeStruct((B, S, CONV_W), BF16)],
        scratch_shapes=[pltpu.VMEM((tm + 8, CONV_W), F32)],
        compiler_params=pltpu.CompilerParams(dimension_semantics=("arbitrary", "arbitrary")),
        name="proj",
    )(x, cos_r, sin_r, g_meta, row(norm_g), w_in_p, row(q_norm_g), wq_p, row(kv_norm_g), w_kv_b,
      conv_w.astype(F32), row(conv_out_g), gmat)

    tq = Q_TILE
    qtile = lambda w: pl.BlockSpec((1, tq, w), lambda b, i: (b, i, 0))
    seq = lambda w: pl.BlockSpec((1, S, w), lambda b, i: (b, 0, 0))
    out = pl.pallas_call(
        _attn_kernel,
        grid=(B, S // tq),
        in_specs=[qtile(HEADS * HEAD_PAD), seq(HEADS * HEAD_PAD), seq(HEADS * VDIM),
                  _full((N_META, HEADS * HEAD_PAD)), _full((N_META, HEADS * VDIM)),
                  qtile(HEADS * VDIM), qtile(CONV_W), qtile(D_MODEL),
                  _full((D_MODEL, D_MODEL)), _full((1, HEADS * VDIM)), _full((1, D_MODEL))],
        out_specs=qtile(D_MODEL),
        out_shape=jax.ShapeDtypeStruct((B, S, D_MODEL), F32),
        scratch_shapes=[pltpu.VMEM((tq, D_MODEL), BF16)],
        compiler_params=pltpu.CompilerParams(dimension_semantics=("arbitrary", "arbitrary")),
        name="attn_out",
    )(q_s, k_s, v_s, k_meta, v_meta, ga_s, yc_s, x, w_out_b, row(attn_out_g), row(final_norm_g))
    return out


def kernel(x, meta_tokens, norm_g, w_in, q_norm_g, w_q_up, kv_norm_g, w_kv_up, conv_w,
           attn_out_g, conv_out_g, w_out, final_norm_g):
    assert norm_g.shape[0] == 1, "single-layer block"
    return _layer(x, meta_tokens, norm_g[0], w_in[0], q_norm_g[0], w_q_up[0], kv_norm_g[0],
                  w_kv_up[0], conv_w[0], attn_out_g[0], conv_out_g[0], w_out[0], final_norm_g)
```

```python
import math

import jax
import jax.numpy as jnp
import numpy as np
from jax import lax
from jax.experimental import pallas as pl
from jax.experimental.pallas import tpu as pltpu

F32 = jnp.float32
BF16 = jnp.bfloat16

D_MODEL = 1024
N_META = 16
HEADS = 4
NOPE = 128
ROPE = 64
VDIM = 128
Q_LORA = 256
KV_LORA = 128
CONV_W = 512
CONV_GROUP = 64
ROPE_THETA = 10000.0
ATTN_SCALE = (NOPE + ROPE) ** -0.5
NEG_INF = -1e30
EPS = 1e-6

LANES = 128
HEAD_PAD = 2 * LANES
P_COLS = 3072

TILE = 512
HALF = TILE // 2
Q_SCALE = ATTN_SCALE * math.log2(math.e)


def _rms(x, g):
    ms = jnp.mean(x * x, axis=-1, keepdims=True)
    return x * lax.rsqrt(ms + EPS) * g


def _silu(x):
    return x * (1.0 / (1.0 + jnp.exp(-x)))


def _dot(a, b):
    return jnp.dot(a, b, preferred_element_type=F32)


def _dot_nt(a, b):
    return lax.dot_general(a, b, (((1,), (1,)), ((), ())), preferred_element_type=F32)


def _in_proj(x, norm_g, w_in):
    u = _rms(x, norm_g).astype(BF16)
    return _dot(u, w_in)


def _keys_values(p, cos, sin, kv_g, w_k, w_vt):
    c_kv = _rms(p[:, Q_LORA:Q_LORA + KV_LORA], kv_g).astype(BF16)
    k_nope = _dot(c_kv, w_k)
    v_t = _dot_nt(w_vt, c_kv)
    kr = p[:, 384:512]
    k_pe = (kr * cos + pltpu.roll(kr, ROPE, 1) * sin).astype(BF16)
    ks = []
    for h in range(HEADS):
        ks.append(k_nope[:, h * NOPE:(h + 1) * NOPE].astype(BF16))
        ks.append(k_pe)
    return ks, v_t.astype(BF16)


def _meta_kernel(x_ref, cos_ref, sin_ref, norm_g_ref, w_in_ref, kv_g_ref, w_k_ref, w_vt_ref,
                 k_out, vt_out, g_out):
    p = _in_proj(x_ref[...], norm_g_ref[...], w_in_ref[...])
    ks, v_t = _keys_values(p, cos_ref[...], sin_ref[...], kv_g_ref[...], w_k_ref[...], w_vt_ref[...])
    for i, kk in enumerate(ks):
        k_out[:, i * LANES:(i + 1) * LANES] = kk
    vt_out[...] = v_t
    g_out[...] = p[:, 2048:2560] * p[:, 1536:2048]


def _proj_kernel(x_ref, cos_ref, sin_ref, cos_t_ref, sin_t_ref, ginit_ref, norm_g_ref, w_in_ref,
                 q_g_ref, w_qt_ref, kv_g_ref, w_k_ref, w_vt_ref, conv_w_ref, conv_g_ref, gmat_ref,
                 qt_out, k_out, vt_out, ga_out, yc_out, gbuf):
    tm = x_ref.shape[1]
    p = _in_proj(x_ref[0], norm_g_ref[...], w_in_ref[...])

    c_q = _rms(p[:, 0:Q_LORA], q_g_ref[...]).astype(BF16)
    q_t = _dot_nt(w_qt_ref[...], c_q)
    cos_t = cos_t_ref[...]
    sin_t = sin_t_ref[...]
    for h in range(HEADS):
        r = h * HEAD_PAD
        qt_out[0, r:r + NOPE, :] = (q_t[r:r + NOPE, :] * Q_SCALE).astype(BF16)
        q_pe = q_t[r + NOPE:r + NOPE + ROPE, :] * cos_t + q_t[r + NOPE + ROPE:r + HEAD_PAD, :] * sin_t
        qt_out[0, r + NOPE:r + NOPE + ROPE, :] = (q_pe * Q_SCALE).astype(BF16)
        qt_out[0, r + NOPE + ROPE:r + HEAD_PAD, :] = jnp.zeros((ROPE, tm), BF16)

    ks, v_t = _keys_values(p, cos_ref[...], sin_ref[...], kv_g_ref[...], w_k_ref[...], w_vt_ref[...])
    for i, kk in enumerate(ks):
        k_out[0, :, i * LANES:(i + 1) * LANES] = kk
    vt_out[0, 0] = v_t

    ga_out[0] = _silu(p[:, 512:1024]).astype(BF16)

    @pl.when(pl.program_id(1) == 0)
    def _():
        gbuf[0:8, :] = ginit_ref[8:16, :]

    g = p[:, 1536:2048] * p[:, 2048:2560]
    gbuf[8:8 + tm, :] = g
    g1 = gbuf[7:7 + tm, :]
    g2 = gbuf[6:6 + tm, :]
    cw = conv_w_ref[...]
    conv = cw[0:1, :] * g2 + cw[1:2, :] * g1 + cw[2:3, :] * g
    gbuf[0:8, :] = gbuf[tm:tm + 8, :]
    yc = p[:, 1024:1536] * conv
    sq = yc * yc
    hi = sq.astype(BF16)
    lo = (sq - hi.astype(F32)).astype(BF16)
    gm = gmat_ref[...]
    ssum = _dot(hi, gm) + _dot(lo, gm)
    ycn = yc * lax.rsqrt(ssum * (1.0 / CONV_GROUP) + EPS) * conv_g_ref[...]
    yc_out[0] = (ycn * _silu(p[:, 2560:3072])).astype(BF16)


def _attn_kernel(qt_ref, k_ref, vt_ref, km_ref, vtm_ref, ga_ref, yc_ref, x_ref, w_out_ref,
                 attn_g_ref, final_g_ref, o_ref, m_scr, l_scr, acc_scr, y_scr):
    i = pl.program_id(1)
    tri_full = (lax.broadcasted_iota(jnp.int32, (HALF, TILE), 0)
                <= lax.broadcasted_iota(jnp.int32, (HALF, TILE), 1))
    tri_half = tri_full[:, :HALF]

    def q_of(h, lo=0, hi=TILE):
        return qt_ref[0, h * HEAD_PAD:(h + 1) * HEAD_PAD, lo:hi]

    def update(h, s, v_t, lo=0, hi=TILE):
        m_old = m_scr[h, :, lo:hi]
        m_new = jnp.maximum(m_old, jnp.max(s, axis=0, keepdims=True))
        alpha = jnp.exp2(m_old - m_new)
        pr = jnp.exp2(s - m_new)
        l_scr[h, :, lo:hi] = alpha * l_scr[h, :, lo:hi] + jnp.sum(pr, axis=0, keepdims=True)
        acc_scr[h, :, lo:hi] = alpha * acc_scr[h, :, lo:hi] + _dot(v_t, pr.astype(BF16))
        m_scr[h, :, lo:hi] = m_new

    for h in range(HEADS):
        s = _dot(km_ref[:, h * HEAD_PAD:(h + 1) * HEAD_PAD], q_of(h))
        m = jnp.max(s, axis=0, keepdims=True)
        pr = jnp.exp2(s - m)
        m_scr[h] = m
        l_scr[h] = jnp.sum(pr, axis=0, keepdims=True)
        acc_scr[h] = _dot(vtm_ref[h * VDIM:(h + 1) * VDIM, :], pr.astype(BF16))

    def full_block(j, carry):
        start = pl.multiple_of(j * TILE, TILE)
        for h in range(HEADS):
            k = k_ref[0, pl.ds(start, TILE), h * HEAD_PAD:(h + 1) * HEAD_PAD]
            update(h, _dot(k, q_of(h)), vt_ref[0, j, h * VDIM:(h + 1) * VDIM, :])
        return carry

    lax.fori_loop(0, i, full_block, 0)

    d0 = pl.multiple_of(i * TILE, TILE)
    d1 = pl.multiple_of(i * TILE + HALF, HALF)
    for h in range(HEADS):
        k = k_ref[0, pl.ds(d0, HALF), h * HEAD_PAD:(h + 1) * HEAD_PAD]
        s = jnp.where(tri_full, _dot(k, q_of(h)), NEG_INF)
        update(h, s, vt_ref[0, i, h * VDIM:(h + 1) * VDIM, 0:HALF])
    for h in range(HEADS):
        k = k_ref[0, pl.ds(d1, HALF), h * HEAD_PAD:(h + 1) * HEAD_PAD]
        s = jnp.where(tri_half, _dot(k, q_of(h, HALF, TILE)), NEG_INF)
        update(h, s, vt_ref[0, i, h * VDIM:(h + 1) * VDIM, HALF:TILE], HALF, TILE)

    for h in range(HEADS):
        o_t = acc_scr[h] * (1.0 / l_scr[h])
        o = _rms(o_t.T, attn_g_ref[:, h * VDIM:(h + 1) * VDIM])
        y_scr[:, h * VDIM:(h + 1) * VDIM] = (o * ga_ref[0, :, h * VDIM:(h + 1) * VDIM].astype(F32)).astype(BF16)

    y_scr[:, HEADS * VDIM:] = yc_ref[0]
    mix = _dot(y_scr[...], w_out_ref[...])
    o_ref[0] = _rms(x_ref[0] + mix, final_g_ref[...])


def _rope_angles(pos):
    half = ROPE // 2
    inv_freq = 1.0 / (ROPE_THETA ** (jnp.arange(half, dtype=F32) / half))
    ang = pos.astype(F32)[:, None] * inv_freq[None, :]
    return jnp.cos(ang), jnp.sin(ang)


def _swap_halves(w):
    half = w.shape[-1] // 2
    return jnp.concatenate([w[..., half:], w[..., :half]], axis=-1)


def _full(shape):
    return pl.BlockSpec(shape, lambda *_: (0,) * len(shape))


def _layer(x, meta_tokens, norm_g, w_in, q_norm_g, w_q_up, kv_norm_g, w_kv_up, conv_w,
           attn_out_g, conv_out_g, w_out, final_norm_g):
    B, S, D = x.shape
    assert D == D_MODEL and S % TILE == 0
    assert meta_tokens.shape == (N_META, D_MODEL)
    nblk = S // TILE

    k_rope_w = w_in[:, 384:448]
    w_in_p = jnp.concatenate([w_in[:, :448], _swap_halves(k_rope_w), w_in[:, 448:]], axis=1).astype(BF16)
    wq = w_q_up.reshape(Q_LORA, HEADS, NOPE + ROPE)
    wq_t = jnp.concatenate([wq, _swap_halves(wq[..., NOPE:])], axis=-1).reshape(Q_LORA, HEADS * HEAD_PAD).T.astype(BF16)
    wkv = w_kv_up.reshape(KV_LORA, HEADS, NOPE + VDIM)
    w_k = wkv[..., :NOPE].reshape(KV_LORA, HEADS * NOPE).astype(BF16)
    w_vt = wkv[..., NOPE:].reshape(KV_LORA, HEADS * VDIM).T.astype(BF16)
    w_out_b = w_out.astype(BF16)
    gid = np.arange(CONV_W) // CONV_GROUP
    gmat = jnp.asarray((gid[:, None] == gid[None, :]).astype(np.float32), dtype=BF16)
    row = lambda v: v.reshape(1, -1).astype(F32)

    zpad = lambda n: jnp.zeros((n, LANES - ROPE), F32)
    c_m, s_m = _rope_angles(jnp.arange(N_META))
    cos_m = jnp.concatenate([c_m, c_m, zpad(N_META)], axis=-1)
    sin_m = jnp.concatenate([-s_m, s_m, zpad(N_META)], axis=-1)
    c_r, s_r = _rope_angles(N_META + jnp.arange(S))
    cos_r = jnp.concatenate([c_r, c_r, zpad(S)], axis=-1)
    sin_r = jnp.concatenate([-s_r, s_r, zpad(S)], axis=-1)
    cos_t = jnp.concatenate([c_r, c_r], axis=-1).T
    sin_t = jnp.concatenate([-s_r, s_r], axis=-1).T

    k_meta, vt_meta, g_meta = pl.pallas_call(
        _meta_kernel,
        out_shape=(jax.ShapeDtypeStruct((N_META, HEADS * HEAD_PAD), BF16),
                   jax.ShapeDtypeStruct((HEADS * VDIM, N_META), BF16),
                   jax.ShapeDtypeStruct((N_META, CONV_W), F32)),
        name="meta_proj",
    )(meta_tokens.astype(F32), cos_m, sin_m, row(norm_g), w_in_p, row(kv_norm_g), w_k, w_vt)

    tok = lambda w: pl.BlockSpec((1, TILE, w), lambda b, t: (b, t, 0))
    qt_s, k_s, vt_s, ga_s, yc_s = pl.pallas_call(
        _proj_kernel,
        grid=(B, nblk),
        in_specs=[tok(D_MODEL),
                  pl.BlockSpec((TILE, LANES), lambda b, t: (t, 0)),
                  pl.BlockSpec((TILE, LANES), lambda b, t: (t, 0)),
                  pl.BlockSpec((ROPE, TILE), lambda b, t: (0, t)),
                  pl.BlockSpec((ROPE, TILE), lambda b, t: (0, t)),
                  _full((N_META, CONV_W)), _full((1, D_MODEL)), _full((D_MODEL, P_COLS)),
                  _full((1, Q_LORA)), _full((HEADS * HEAD_PAD, Q_LORA)),
                  _full((1, KV_LORA)), _full((KV_LORA, HEADS * NOPE)), _full((HEADS * VDIM, KV_LORA)),
                  _full((3, CONV_W)), _full((1, CONV_W)), _full((CONV_W, CONV_W))],
        out_specs=[pl.BlockSpec((1, HEADS * HEAD_PAD, TILE), lambda b, t: (b, 0, t)),
                   tok(HEADS * HEAD_PAD),
                   pl.BlockSpec((1, 1, HEADS * VDIM, TILE), lambda b, t: (b, t, 0, 0)),
                   tok(HEADS * VDIM), tok(CONV_W)],
        out_shape=[jax.ShapeDtypeStruct((B, HEADS * HEAD_PAD, S), BF16),
                   jax.ShapeDtypeStruct((B, S, HEADS * HEAD_PAD), BF16),
                   jax.ShapeDtypeStruct((B, nblk, HEADS * VDIM, TILE), BF16),
                   jax.ShapeDtypeStruct((B, S, HEADS * VDIM), BF16),
                   jax.ShapeDtypeStruct((B, S, CONV_W), BF16)],
        scratch_shapes=[pltpu.VMEM((TILE + 8, CONV_W), F32)],
        compiler_params=pltpu.CompilerParams(dimension_semantics=("arbitrary", "arbitrary")),
        name="proj",
    )(x, cos_r, sin_r, cos_t, sin_t, g_meta, row(norm_g), w_in_p, row(q_norm_g), wq_t,
      row(kv_norm_g), w_k, w_vt, conv_w.astype(F32), row(conv_out_g), gmat)

    qtile = lambda w: pl.BlockSpec((1, TILE, w), lambda b, i: (b, i, 0))
    out = pl.pallas_call(
        _attn_kernel,
        grid=(B, nblk),
        in_specs=[pl.BlockSpec((1, HEADS * HEAD_PAD, TILE), lambda b, i: (b, 0, i)),
                  pl.BlockSpec((1, S, HEADS * HEAD_PAD), lambda b, i: (b, 0, 0)),
                  pl.BlockSpec((1, nblk, HEADS * VDIM, TILE), lambda b, i: (b, 0, 0, 0)),
                  _full((N_META, HEADS * HEAD_PAD)), _full((HEADS * VDIM, N_META)),
                  qtile(HEADS * VDIM), qtile(CONV_W), qtile(D_MODEL),
                  _full((D_MODEL, D_MODEL)), _full((1, HEADS * VDIM)), _full((1, D_MODEL))],
        out_specs=qtile(D_MODEL),
        out_shape=jax.ShapeDtypeStruct((B, S, D_MODEL), F32),
        scratch_shapes=[pltpu.VMEM((HEADS, 1, TILE), F32), pltpu.VMEM((HEADS, 1, TILE), F32),
                        pltpu.VMEM((HEADS, VDIM, TILE), F32), pltpu.VMEM((TILE, D_MODEL), BF16)],
        compiler_params=pltpu.CompilerParams(dimension_semantics=("arbitrary", "arbitrary")),
        name="attn_out",
    )(qt_s, k_s, vt_s, k_meta, vt_meta, ga_s, yc_s, x, w_out_b, row(attn_out_g), row(final_norm_g))
    return out


def kernel(x, meta_tokens, norm_g, w_in, q_norm_g, w_q_up, kv_norm_g, w_kv_up, conv_w,
           attn_out_g, conv_out_g, w_out, final_norm_g):
    assert norm_g.shape[0] == 1, "single-layer block"
    return _layer(x, meta_tokens, norm_g[0], w_in[0], q_norm_g[0], w_q_up[0], kv_norm_g[0],
                  w_kv_up[0], conv_w[0], attn_out_g[0], conv_out_g[0], w_out[0], final_norm_g)
```

```python
import math

import jax
import jax.numpy as jnp
import numpy as np
from jax import lax
from jax.experimental import pallas as pl
from jax.experimental.pallas import tpu as pltpu

F32 = jnp.float32
BF16 = jnp.bfloat16

D_MODEL = 1024
N_META = 16
HEADS = 4
NOPE = 128
ROPE = 64
VDIM = 128
Q_LORA = 256
KV_LORA = 128
CONV_W = 512
CONV_GROUP = 64
ROPE_THETA = 10000.0
ATTN_SCALE = (NOPE + ROPE) ** -0.5
NEG_INF = -1e30
EPS = 1e-6

LANES = 128
HEAD_PAD = 2 * LANES
P_COLS = 3072

TILE = 512
HALF = TILE // 2
QK_AHEAD = 2
Q_SCALE = ATTN_SCALE * math.log2(math.e)


def _rms(x, g):
    ms = jnp.mean(x * x, axis=-1, keepdims=True)
    return x * lax.rsqrt(ms + EPS) * g


def _silu(x):
    return x * (1.0 / (1.0 + jnp.exp(-x)))


def _dot(a, b):
    return jnp.dot(a, b, preferred_element_type=F32)


def _dot_nt(a, b):
    return lax.dot_general(a, b, (((1,), (1,)), ((), ())), preferred_element_type=F32)


def _in_proj(x, norm_g, w_in):
    u = _rms(x, norm_g).astype(BF16)
    return _dot(u, w_in)


def _keys_values(p, cos, sin, kv_g, w_k, w_vt):
    c_kv = _rms(p[:, Q_LORA:Q_LORA + KV_LORA], kv_g).astype(BF16)
    k_nope = _dot(c_kv, w_k)
    v_t = _dot_nt(w_vt, c_kv)
    kr = p[:, 384:512]
    k_pe = (kr * cos + pltpu.roll(kr, ROPE, 1) * sin).astype(BF16)
    ks = []
    for h in range(HEADS):
        ks.append(k_nope[:, h * NOPE:(h + 1) * NOPE].astype(BF16))
        ks.append(k_pe)
    return ks, v_t.astype(BF16)


def _meta_kernel(x_ref, cos_ref, sin_ref, norm_g_ref, w_in_ref, kv_g_ref, w_k_ref, w_vt_ref,
                 k_out, vt_out, g_out):
    p = _in_proj(x_ref[...], norm_g_ref[...], w_in_ref[...])
    ks, v_t = _keys_values(p, cos_ref[...], sin_ref[...], kv_g_ref[...], w_k_ref[...], w_vt_ref[...])
    for i, kk in enumerate(ks):
        k_out[:, i * LANES:(i + 1) * LANES] = kk
    vt_out[...] = v_t
    g_out[...] = p[:, 2048:2560] * p[:, 1536:2048]


def _proj_kernel(x_ref, cos_ref, sin_ref, cos_t_ref, sin_t_ref, ginit_ref, norm_g_ref, w_in_ref,
                 q_g_ref, w_qt_ref, kv_g_ref, w_k_ref, w_vt_ref, conv_w_ref, conv_g_ref, gmat_ref,
                 qt_out, k_out, vt_out, ga_out, yc_out, gbuf):
    tm = x_ref.shape[1]
    p = _in_proj(x_ref[0], norm_g_ref[...], w_in_ref[...])

    c_q = _rms(p[:, 0:Q_LORA], q_g_ref[...]).astype(BF16)
    q_t = _dot_nt(w_qt_ref[...], c_q)
    cos_t = cos_t_ref[...]
    sin_t = sin_t_ref[...]
    for h in range(HEADS):
        r = h * HEAD_PAD
        qt_out[0, r:r + NOPE, :] = (q_t[r:r + NOPE, :] * Q_SCALE).astype(BF16)
        q_pe = q_t[r + NOPE:r + NOPE + ROPE, :] * cos_t + q_t[r + NOPE + ROPE:r + HEAD_PAD, :] * sin_t
        qt_out[0, r + NOPE:r + NOPE + ROPE, :] = (q_pe * Q_SCALE).astype(BF16)
        qt_out[0, r + NOPE + ROPE:r + HEAD_PAD, :] = jnp.zeros((ROPE, tm), BF16)

    ks, v_t = _keys_values(p, cos_ref[...], sin_ref[...], kv_g_ref[...], w_k_ref[...], w_vt_ref[...])
    for i, kk in enumerate(ks):
        k_out[0, :, i * LANES:(i + 1) * LANES] = kk
    vt_out[0, 0] = v_t

    ga_out[0] = _silu(p[:, 512:1024]).astype(BF16)

    @pl.when(pl.program_id(1) == 0)
    def _():
        gbuf[0:8, :] = ginit_ref[8:16, :]

    g = p[:, 1536:2048] * p[:, 2048:2560]
    gbuf[8:8 + tm, :] = g
    g1 = gbuf[7:7 + tm, :]
    g2 = gbuf[6:6 + tm, :]
    cw = conv_w_ref[...]
    conv = cw[0:1, :] * g2 + cw[1:2, :] * g1 + cw[2:3, :] * g
    gbuf[0:8, :] = gbuf[tm:tm + 8, :]
    yc = p[:, 1024:1536] * conv
    sq = yc * yc
    hi = sq.astype(BF16)
    lo = (sq - hi.astype(F32)).astype(BF16)
    gm = gmat_ref[...]
    ssum = _dot(hi, gm) + _dot(lo, gm)
    ycn = yc * lax.rsqrt(ssum * (1.0 / CONV_GROUP) + EPS) * conv_g_ref[...]
    yc_out[0] = (ycn * _silu(p[:, 2560:3072])).astype(BF16)


def _attention_tile(c, qt_ref, k_ref, vt_ref, km_ref, vtm_ref, m_scr, l_scr, acc_scr):
    tri_full = (lax.broadcasted_iota(jnp.int32, (HALF, TILE), 0)
                <= lax.broadcasted_iota(jnp.int32, (HALF, TILE), 1))
    tri_half = tri_full[:, :HALF]

    def q_of(h, lo, hi):
        return qt_ref[0, h * HEAD_PAD:(h + 1) * HEAD_PAD, lo:hi]

    def k_of(h, r0, r1):
        return k_ref[0, r0:r1, h * HEAD_PAD:(h + 1) * HEAD_PAD]

    def v_of(h, j, lo, hi):
        return vt_ref[0, j, h * VDIM:(h + 1) * VDIM, lo:hi]

    tasks = []
    for h in range(HEADS):
        tasks.append((h, lambda h=h: km_ref[:, h * HEAD_PAD:(h + 1) * HEAD_PAD],
                      lambda h=h: vtm_ref[h * VDIM:(h + 1) * VDIM, :], 0, TILE, None, True))
    for j in range(c):
        for h in range(HEADS):
            tasks.append((h, lambda h=h, j=j: k_of(h, j * TILE, (j + 1) * TILE),
                          lambda h=h, j=j: v_of(h, j, 0, TILE), 0, TILE, None, False))
    for h in range(HEADS):
        tasks.append((h, lambda h=h: k_of(h, c * TILE, c * TILE + HALF),
                      lambda h=h: v_of(h, c, 0, HALF), 0, TILE, tri_full, False))
    for h in range(HEADS):
        tasks.append((h, lambda h=h: k_of(h, c * TILE + HALF, (c + 1) * TILE),
                      lambda h=h: v_of(h, c, HALF, TILE), HALF, TILE, tri_half, False))

    def scores(task):
        h, k_fn, _, lo, hi, mask, _ = task
        s = _dot(k_fn(), q_of(h, lo, hi))
        return s if mask is None else jnp.where(mask, s, NEG_INF)

    def softmax_pv(task, s):
        h, _, v_fn, lo, hi, _, first = task
        blk_max = jnp.max(s, axis=0, keepdims=True)
        if first:
            m_new = blk_max
        else:
            m_old = m_scr[h, :, lo:hi]
            m_new = jnp.maximum(m_old, blk_max)
            alpha = jnp.exp2(m_old - m_new)
        pr = jnp.exp2(s - m_new)
        row_sum = jnp.sum(pr, axis=0, keepdims=True)
        pv = _dot(v_fn(), pr.astype(BF16))
        if first:
            l_scr[h, :, lo:hi] = row_sum
            acc_scr[h, :, lo:hi] = pv
        else:
            l_scr[h, :, lo:hi] = alpha * l_scr[h, :, lo:hi] + row_sum
            acc_scr[h, :, lo:hi] = alpha * acc_scr[h, :, lo:hi] + pv
        m_scr[h, :, lo:hi] = m_new

    pending = [scores(t) for t in tasks[:QK_AHEAD]]
    for n, task in enumerate(tasks):
        if n + QK_AHEAD < len(tasks):
            pending.append(scores(tasks[n + QK_AHEAD]))
        softmax_pv(task, pending.pop(0))


def _attn_kernel(qt_ref, k_ref, vt_ref, km_ref, vtm_ref, ga_ref, yc_ref, x_ref, w_out_ref,
                 attn_g_ref, final_g_ref, o_ref, m_scr, l_scr, acc_scr, y_scr):
    nblk = k_ref.shape[1] // TILE
    for c in range(nblk):
        @pl.when(pl.program_id(1) == c)
        def _(c=c):
            _attention_tile(c, qt_ref, k_ref, vt_ref, km_ref, vtm_ref, m_scr, l_scr, acc_scr)

    for h in range(HEADS):
        o_t = acc_scr[h] * (1.0 / l_scr[h])
        o = _rms(o_t.T, attn_g_ref[:, h * VDIM:(h + 1) * VDIM])
        y_scr[:, h * VDIM:(h + 1) * VDIM] = (o * ga_ref[0, :, h * VDIM:(h + 1) * VDIM].astype(F32)).astype(BF16)

    y_scr[:, HEADS * VDIM:] = yc_ref[0]
    mix = _dot(y_scr[...], w_out_ref[...])
    o_ref[0] = _rms(x_ref[0] + mix, final_g_ref[...])


def _rope_angles(pos):
    half = ROPE // 2
    inv_freq = 1.0 / (ROPE_THETA ** (jnp.arange(half, dtype=F32) / half))
    ang = pos.astype(F32)[:, None] * inv_freq[None, :]
    return jnp.cos(ang), jnp.sin(ang)


def _swap_halves(w):
    half = w.shape[-1] // 2
    return jnp.concatenate([w[..., half:], w[..., :half]], axis=-1)


def _full(shape):
    return pl.BlockSpec(shape, lambda *_: (0,) * len(shape))


def _layer(x, meta_tokens, norm_g, w_in, q_norm_g, w_q_up, kv_norm_g, w_kv_up, conv_w,
           attn_out_g, conv_out_g, w_out, final_norm_g):
    B, S, D = x.shape
    assert D == D_MODEL and S % TILE == 0
    assert meta_tokens.shape == (N_META, D_MODEL)
    nblk = S // TILE

    k_rope_w = w_in[:, 384:448]
    w_in_p = jnp.concatenate([w_in[:, :448], _swap_halves(k_rope_w), w_in[:, 448:]], axis=1).astype(BF16)
    wq = w_q_up.reshape(Q_LORA, HEADS, NOPE + ROPE)
    wq_t = jnp.concatenate([wq, _swap_halves(wq[..., NOPE:])], axis=-1).reshape(Q_LORA, HEADS * HEAD_PAD).T.astype(BF16)
    wkv = w_kv_up.reshape(KV_LORA, HEADS, NOPE + VDIM)
    w_k = wkv[..., :NOPE].reshape(KV_LORA, HEADS * NOPE).astype(BF16)
    w_vt = wkv[..., NOPE:].reshape(KV_LORA, HEADS * VDIM).T.astype(BF16)
    w_out_b = w_out.astype(BF16)
    gid = np.arange(CONV_W) // CONV_GROUP
    gmat = jnp.asarray((gid[:, None] == gid[None, :]).astype(np.float32), dtype=BF16)
    row = lambda v: v.reshape(1, -1).astype(F32)

    zpad = lambda n: jnp.zeros((n, LANES - ROPE), F32)
    c_m, s_m = _rope_angles(jnp.arange(N_META))
    cos_m = jnp.concatenate([c_m, c_m, zpad(N_META)], axis=-1)
    sin_m = jnp.concatenate([-s_m, s_m, zpad(N_META)], axis=-1)
    c_r, s_r = _rope_angles(N_META + jnp.arange(S))
    cos_r = jnp.concatenate([c_r, c_r, zpad(S)], axis=-1)
    sin_r = jnp.concatenate([-s_r, s_r, zpad(S)], axis=-1)
    cos_t = jnp.concatenate([c_r, c_r], axis=-1).T
    sin_t = jnp.concatenate([-s_r, s_r], axis=-1).T

    k_meta, vt_meta, g_meta = pl.pallas_call(
        _meta_kernel,
        out_shape=(jax.ShapeDtypeStruct((N_META, HEADS * HEAD_PAD), BF16),
                   jax.ShapeDtypeStruct((HEADS * VDIM, N_META), BF16),
                   jax.ShapeDtypeStruct((N_META, CONV_W), F32)),
        name="meta_proj",
    )(meta_tokens.astype(F32), cos_m, sin_m, row(norm_g), w_in_p, row(kv_norm_g), w_k, w_vt)

    tok = lambda w: pl.BlockSpec((1, TILE, w), lambda b, t: (b, t, 0))
    qt_s, k_s, vt_s, ga_s, yc_s = pl.pallas_call(
        _proj_kernel,
        grid=(B, nblk),
        in_specs=[tok(D_MODEL),
                  pl.BlockSpec((TILE, LANES), lambda b, t: (t, 0)),
                  pl.BlockSpec((TILE, LANES), lambda b, t: (t, 0)),
                  pl.BlockSpec((ROPE, TILE), lambda b, t: (0, t)),
                  pl.BlockSpec((ROPE, TILE), lambda b, t: (0, t)),
                  _full((N_META, CONV_W)), _full((1, D_MODEL)), _full((D_MODEL, P_COLS)),
                  _full((1, Q_LORA)), _full((HEADS * HEAD_PAD, Q_LORA)),
                  _full((1, KV_LORA)), _full((KV_LORA, HEADS * NOPE)), _full((HEADS * VDIM, KV_LORA)),
                  _full((3, CONV_W)), _full((1, CONV_W)), _full((CONV_W, CONV_W))],
        out_specs=[pl.BlockSpec((1, HEADS * HEAD_PAD, TILE), lambda b, t: (b, 0, t)),
                   tok(HEADS * HEAD_PAD),
                   pl.BlockSpec((1, 1, HEADS * VDIM, TILE), lambda b, t: (b, t, 0, 0)),
                   tok(HEADS * VDIM), tok(CONV_W)],
        out_shape=[jax.ShapeDtypeStruct((B, HEADS * HEAD_PAD, S), BF16),
                   jax.ShapeDtypeStruct((B, S, HEADS * HEAD_PAD), BF16),
                   jax.ShapeDtypeStruct((B, nblk, HEADS * VDIM, TILE), BF16),
                   jax.ShapeDtypeStruct((B, S, HEADS * VDIM), BF16),
                   jax.ShapeDtypeStruct((B, S, CONV_W), BF16)],
        scratch_shapes=[pltpu.VMEM((TILE + 8, CONV_W), F32)],
        compiler_params=pltpu.CompilerParams(dimension_semantics=("arbitrary", "arbitrary")),
        name="proj",
    )(x, cos_r, sin_r, cos_t, sin_t, g_meta, row(norm_g), w_in_p, row(q_norm_g), wq_t,
      row(kv_norm_g), w_k, w_vt, conv_w.astype(F32), row(conv_out_g), gmat)

    qtile = lambda w: pl.BlockSpec((1, TILE, w), lambda b, i: (b, i, 0))
    out = pl.pallas_call(
        _attn_kernel,
        grid=(B, nblk),
        in_specs=[pl.BlockSpec((1, HEADS * HEAD_PAD, TILE), lambda b, i: (b, 0, i)),
                  pl.BlockSpec((1, S, HEADS * HEAD_PAD), lambda b, i: (b, 0, 0)),
                  pl.BlockSpec((1, nblk, HEADS * VDIM, TILE), lambda b, i: (b, 0, 0, 0)),
                  _full((N_META, HEADS * HEAD_PAD)), _full((HEADS * VDIM, N_META)),
                  qtile(HEADS * VDIM), qtile(CONV_W), qtile(D_MODEL),
                  _full((D_MODEL, D_MODEL)), _full((1, HEADS * VDIM)), _full((1, D_MODEL))],
        out_specs=qtile(D_MODEL),
        out_shape=jax.ShapeDtypeStruct((B, S, D_MODEL), F32),
        scratch_shapes=[pltpu.VMEM((HEADS, 1, TILE), F32), pltpu.VMEM((HEADS, 1, TILE), F32),
                        pltpu.VMEM((HEADS, VDIM, TILE), F32), pltpu.VMEM((TILE, D_MODEL), BF16)],
        compiler_params=pltpu.CompilerParams(dimension_semantics=("arbitrary", "arbitrary")),
        name="attn_out",
    )(qt_s, k_s, vt_s, k_meta, vt_meta, ga_s, yc_s, x, w_out_b, row(attn_out_g), row(final_norm_g))
    return out


def kernel(x, meta_tokens, norm_g, w_in, q_norm_g, w_q_up, kv_norm_g, w_kv_up, conv_w,
           attn_out_g, conv_out_g, w_out, final_norm_g):
    assert norm_g.shape[0] == 1, "single-layer block"
    return _layer(x, meta_tokens, norm_g[0], w_in[0], q_norm_g[0], w_q_up[0], kv_norm_g[0],
                  w_kv_up[0], conv_w[0], attn_out_g[0], conv_out_g[0], w_out[0], final_norm_g)
```

```python
import math

import jax
import jax.numpy as jnp
import numpy as np
from jax import lax
from jax.experimental import pallas as pl
from jax.experimental.pallas import tpu as pltpu

F32 = jnp.float32
BF16 = jnp.bfloat16

D_MODEL = 1024
N_META = 16
HEADS = 4
NOPE = 128
ROPE = 64
VDIM = 128
Q_LORA = 256
KV_LORA = 128
CONV_W = 512
CONV_GROUP = 64
ROPE_THETA = 10000.0
ATTN_SCALE = (NOPE + ROPE) ** -0.5
NEG_INF = -1e30
EPS = 1e-6

LANES = 128
HEAD_PAD = 2 * LANES
P_COLS = 3072

TILE = 512
HALF = TILE // 2
QK_AHEAD = 2
ONES_ROWS = 16
REF_ROW = NOPE + ROPE
REF_ROWS = 16
RISE_LIMIT = 100.0
Q_SCALE = ATTN_SCALE * math.log2(math.e)


def _rms(x, g):
    ms = jnp.mean(x * x, axis=-1, keepdims=True)
    return x * lax.rsqrt(ms + EPS) * g


def _silu(x):
    return x * (1.0 / (1.0 + jnp.exp(-x)))


def _dot(a, b):
    return jnp.dot(a, b, preferred_element_type=F32)


def _dot_nt(a, b):
    return lax.dot_general(a, b, (((1,), (1,)), ((), ())), preferred_element_type=F32)


def _in_proj(x, norm_g, w_in):
    u = _rms(x, norm_g).astype(BF16)
    return _dot(u, w_in)


def _keys_values(p, cos, sin, kv_g, w_k, w_vt):
    c_kv = _rms(p[:, Q_LORA:Q_LORA + KV_LORA], kv_g).astype(BF16)
    k_nope = _dot(c_kv, w_k)
    v_t = _dot_nt(w_vt, c_kv)
    kr = p[:, 384:512]
    k_pe = kr * cos + pltpu.roll(kr, ROPE, 1) * sin
    lane = lax.broadcasted_iota(jnp.int32, k_pe.shape, 1)
    k_pe = jnp.where(lane == REF_ROW - NOPE, 1.0, k_pe).astype(BF16)
    ks = []
    for h in range(HEADS):
        ks.append(k_nope[:, h * NOPE:(h + 1) * NOPE].astype(BF16))
        ks.append(k_pe)
    return ks, v_t.astype(BF16)


def _meta_kernel(x_ref, cos_ref, sin_ref, norm_g_ref, w_in_ref, kv_g_ref, w_k_ref, w_vt_ref,
                 k_out, vt_out, g_out):
    p = _in_proj(x_ref[...], norm_g_ref[...], w_in_ref[...])
    ks, v_t = _keys_values(p, cos_ref[...], sin_ref[...], kv_g_ref[...], w_k_ref[...], w_vt_ref[...])
    for i, kk in enumerate(ks):
        k_out[:, i * LANES:(i + 1) * LANES] = kk
    vt_out[...] = v_t
    g_out[...] = p[:, 2048:2560] * p[:, 1536:2048]


def _proj_kernel(x_ref, cos_ref, sin_ref, cos_t_ref, sin_t_ref, ginit_ref, norm_g_ref, w_in_ref,
                 q_g_ref, w_qt_ref, kv_g_ref, w_k_ref, w_vt_ref, conv_w_ref, conv_g_ref, gmat_ref,
                 qt_out, k_out, vt_out, ga_out, yc_out, gbuf):
    tm = x_ref.shape[1]
    p = _in_proj(x_ref[0], norm_g_ref[...], w_in_ref[...])

    c_q = _rms(p[:, 0:Q_LORA], q_g_ref[...]).astype(BF16)
    q_t = _dot_nt(w_qt_ref[...], c_q)
    cos_t = cos_t_ref[...]
    sin_t = sin_t_ref[...]
    for h in range(HEADS):
        r = h * HEAD_PAD
        qt_out[0, r:r + NOPE, :] = (q_t[r:r + NOPE, :] * Q_SCALE).astype(BF16)
        q_pe = q_t[r + NOPE:r + NOPE + ROPE, :] * cos_t + q_t[r + NOPE + ROPE:r + HEAD_PAD, :] * sin_t
        qt_out[0, r + NOPE:r + NOPE + ROPE, :] = (q_pe * Q_SCALE).astype(BF16)
        qt_out[0, r + NOPE + ROPE:r + HEAD_PAD, :] = jnp.zeros((ROPE, tm), BF16)

    ks, v_t = _keys_values(p, cos_ref[...], sin_ref[...], kv_g_ref[...], w_k_ref[...], w_vt_ref[...])
    for i, kk in enumerate(ks):
        k_out[0, :, i * LANES:(i + 1) * LANES] = kk
    vt_out[0, 0] = v_t

    ga_out[0] = _silu(p[:, 512:1024]).astype(BF16)

    @pl.when(pl.program_id(1) == 0)
    def _():
        gbuf[0:8, :] = ginit_ref[8:16, :]

    g = p[:, 1536:2048] * p[:, 2048:2560]
    gbuf[8:8 + tm, :] = g
    g1 = gbuf[7:7 + tm, :]
    g2 = gbuf[6:6 + tm, :]
    cw = conv_w_ref[...]
    conv = cw[0:1, :] * g2 + cw[1:2, :] * g1 + cw[2:3, :] * g
    gbuf[0:8, :] = gbuf[tm:tm + 8, :]
    yc = p[:, 1024:1536] * conv
    sq = yc * yc
    hi = sq.astype(BF16)
    lo = (sq - hi.astype(F32)).astype(BF16)
    gm = gmat_ref[...]
    ssum = _dot(hi, gm) + _dot(lo, gm)
    ycn = yc * lax.rsqrt(ssum * (1.0 / CONV_GROUP) + EPS) * conv_g_ref[...]
    yc_out[0] = (ycn * _silu(p[:, 2560:3072])).astype(BF16)


def _ones_rows(v_t):
    return jnp.concatenate([v_t, jnp.ones((ONES_ROWS, v_t.shape[1]), BF16)], axis=0)


def _bf16_exact(x):
    return x.astype(BF16).astype(F32)


def _attention_tile(c, qt_ref, k_ref, vt_ref, km_ref, vtm_ref, q_aug, m_scr, acc_scr, p_scr, rise_scr):
    tri_full = (lax.broadcasted_iota(jnp.int32, (HALF, TILE), 0)
                <= lax.broadcasted_iota(jnp.int32, (HALF, TILE), 1))
    tri_half = tri_full[:, :HALF]

    def k_of(h, r0, r1):
        return k_ref[0, r0:r1, h * HEAD_PAD:(h + 1) * HEAD_PAD]

    def v_of(h, j, lo, hi):
        return vt_ref[0, j, h * VDIM:(h + 1) * VDIM, lo:hi]

    tasks = []
    for h in range(HEADS):
        tasks.append((h, lambda h=h: km_ref[:, h * HEAD_PAD:(h + 1) * HEAD_PAD],
                      lambda h=h: vtm_ref[h * VDIM:(h + 1) * VDIM, :], 0, TILE, None, True))
    for j in range(c):
        for h in range(HEADS):
            tasks.append((h, lambda h=h, j=j: k_of(h, j * TILE, (j + 1) * TILE),
                          lambda h=h, j=j: v_of(h, j, 0, TILE), 0, TILE, None, False))
    for h in range(HEADS):
        tasks.append((h, lambda h=h: k_of(h, c * TILE, c * TILE + HALF),
                      lambda h=h: v_of(h, c, 0, HALF), 0, TILE, tri_full, False))
    for h in range(HEADS):
        tasks.append((h, lambda h=h: k_of(h, c * TILE + HALF, (c + 1) * TILE),
                      lambda h=h: v_of(h, c, HALF, TILE), HALF, TILE, tri_half, False))
    assert QK_AHEAD < HEADS and all(t[0] == n % HEADS for n, t in enumerate(tasks))

    for h in range(HEADS):
        q_aug[h] = qt_ref[0, h * HEAD_PAD:(h + 1) * HEAD_PAD, :]
    rise_scr[...] = jnp.zeros(rise_scr.shape, F32)

    def scores(task):
        h, k_fn, _, lo, hi, mask, _ = task
        s = _dot(k_fn(), q_aug[h, :, lo:hi])
        return s if mask is None else jnp.where(mask, s, NEG_INF)

    def softmax(n, task, s):
        h, _, _, lo, hi, _, first = task
        nkeys, ncols = s.shape
        blk_max = jnp.max(s, axis=0, keepdims=True)
        if first:
            m_new, beta = _bf16_exact(blk_max), None
            p = jnp.exp2(s - m_new)
        else:
            rise = jnp.maximum(blk_max, 0.0)
            p = jnp.exp2(s)
            m_old = m_scr[h, :, lo:hi]
            m_new = _bf16_exact(m_old + rise)
            beta = jnp.exp2(m_old - m_new)
            rise_scr[:, lo:hi] = jnp.maximum(rise_scr[:, lo:hi], rise)
        p_scr[n % 2, 0:nkeys, 0:ncols] = p.astype(BF16)
        m_scr[h, :, lo:hi] = m_new
        q_aug[h, REF_ROW:REF_ROW + REF_ROWS, lo:hi] = jnp.broadcast_to(-m_new, (REF_ROWS, ncols)).astype(BF16)
        return nkeys, ncols, beta

    def values(n, task, nkeys, ncols, beta):
        h, _, v_fn, lo, hi, _, _ = task
        pv = _dot(_ones_rows(v_fn()), p_scr[n % 2, 0:nkeys, 0:ncols])
        if beta is None:
            acc_scr[h, :, lo:hi] = pv
        else:
            acc_scr[h, :, lo:hi] = (acc_scr[h, :, lo:hi] + pv) * beta

    pending = [scores(t) for t in tasks[:QK_AHEAD]]
    prev = None
    for n, task in enumerate(tasks):
        if prev is not None:
            values(*prev)
        if n + QK_AHEAD < len(tasks):
            pending.append(scores(tasks[n + QK_AHEAD]))
        prev = (n, task) + softmax(n, task, pending.pop(0))
    values(*prev)


def _attention_tile_two_pass(i, qt_ref, k_ref, vt_ref, km_ref, vtm_ref, m_scr, acc_scr):
    tri_full = (lax.broadcasted_iota(jnp.int32, (HALF, TILE), 0)
                <= lax.broadcasted_iota(jnp.int32, (HALF, TILE), 1))
    tri_half = tri_full[:, :HALF]

    def q_of(h, lo, hi):
        return qt_ref[0, h * HEAD_PAD:(h + 1) * HEAD_PAD, lo:hi]

    def update(h, s, v_t, lo, hi):
        m_old = m_scr[h, :, lo:hi]
        m_new = jnp.maximum(m_old, jnp.max(s, axis=0, keepdims=True))
        pv = _dot(_ones_rows(v_t), jnp.exp2(s - m_new).astype(BF16))
        acc_scr[h, :, lo:hi] = jnp.exp2(m_old - m_new) * acc_scr[h, :, lo:hi] + pv
        m_scr[h, :, lo:hi] = m_new

    for h in range(HEADS):
        s = _dot(km_ref[:, h * HEAD_PAD:(h + 1) * HEAD_PAD], q_of(h, 0, TILE))
        m = jnp.max(s, axis=0, keepdims=True)
        m_scr[h] = m
        acc_scr[h] = _dot(_ones_rows(vtm_ref[h * VDIM:(h + 1) * VDIM, :]), jnp.exp2(s - m).astype(BF16))

    def full_block(j, carry):
        start = pl.multiple_of(j * TILE, TILE)
        for h in range(HEADS):
            k = k_ref[0, pl.ds(start, TILE), h * HEAD_PAD:(h + 1) * HEAD_PAD]
            update(h, _dot(k, q_of(h, 0, TILE)), vt_ref[0, j, h * VDIM:(h + 1) * VDIM, :], 0, TILE)
        return carry

    lax.fori_loop(0, i, full_block, 0)

    d0 = pl.multiple_of(i * TILE, TILE)
    d1 = pl.multiple_of(i * TILE + HALF, HALF)
    for h in range(HEADS):
        k = k_ref[0, pl.ds(d0, HALF), h * HEAD_PAD:(h + 1) * HEAD_PAD]
        s = jnp.where(tri_full, _dot(k, q_of(h, 0, TILE)), NEG_INF)
        update(h, s, vt_ref[0, i, h * VDIM:(h + 1) * VDIM, 0:HALF], 0, TILE)
    for h in range(HEADS):
        k = k_ref[0, pl.ds(d1, HALF), h * HEAD_PAD:(h + 1) * HEAD_PAD]
        s = jnp.where(tri_half, _dot(k, q_of(h, HALF, TILE)), NEG_INF)
        update(h, s, vt_ref[0, i, h * VDIM:(h + 1) * VDIM, HALF:TILE], HALF, TILE)


def _attn_kernel(qt_ref, k_ref, vt_ref, km_ref, vtm_ref, ga_ref, yc_ref, x_ref, w_out_ref,
                 attn_g_ref, final_g_ref, o_ref, q_aug, m_scr, acc_scr, p_scr, rise_scr, y_scr):
    nblk = k_ref.shape[1] // TILE
    for c in range(nblk):
        @pl.when(pl.program_id(1) == c)
        def _(c=c):
            _attention_tile(c, qt_ref, k_ref, vt_ref, km_ref, vtm_ref, q_aug, m_scr, acc_scr, p_scr, rise_scr)

    @pl.when(jnp.max(rise_scr[...]) > RISE_LIMIT)
    def _():
        _attention_tile_two_pass(pl.program_id(1), qt_ref, k_ref, vt_ref, km_ref, vtm_ref, m_scr, acc_scr)

    for h in range(HEADS):
        o_t = acc_scr[h, 0:VDIM, :] * (1.0 / acc_scr[h, VDIM:VDIM + 1, :])
        o = _rms(o_t.T, attn_g_ref[:, h * VDIM:(h + 1) * VDIM])
        y_scr[:, h * VDIM:(h + 1) * VDIM] = (o * ga_ref[0, :, h * VDIM:(h + 1) * VDIM].astype(F32)).astype(BF16)

    y_scr[:, HEADS * VDIM:] = yc_ref[0]
    mix = _dot(y_scr[...], w_out_ref[...])
    o_ref[0] = _rms(x_ref[0] + mix, final_g_ref[...])


def _rope_angles(pos):
    half = ROPE // 2
    inv_freq = 1.0 / (ROPE_THETA ** (jnp.arange(half, dtype=F32) / half))
    ang = pos.astype(F32)[:, None] * inv_freq[None, :]
    return jnp.cos(ang), jnp.sin(ang)


def _swap_halves(w):
    half = w.shape[-1] // 2
    return jnp.concatenate([w[..., half:], w[..., :half]], axis=-1)


def _full(shape):
    return pl.BlockSpec(shape, lambda *_: (0,) * len(shape))


def _layer(x, meta_tokens, norm_g, w_in, q_norm_g, w_q_up, kv_norm_g, w_kv_up, conv_w,
           attn_out_g, conv_out_g, w_out, final_norm_g):
    B, S, D = x.shape
    assert D == D_MODEL and S % TILE == 0
    assert meta_tokens.shape == (N_META, D_MODEL)
    nblk = S // TILE

    k_rope_w = w_in[:, 384:448]
    w_in_p = jnp.concatenate([w_in[:, :448], _swap_halves(k_rope_w), w_in[:, 448:]], axis=1).astype(BF16)
    wq = w_q_up.reshape(Q_LORA, HEADS, NOPE + ROPE)
    wq_t = jnp.concatenate([wq, _swap_halves(wq[..., NOPE:])], axis=-1).reshape(Q_LORA, HEADS * HEAD_PAD).T.astype(BF16)
    wkv = w_kv_up.reshape(KV_LORA, HEADS, NOPE + VDIM)
    w_k = wkv[..., :NOPE].reshape(KV_LORA, HEADS * NOPE).astype(BF16)
    w_vt = wkv[..., NOPE:].reshape(KV_LORA, HEADS * VDIM).T.astype(BF16)
    w_out_b = w_out.astype(BF16)
    gid = np.arange(CONV_W) // CONV_GROUP
    gmat = jnp.asarray((gid[:, None] == gid[None, :]).astype(np.float32), dtype=BF16)
    row = lambda v: v.reshape(1, -1).astype(F32)

    zpad = lambda n: jnp.zeros((n, LANES - ROPE), F32)
    c_m, s_m = _rope_angles(jnp.arange(N_META))
    cos_m = jnp.concatenate([c_m, c_m, zpad(N_META)], axis=-1)
    sin_m = jnp.concatenate([-s_m, s_m, zpad(N_META)], axis=-1)
    c_r, s_r = _rope_angles(N_META + jnp.arange(S))
    cos_r = jnp.concatenate([c_r, c_r, zpad(S)], axis=-1)
    sin_r = jnp.concatenate([-s_r, s_r, zpad(S)], axis=-1)
    cos_t = jnp.concatenate([c_r, c_r], axis=-1).T
    sin_t = jnp.concatenate([-s_r, s_r], axis=-1).T

    k_meta, vt_meta, g_meta = pl.pallas_call(
        _meta_kernel,
        out_shape=(jax.ShapeDtypeStruct((N_META, HEADS * HEAD_PAD), BF16),
                   jax.ShapeDtypeStruct((HEADS * VDIM, N_META), BF16),
                   jax.ShapeDtypeStruct((N_META, CONV_W), F32)),
        name="meta_proj",
    )(meta_tokens.astype(F32), cos_m, sin_m, row(norm_g), w_in_p, row(kv_norm_g), w_k, w_vt)

    tok = lambda w: pl.BlockSpec((1, TILE, w), lambda b, t: (b, t, 0))
    qt_s, k_s, vt_s, ga_s, yc_s = pl.pallas_call(
        _proj_kernel,
        grid=(B, nblk),
        in_specs=[tok(D_MODEL),
                  pl.BlockSpec((TILE, LANES), lambda b, t: (t, 0)),
                  pl.BlockSpec((TILE, LANES), lambda b, t: (t, 0)),
                  pl.BlockSpec((ROPE, TILE), lambda b, t: (0, t)),
                  pl.BlockSpec((ROPE, TILE), lambda b, t: (0, t)),
                  _full((N_META, CONV_W)), _full((1, D_MODEL)), _full((D_MODEL, P_COLS)),
                  _full((1, Q_LORA)), _full((HEADS * HEAD_PAD, Q_LORA)),
                  _full((1, KV_LORA)), _full((KV_LORA, HEADS * NOPE)), _full((HEADS * VDIM, KV_LORA)),
                  _full((3, CONV_W)), _full((1, CONV_W)), _full((CONV_W, CONV_W))],
        out_specs=[pl.BlockSpec((1, HEADS * HEAD_PAD, TILE), lambda b, t: (b, 0, t)),
                   tok(HEADS * HEAD_PAD),
                   pl.BlockSpec((1, 1, HEADS * VDIM, TILE), lambda b, t: (b, t, 0, 0)),
                   tok(HEADS * VDIM), tok(CONV_W)],
        out_shape=[jax.ShapeDtypeStruct((B, HEADS * HEAD_PAD, S), BF16),
                   jax.ShapeDtypeStruct((B, S, HEADS * HEAD_PAD), BF16),
                   jax.ShapeDtypeStruct((B, nblk, HEADS * VDIM, TILE), BF16),
                   jax.ShapeDtypeStruct((B, S, HEADS * VDIM), BF16),
                   jax.ShapeDtypeStruct((B, S, CONV_W), BF16)],
        scratch_shapes=[pltpu.VMEM((TILE + 8, CONV_W), F32)],
        compiler_params=pltpu.CompilerParams(dimension_semantics=("arbitrary", "arbitrary")),
        name="proj",
    )(x, cos_r, sin_r, cos_t, sin_t, g_meta, row(norm_g), w_in_p, row(q_norm_g), wq_t,
      row(kv_norm_g), w_k, w_vt, conv_w.astype(F32), row(conv_out_g), gmat)

    qtile = lambda w: pl.BlockSpec((1, TILE, w), lambda b, i: (b, i, 0))
    out = pl.pallas_call(
        _attn_kernel,
        grid=(B, nblk),
        in_specs=[pl.BlockSpec((1, HEADS * HEAD_PAD, TILE), lambda b, i: (b, 0, i)),
                  pl.BlockSpec((1, S, HEADS * HEAD_PAD), lambda b, i: (b, 0, 0)),
                  pl.BlockSpec((1, nblk, HEADS * VDIM, TILE), lambda b, i: (b, 0, 0, 0)),
                  _full((N_META, HEADS * HEAD_PAD)), _full((HEADS * VDIM, N_META)),
                  qtile(HEADS * VDIM), qtile(CONV_W), qtile(D_MODEL),
                  _full((D_MODEL, D_MODEL)), _full((1, HEADS * VDIM)), _full((1, D_MODEL))],
        out_specs=qtile(D_MODEL),
        out_shape=jax.ShapeDtypeStruct((B, S, D_MODEL), F32),
        scratch_shapes=[pltpu.VMEM((HEADS, HEAD_PAD, TILE), BF16),
                        pltpu.VMEM((HEADS, 1, TILE), F32),
                        pltpu.VMEM((HEADS, VDIM + ONES_ROWS, TILE), F32),
                        pltpu.VMEM((2, TILE, TILE), BF16),
                        pltpu.VMEM((1, TILE), F32),
                        pltpu.VMEM((TILE, D_MODEL), BF16)],
        compiler_params=pltpu.CompilerParams(dimension_semantics=("arbitrary", "arbitrary")),
        name="attn_out",
    )(qt_s, k_s, vt_s, k_meta, vt_meta, ga_s, yc_s, x, w_out_b, row(attn_out_g), row(final_norm_g))
    return out


def kernel(x, meta_tokens, norm_g, w_in, q_norm_g, w_q_up, kv_norm_g, w_kv_up, conv_w,
           attn_out_g, conv_out_g, w_out, final_norm_g):
    assert norm_g.shape[0] == 1, "single-layer block"
    return _layer(x, meta_tokens, norm_g[0], w_in[0], q_norm_g[0], w_q_up[0], kv_norm_g[0],
                  w_kv_up[0], conv_w[0], attn_out_g[0], conv_out_g[0], w_out[0], final_norm_g)
```

```python
import math

import jax
import jax.numpy as jnp
import numpy as np
from jax import lax
from jax.experimental import pallas as pl
from jax.experimental.pallas import tpu as pltpu

F32 = jnp.float32
BF16 = jnp.bfloat16

D_MODEL = 1024
N_META = 16
HEADS = 4
NOPE = 128
ROPE = 64
VDIM = 128
Q_LORA = 256
KV_LORA = 128
CONV_W = 512
CONV_GROUP = 64
ROPE_THETA = 10000.0
ATTN_SCALE = (NOPE + ROPE) ** -0.5
NEG_INF = -1e30
EPS = 1e-6

LANES = 128
HEAD_PAD = 2 * LANES
P_COLS = 3072

TILE = 512
HALF = TILE // 2
QK_AHEAD = 2
ONES_ROWS = 16
REF_ROW = NOPE + ROPE
REF_ROWS = 16
OUT_ROWS = 256
RISE_LIMIT = 100.0
VMEM_LIMIT_BYTES = 56 * 2 ** 20
Q_SCALE = ATTN_SCALE * math.log2(math.e)


def _rms(x, g):
    ms = jnp.mean(x * x, axis=-1, keepdims=True)
    return x * lax.rsqrt(ms + EPS) * g


def _silu(x):
    return x * (1.0 / (1.0 + jnp.exp(-x)))


def _dot(a, b):
    return jnp.dot(a, b, preferred_element_type=F32)


def _dot_nt(a, b):
    return lax.dot_general(a, b, (((1,), (1,)), ((), ())), preferred_element_type=F32)


def _in_proj(x, norm_g, w_in):
    u = _rms(x, norm_g).astype(BF16)
    return _dot(u, w_in)


def _keys_values(p, cos, sin, kv_g, w_k, w_vt):
    c_kv = _rms(p[:, Q_LORA:Q_LORA + KV_LORA], kv_g).astype(BF16)
    k_nope = _dot(c_kv, w_k)
    v_t = _dot_nt(w_vt, c_kv)
    kr = p[:, 384:512]
    k_pe = kr * cos + pltpu.roll(kr, ROPE, 1) * sin
    lane = lax.broadcasted_iota(jnp.int32, k_pe.shape, 1)
    k_pe = jnp.where(lane == REF_ROW - NOPE, 1.0, k_pe).astype(BF16)
    ks = []
    for h in range(HEADS):
        ks.append(k_nope[:, h * NOPE:(h + 1) * NOPE].astype(BF16))
        ks.append(k_pe)
    return ks, v_t.astype(BF16)


def _meta_kernel(x_ref, cos_ref, sin_ref, norm_g_ref, w_in_ref, kv_g_ref, w_k_ref, w_vt_ref,
                 k_out, vt_out, g_out):
    p = _in_proj(x_ref[...], norm_g_ref[...], w_in_ref[...])
    ks, v_t = _keys_values(p, cos_ref[...], sin_ref[...], kv_g_ref[...], w_k_ref[...], w_vt_ref[...])
    for i, kk in enumerate(ks):
        k_out[:, i * LANES:(i + 1) * LANES] = kk
    vt_out[...] = v_t
    g_out[...] = p[:, 2048:2560] * p[:, 1536:2048]


def _proj_kernel(x_ref, cos_ref, sin_ref, cos_t_ref, sin_t_ref, ginit_ref, norm_g_ref, w_in_ref,
                 q_g_ref, w_qt_ref, kv_g_ref, w_k_ref, w_vt_ref, conv_w_ref, conv_g_ref, gmat_ref,
                 qt_out, k_out, vt_out, ga_out, yc_out, gbuf):
    tm = x_ref.shape[1]
    p = _in_proj(x_ref[0], norm_g_ref[...], w_in_ref[...])

    c_q = _rms(p[:, 0:Q_LORA], q_g_ref[...]).astype(BF16)
    q_t = _dot_nt(w_qt_ref[...], c_q)
    cos_t = cos_t_ref[...]
    sin_t = sin_t_ref[...]
    for h in range(HEADS):
        r = h * HEAD_PAD
        qt_out[0, r:r + NOPE, :] = (q_t[r:r + NOPE, :] * Q_SCALE).astype(BF16)
        q_pe = q_t[r + NOPE:r + NOPE + ROPE, :] * cos_t + q_t[r + NOPE + ROPE:r + HEAD_PAD, :] * sin_t
        qt_out[0, r + NOPE:r + NOPE + ROPE, :] = (q_pe * Q_SCALE).astype(BF16)
        qt_out[0, r + NOPE + ROPE:r + HEAD_PAD, :] = jnp.zeros((ROPE, tm), BF16)

    ks, v_t = _keys_values(p, cos_ref[...], sin_ref[...], kv_g_ref[...], w_k_ref[...], w_vt_ref[...])
    for i, kk in enumerate(ks):
        k_out[0, :, i * LANES:(i + 1) * LANES] = kk
    vt_out[0, 0] = v_t

    ga_out[0] = _silu(p[:, 512:1024]).astype(BF16)

    @pl.when(pl.program_id(1) == 0)
    def _():
        gbuf[0:8, :] = ginit_ref[8:16, :]

    g = p[:, 1536:2048] * p[:, 2048:2560]
    gbuf[8:8 + tm, :] = g
    g1 = gbuf[7:7 + tm, :]
    g2 = gbuf[6:6 + tm, :]
    cw = conv_w_ref[...]
    conv = cw[0:1, :] * g2 + cw[1:2, :] * g1 + cw[2:3, :] * g
    gbuf[0:8, :] = gbuf[tm:tm + 8, :]
    yc = p[:, 1024:1536] * conv
    ssum = _dot((yc * yc).astype(BF16), gmat_ref[...])
    ycn = yc * lax.rsqrt(ssum * (1.0 / CONV_GROUP) + EPS) * conv_g_ref[...]
    yc_out[0] = (ycn * _silu(p[:, 2560:3072])).astype(BF16)


def _ones_rows(v_t):
    return jnp.concatenate([v_t, jnp.ones((ONES_ROWS, v_t.shape[1]), BF16)], axis=0)


def _bf16_exact(x):
    return x.astype(BF16).astype(F32)


def _attention_tile(c, qt_ref, k_ref, vt_ref, km_ref, vtm_ref, q_aug, m_scr, acc_scr, p_scr, rise_scr):
    tri_full = (lax.broadcasted_iota(jnp.int32, (HALF, TILE), 0)
                <= lax.broadcasted_iota(jnp.int32, (HALF, TILE), 1))
    tri_half = tri_full[:, :HALF]

    def k_of(h, r0, r1):
        return k_ref[0, r0:r1, h * HEAD_PAD:(h + 1) * HEAD_PAD]

    def v_of(h, j, lo, hi):
        return vt_ref[0, j, h * VDIM:(h + 1) * VDIM, lo:hi]

    tasks = []
    for h in range(HEADS):
        tasks.append((h, lambda h=h: km_ref[:, h * HEAD_PAD:(h + 1) * HEAD_PAD],
                      lambda h=h: vtm_ref[h * VDIM:(h + 1) * VDIM, :], 0, TILE, None, True))
    for j in range(c):
        for h in range(HEADS):
            tasks.append((h, lambda h=h, j=j: k_of(h, j * TILE, (j + 1) * TILE),
                          lambda h=h, j=j: v_of(h, j, 0, TILE), 0, TILE, None, False))
    for h in range(HEADS):
        tasks.append((h, lambda h=h: k_of(h, c * TILE, c * TILE + HALF),
                      lambda h=h: v_of(h, c, 0, HALF), 0, TILE, tri_full, False))
    for h in range(HEADS):
        tasks.append((h, lambda h=h: k_of(h, c * TILE + HALF, (c + 1) * TILE),
                      lambda h=h: v_of(h, c, HALF, TILE), HALF, TILE, tri_half, False))
    assert QK_AHEAD < HEADS and all(t[0] == n % HEADS for n, t in enumerate(tasks))

    for h in range(HEADS):
        q_aug[h] = qt_ref[0, h * HEAD_PAD:(h + 1) * HEAD_PAD, :]
    rise_scr[...] = jnp.zeros(rise_scr.shape, F32)

    def scores(task):
        h, k_fn, _, lo, hi, mask, _ = task
        s = _dot(k_fn(), q_aug[h, :, lo:hi])
        return s if mask is None else jnp.where(mask, s, NEG_INF)

    def softmax(n, task, s):
        h, _, _, lo, hi, _, first = task
        nkeys, ncols = s.shape
        blk_max = jnp.max(s, axis=0, keepdims=True)
        if first:
            m_new, beta = _bf16_exact(blk_max), None
            p = jnp.exp2(s - m_new)
        else:
            rise = jnp.maximum(blk_max, 0.0)
            p = jnp.exp2(s)
            m_old = m_scr[h, :, lo:hi]
            m_new = _bf16_exact(m_old + rise)
            beta = jnp.exp2(m_old - m_new)
            rise_scr[:, lo:hi] = jnp.maximum(rise_scr[:, lo:hi], rise)
        p_scr[n % 2, 0:nkeys, 0:ncols] = p.astype(BF16)
        m_scr[h, :, lo:hi] = m_new
        q_aug[h, REF_ROW:REF_ROW + REF_ROWS, lo:hi] = jnp.broadcast_to(-m_new, (REF_ROWS, ncols)).astype(BF16)
        return nkeys, ncols, beta

    def values(n, task, nkeys, ncols, beta):
        h, _, v_fn, lo, hi, _, _ = task
        pv = _dot(_ones_rows(v_fn()), p_scr[n % 2, 0:nkeys, 0:ncols])
        if beta is None:
            acc_scr[h, :, lo:hi] = pv
        else:
            acc_scr[h, :, lo:hi] = (acc_scr[h, :, lo:hi] + pv) * beta

    pending = [scores(t) for t in tasks[:QK_AHEAD]]
    prev = None
    for n, task in enumerate(tasks):
        if prev is not None:
            values(*prev)
        if n + QK_AHEAD < len(tasks):
            pending.append(scores(tasks[n + QK_AHEAD]))
        prev = (n, task) + softmax(n, task, pending.pop(0))
    values(*prev)


def _attention_tile_two_pass(i, qt_ref, k_ref, vt_ref, km_ref, vtm_ref, m_scr, acc_scr):
    tri_full = (lax.broadcasted_iota(jnp.int32, (HALF, TILE), 0)
                <= lax.broadcasted_iota(jnp.int32, (HALF, TILE), 1))
    tri_half = tri_full[:, :HALF]

    def q_of(h, lo, hi):
        return qt_ref[0, h * HEAD_PAD:(h + 1) * HEAD_PAD, lo:hi]

    def update(h, s, v_t, lo, hi):
        m_old = m_scr[h, :, lo:hi]
        m_new = jnp.maximum(m_old, jnp.max(s, axis=0, keepdims=True))
        pv = _dot(_ones_rows(v_t), jnp.exp2(s - m_new).astype(BF16))
        acc_scr[h, :, lo:hi] = jnp.exp2(m_old - m_new) * acc_scr[h, :, lo:hi] + pv
        m_scr[h, :, lo:hi] = m_new

    for h in range(HEADS):
        s = _dot(km_ref[:, h * HEAD_PAD:(h + 1) * HEAD_PAD], q_of(h, 0, TILE))
        m = jnp.max(s, axis=0, keepdims=True)
        m_scr[h] = m
        acc_scr[h] = _dot(_ones_rows(vtm_ref[h * VDIM:(h + 1) * VDIM, :]), jnp.exp2(s - m).astype(BF16))

    def full_block(j, carry):
        start = pl.multiple_of(j * TILE, TILE)
        for h in range(HEADS):
            k = k_ref[0, pl.ds(start, TILE), h * HEAD_PAD:(h + 1) * HEAD_PAD]
            update(h, _dot(k, q_of(h, 0, TILE)), vt_ref[0, j, h * VDIM:(h + 1) * VDIM, :], 0, TILE)
        return carry

    lax.fori_loop(0, i, full_block, 0)

    d0 = pl.multiple_of(i * TILE, TILE)
    d1 = pl.multiple_of(i * TILE + HALF, HALF)
    for h in range(HEADS):
        k = k_ref[0, pl.ds(d0, HALF), h * HEAD_PAD:(h + 1) * HEAD_PAD]
        s = jnp.where(tri_full, _dot(k, q_of(h, 0, TILE)), NEG_INF)
        update(h, s, vt_ref[0, i, h * VDIM:(h + 1) * VDIM, 0:HALF], 0, TILE)
    for h in range(HEADS):
        k = k_ref[0, pl.ds(d1, HALF), h * HEAD_PAD:(h + 1) * HEAD_PAD]
        s = jnp.where(tri_half, _dot(k, q_of(h, HALF, TILE)), NEG_INF)
        update(h, s, vt_ref[0, i, h * VDIM:(h + 1) * VDIM, HALF:TILE], HALF, TILE)


def _attn_kernel(qt_ref, k_ref, vt_ref, km_ref, vtm_ref, ga_ref, yc_ref, x_ref, w_out_ref,
                 attn_g_ref, final_g_ref, o_ref, q_aug, m_scr, acc_scr, p_scr, rise_scr, y_scr):
    nblk = k_ref.shape[1] // TILE
    for c in range(nblk):
        @pl.when(pl.program_id(1) == c)
        def _(c=c):
            _attention_tile(c, qt_ref, k_ref, vt_ref, km_ref, vtm_ref, q_aug, m_scr, acc_scr, p_scr, rise_scr)

    @pl.when(jnp.max(rise_scr[...]) > RISE_LIMIT)
    def _():
        _attention_tile_two_pass(pl.program_id(1), qt_ref, k_ref, vt_ref, km_ref, vtm_ref, m_scr, acc_scr)

    for h in range(HEADS):
        o_t = acc_scr[h, 0:VDIM, :] * (1.0 / acc_scr[h, VDIM:VDIM + 1, :])
        o = _rms(o_t.T, attn_g_ref[:, h * VDIM:(h + 1) * VDIM])
        y_scr[:, h * VDIM:(h + 1) * VDIM] = (o * ga_ref[0, :, h * VDIM:(h + 1) * VDIM].astype(F32)).astype(BF16)

    y_scr[:, HEADS * VDIM:] = yc_ref[0]
    for r0 in range(0, TILE, OUT_ROWS):
        mix = _dot(y_scr[r0:r0 + OUT_ROWS, :], w_out_ref[...])
        o_ref[0, r0:r0 + OUT_ROWS, :] = _rms(x_ref[0, r0:r0 + OUT_ROWS, :] + mix, final_g_ref[...])


def _rope_angles(pos):
    half = ROPE // 2
    inv_freq = (1.0 / (ROPE_THETA ** (np.arange(half, dtype=np.float32) / half))).astype(np.float32)
    ang = pos.astype(np.float32)[:, None] * inv_freq[None, :]
    return np.cos(ang).astype(np.float32), np.sin(ang).astype(np.float32)


def _swap_halves(w):
    half = w.shape[-1] // 2
    return jnp.concatenate([w[..., half:], w[..., :half]], axis=-1)


def _full(shape):
    return pl.BlockSpec(shape, lambda *_: (0,) * len(shape))


def _layer(x, meta_tokens, norm_g, w_in, q_norm_g, w_q_up, kv_norm_g, w_kv_up, conv_w,
           attn_out_g, conv_out_g, w_out, final_norm_g):
    B, S, D = x.shape
    assert D == D_MODEL and S % TILE == 0
    assert meta_tokens.shape == (N_META, D_MODEL)
    nblk = S // TILE

    k_rope_w = w_in[:, 384:448]
    w_in_p = jnp.concatenate([w_in[:, :448], _swap_halves(k_rope_w), w_in[:, 448:]], axis=1).astype(BF16)
    wq = w_q_up.reshape(Q_LORA, HEADS, NOPE + ROPE)
    wq_t = jnp.concatenate([wq, _swap_halves(wq[..., NOPE:])], axis=-1).reshape(Q_LORA, HEADS * HEAD_PAD).T.astype(BF16)
    wkv = w_kv_up.reshape(KV_LORA, HEADS, NOPE + VDIM)
    w_k = wkv[..., :NOPE].reshape(KV_LORA, HEADS * NOPE).astype(BF16)
    w_vt = wkv[..., NOPE:].reshape(KV_LORA, HEADS * VDIM).T.astype(BF16)
    w_out_b = w_out.astype(BF16)
    gid = np.arange(CONV_W) // CONV_GROUP
    gmat = jnp.asarray((gid[:, None] == gid[None, :]).astype(np.float32), dtype=BF16)
    row = lambda v: v.reshape(1, -1).astype(F32)

    zpad = lambda n: np.zeros((n, LANES - ROPE), np.float32)
    c_m, s_m = _rope_angles(np.arange(N_META))
    cos_m = np.concatenate([c_m, c_m, zpad(N_META)], axis=-1)
    sin_m = np.concatenate([-s_m, s_m, zpad(N_META)], axis=-1)
    c_r, s_r = _rope_angles(N_META + np.arange(S))
    cos_r = np.concatenate([c_r, c_r, zpad(S)], axis=-1)
    sin_r = np.concatenate([-s_r, s_r, zpad(S)], axis=-1)
    cos_t = np.ascontiguousarray(np.concatenate([c_r, c_r], axis=-1).T)
    sin_t = np.ascontiguousarray(np.concatenate([-s_r, s_r], axis=-1).T)

    k_meta, vt_meta, g_meta = pl.pallas_call(
        _meta_kernel,
        out_shape=(jax.ShapeDtypeStruct((N_META, HEADS * HEAD_PAD), BF16),
                   jax.ShapeDtypeStruct((HEADS * VDIM, N_META), BF16),
                   jax.ShapeDtypeStruct((N_META, CONV_W), F32)),
        name="meta_proj",
    )(meta_tokens.astype(F32), cos_m, sin_m, row(norm_g), w_in_p, row(kv_norm_g), w_k, w_vt)

    tok = lambda w: pl.BlockSpec((1, TILE, w), lambda b, t: (b, t, 0))
    qt_s, k_s, vt_s, ga_s, yc_s = pl.pallas_call(
        _proj_kernel,
        grid=(B, nblk),
        in_specs=[tok(D_MODEL),
                  pl.BlockSpec((TILE, LANES), lambda b, t: (t, 0)),
                  pl.BlockSpec((TILE, LANES), lambda b, t: (t, 0)),
                  pl.BlockSpec((ROPE, TILE), lambda b, t: (0, t)),
                  pl.BlockSpec((ROPE, TILE), lambda b, t: (0, t)),
                  _full((N_META, CONV_W)), _full((1, D_MODEL)), _full((D_MODEL, P_COLS)),
                  _full((1, Q_LORA)), _full((HEADS * HEAD_PAD, Q_LORA)),
                  _full((1, KV_LORA)), _full((KV_LORA, HEADS * NOPE)), _full((HEADS * VDIM, KV_LORA)),
                  _full((3, CONV_W)), _full((1, CONV_W)), _full((CONV_W, CONV_W))],
        out_specs=[pl.BlockSpec((1, HEADS * HEAD_PAD, TILE), lambda b, t: (b, 0, t)),
                   tok(HEADS * HEAD_PAD),
                   pl.BlockSpec((1, 1, HEADS * VDIM, TILE), lambda b, t: (b, t, 0, 0)),
                   tok(HEADS * VDIM), tok(CONV_W)],
        out_shape=[jax.ShapeDtypeStruct((B, HEADS * HEAD_PAD, S), BF16),
                   jax.ShapeDtypeStruct((B, S, HEADS * HEAD_PAD), BF16),
                   jax.ShapeDtypeStruct((B, nblk, HEADS * VDIM, TILE), BF16),
                   jax.ShapeDtypeStruct((B, S, HEADS * VDIM), BF16),
                   jax.ShapeDtypeStruct((B, S, CONV_W), BF16)],
        scratch_shapes=[pltpu.VMEM((TILE + 8, CONV_W), F32)],
        compiler_params=pltpu.CompilerParams(dimension_semantics=("arbitrary", "arbitrary"),
                                             vmem_limit_bytes=VMEM_LIMIT_BYTES),
        name="proj",
    )(x, cos_r, sin_r, cos_t, sin_t, g_meta, row(norm_g), w_in_p, row(q_norm_g), wq_t,
      row(kv_norm_g), w_k, w_vt, conv_w.astype(F32), row(conv_out_g), gmat)

    qtile = lambda w: pl.BlockSpec((1, TILE, w), lambda b, i: (b, i, 0))
    out = pl.pallas_call(
        _attn_kernel,
        grid=(B, nblk),
        in_specs=[pl.BlockSpec((1, HEADS * HEAD_PAD, TILE), lambda b, i: (b, 0, i)),
                  pl.BlockSpec((1, S, HEADS * HEAD_PAD), lambda b, i: (b, 0, 0)),
                  pl.BlockSpec((1, nblk, HEADS * VDIM, TILE), lambda b, i: (b, 0, 0, 0)),
                  _full((N_META, HEADS * HEAD_PAD)), _full((HEADS * VDIM, N_META)),
                  qtile(HEADS * VDIM), qtile(CONV_W), qtile(D_MODEL),
                  _full((D_MODEL, D_MODEL)), _full((1, HEADS * VDIM)), _full((1, D_MODEL))],
        out_specs=qtile(D_MODEL),
        out_shape=jax.ShapeDtypeStruct((B, S, D_MODEL), F32),
        scratch_shapes=[pltpu.VMEM((HEADS, HEAD_PAD, TILE), BF16),
                        pltpu.VMEM((HEADS, 1, TILE), F32),
                        pltpu.VMEM((HEADS, VDIM + ONES_ROWS, TILE), F32),
                        pltpu.VMEM((2, TILE, TILE), BF16),
                        pltpu.VMEM((1, TILE), F32),
                        pltpu.VMEM((TILE, D_MODEL), BF16)],
        compiler_params=pltpu.CompilerParams(dimension_semantics=("arbitrary", "arbitrary"),
                                             vmem_limit_bytes=VMEM_LIMIT_BYTES),
        name="attn_out",
    )(qt_s, k_s, vt_s, k_meta, vt_meta, ga_s, yc_s, x, w_out_b, row(attn_out_g), row(final_norm_g))
    return out


def kernel(x, meta_tokens, norm_g, w_in, q_norm_g, w_q_up, kv_norm_g, w_kv_up, conv_w,
           attn_out_g, conv_out_g, w_out, final_norm_g):
    assert norm_g.shape[0] == 1, "single-layer block"
    return _layer(x, meta_tokens, norm_g[0], w_in[0], q_norm_g[0], w_q_up[0], kv_norm_g[0],
                  w_kv_up[0], conv_w[0], attn_out_g[0], conv_out_g[0], w_out[0], final_norm_g)
```

```python
import math

import jax
import jax.numpy as jnp
import numpy as np
from jax import lax
from jax.experimental import pallas as pl
from jax.experimental.pallas import tpu as pltpu

F32 = jnp.float32
BF16 = jnp.bfloat16

D_MODEL = 1024
N_META = 16
HEADS = 4
NOPE = 128
ROPE = 64
VDIM = 128
Q_LORA = 256
KV_LORA = 128
CONV_W = 512
CONV_GROUP = 64
ROPE_THETA = 10000.0
ATTN_SCALE = (NOPE + ROPE) ** -0.5
NEG_INF = -1e30
EPS = 1e-6

LANES = 128
HEAD_PAD = 2 * LANES
P_COLS = 3072
C_Q, C_KV, C_KROPE, C_Z_ATTN, C_CONV_B, C_CONV_C, C_CONV_H, C_Z_CONV = (
    0, 256, 384, 512, 1024, 1536, 2048, 2560)

TILE = 512
HALF = TILE // 2
QK_AHEAD = 2
ONES_ROWS = 16
REF_ROW = NOPE + ROPE
REF_ROWS = 16
OUT_ROWS = 256
RISE_LIMIT = 100.0
VMEM_LIMIT_BYTES = 56 * 2 ** 20
Q_SCALE = ATTN_SCALE * math.log2(math.e)


def _rms(x, g):
    ms = jnp.mean(x * x, axis=-1, keepdims=True)
    return x * lax.rsqrt(ms + EPS) * g


def _silu(x):
    hx = (0.5 * x).astype(BF16)
    return hx + hx * jnp.tanh(hx)


def _dot(a, b):
    return jnp.dot(a, b, preferred_element_type=F32)


def _dot_nt(a, b):
    return lax.dot_general(a, b, (((1,), (1,)), ((), ())), preferred_element_type=F32)


def _in_proj(x, norm_g, w_in):
    u = _rms(x, norm_g).astype(BF16)
    return _dot(u, w_in)


def _keys_values(p, cos, sin, kv_g, w_k, w_vt):
    c_kv = _rms(p[:, C_KV:C_KV + KV_LORA], kv_g).astype(BF16)
    k_nope = _dot(c_kv, w_k)
    v_t = _dot_nt(w_vt, c_kv)
    kr = p[:, C_KROPE:C_KROPE + LANES]
    k_pe = kr * cos + pltpu.roll(kr, ROPE, 1) * sin
    lane = lax.broadcasted_iota(jnp.int32, k_pe.shape, 1)
    k_pe = jnp.where(lane == REF_ROW - NOPE, 1.0, k_pe).astype(BF16)
    ks = []
    for h in range(HEADS):
        ks.append(k_nope[:, h * NOPE:(h + 1) * NOPE].astype(BF16))
        ks.append(k_pe)
    return ks, v_t.astype(BF16)


def _meta_kernel(x_ref, cos_ref, sin_ref, norm_g_ref, w_in_ref, kv_g_ref, w_k_ref, w_vt_ref,
                 k_out, vt_out, g_out):
    p = _in_proj(x_ref[...], norm_g_ref[...], w_in_ref[...])
    ks, v_t = _keys_values(p, cos_ref[...], sin_ref[...], kv_g_ref[...], w_k_ref[...], w_vt_ref[...])
    for i, kk in enumerate(ks):
        k_out[:, i * LANES:(i + 1) * LANES] = kk
    vt_out[...] = v_t
    g_out[...] = p[:, C_CONV_C:C_CONV_C + CONV_W] * p[:, C_CONV_H:C_CONV_H + CONV_W]


def _proj_kernel(x_ref, cos_ref, sin_ref, cos_t_ref, sin_t_ref, ginit_ref, norm_g_ref, w_in_ref,
                 q_g_ref, w_qt_ref, kv_g_ref, w_k_ref, w_vt_ref, conv_w_ref, conv_g_ref, gmat_ref,
                 qt_out, k_out, vt_out, ga_out, yc_out, gbuf):
    tm = x_ref.shape[1]

    @pl.when(pl.program_id(1) == 0)
    def _():
        gbuf[0:8, :] = ginit_ref[8:16, :]

    p = _in_proj(x_ref[0], norm_g_ref[...], w_in_ref[...])

    c_q = _rms(p[:, C_Q:C_Q + Q_LORA], q_g_ref[...] * Q_SCALE).astype(BF16)
    q_t = _dot_nt(w_qt_ref[...], c_q)
    cos_t = cos_t_ref[...]
    sin_t = sin_t_ref[...]
    for h in range(HEADS):
        r = h * HEAD_PAD
        qt_out[0, r:r + NOPE, :] = q_t[r:r + NOPE, :].astype(BF16)
        q_pe = q_t[r + NOPE:r + NOPE + ROPE, :] * cos_t + q_t[r + NOPE + ROPE:r + HEAD_PAD, :] * sin_t
        qt_out[0, r + NOPE:r + NOPE + ROPE, :] = q_pe.astype(BF16)
        qt_out[0, r + NOPE + ROPE:r + HEAD_PAD, :] = jnp.zeros((ROPE, tm), BF16)

    ks, v_t = _keys_values(p, cos_ref[...], sin_ref[...], kv_g_ref[...], w_k_ref[...], w_vt_ref[...])
    for i, kk in enumerate(ks):
        k_out[0, :, i * LANES:(i + 1) * LANES] = kk
    vt_out[0, 0] = v_t

    ga_out[0] = _silu(p[:, C_Z_ATTN:C_Z_ATTN + HEADS * VDIM]).astype(BF16)

    g = p[:, C_CONV_C:C_CONV_C + CONV_W] * p[:, C_CONV_H:C_CONV_H + CONV_W]
    gbuf[8:8 + tm, :] = g
    g1 = gbuf[7:7 + tm, :]
    g2 = gbuf[6:6 + tm, :]
    cw = conv_w_ref[...]
    conv = cw[0:1, :] * g2 + cw[1:2, :] * g1 + cw[2:3, :] * g
    gbuf[0:8, :] = gbuf[tm:tm + 8, :]
    yc = p[:, C_CONV_B:C_CONV_B + CONV_W] * conv
    ssum = _dot((yc * yc).astype(BF16), gmat_ref[...])
    ycn = yc * lax.rsqrt(ssum * (1.0 / CONV_GROUP) + EPS) * conv_g_ref[...]
    yc_out[0] = (ycn * _silu(p[:, C_Z_CONV:C_Z_CONV + CONV_W])).astype(BF16)


def _ones_rows(v_t):
    return jnp.concatenate([v_t, jnp.ones((ONES_ROWS, v_t.shape[1]), BF16)], axis=0)


def _bf16_exact(x):
    return x.astype(BF16).astype(F32)


def _attention_tile(c, qt_ref, k_ref, vt_ref, km_ref, vtm_ref, q_aug, m_scr, acc_scr, p_scr, rise_scr):
    tri_full = (lax.broadcasted_iota(jnp.int32, (HALF, TILE), 0)
                <= lax.broadcasted_iota(jnp.int32, (HALF, TILE), 1))
    tri_half = tri_full[:, :HALF]

    def k_of(h, r0, r1):
        return k_ref[0, r0:r1, h * HEAD_PAD:(h + 1) * HEAD_PAD]

    def v_of(h, j, lo, hi):
        return vt_ref[0, j, h * VDIM:(h + 1) * VDIM, lo:hi]

    tasks = []
    for h in range(HEADS):
        tasks.append((h, lambda h=h: km_ref[:, h * HEAD_PAD:(h + 1) * HEAD_PAD],
                      lambda h=h: vtm_ref[h * VDIM:(h + 1) * VDIM, :], 0, TILE, None, True))
    for j in range(c):
        for h in range(HEADS):
            tasks.append((h, lambda h=h, j=j: k_of(h, j * TILE, (j + 1) * TILE),
                          lambda h=h, j=j: v_of(h, j, 0, TILE), 0, TILE, None, False))
    for h in range(HEADS):
        tasks.append((h, lambda h=h: k_of(h, c * TILE, c * TILE + HALF),
                      lambda h=h: v_of(h, c, 0, HALF), 0, TILE, tri_full, False))
    for h in range(HEADS):
        tasks.append((h, lambda h=h: k_of(h, c * TILE + HALF, (c + 1) * TILE),
                      lambda h=h: v_of(h, c, HALF, TILE), HALF, TILE, tri_half, False))
    assert QK_AHEAD < HEADS and all(t[0] == n % HEADS for n, t in enumerate(tasks))

    for h in range(HEADS):
        q_aug[h] = qt_ref[0, h * HEAD_PAD:(h + 1) * HEAD_PAD, :]
    rise_scr[...] = jnp.zeros(rise_scr.shape, F32)

    def scores(task):
        h, k_fn, _, lo, hi, mask, _ = task
        s = _dot(k_fn(), q_aug[h, :, lo:hi])
        return s if mask is None else jnp.where(mask, s, NEG_INF)

    def softmax(n, task, s):
        h, _, _, lo, hi, _, first = task
        nkeys, ncols = s.shape
        blk_max = jnp.max(s, axis=0, keepdims=True)
        if first:
            m_new, beta = _bf16_exact(blk_max), None
            p = jnp.exp2(s - m_new)
        else:
            rise = jnp.maximum(blk_max, 0.0)
            p = jnp.exp2(s)
            m_old = m_scr[h, :, lo:hi]
            m_new = _bf16_exact(m_old + rise)
            beta = jnp.exp2(m_old - m_new)
            rise_scr[:, lo:hi] = jnp.maximum(rise_scr[:, lo:hi], rise)
        p_scr[n % 2, 0:nkeys, 0:ncols] = p.astype(BF16)
        m_scr[h, :, lo:hi] = m_new
        q_aug[h, REF_ROW:REF_ROW + REF_ROWS, lo:hi] = jnp.broadcast_to(-m_new, (REF_ROWS, ncols)).astype(BF16)
        return nkeys, ncols, beta

    def values(n, task, nkeys, ncols, beta):
        h, _, v_fn, lo, hi, _, _ = task
        pv = _dot(_ones_rows(v_fn()), p_scr[n % 2, 0:nkeys, 0:ncols])
        if beta is None:
            acc_scr[h, :, lo:hi] = pv
        else:
            acc_scr[h, :, lo:hi] = (acc_scr[h, :, lo:hi] + pv) * beta

    pending = [scores(t) for t in tasks[:QK_AHEAD]]
    prev = None
    for n, task in enumerate(tasks):
        if prev is not None:
            values(*prev)
        if n + QK_AHEAD < len(tasks):
            pending.append(scores(tasks[n + QK_AHEAD]))
        prev = (n, task) + softmax(n, task, pending.pop(0))
    values(*prev)


def _attention_tile_two_pass(i, qt_ref, k_ref, vt_ref, km_ref, vtm_ref, m_scr, acc_scr):
    tri_full = (lax.broadcasted_iota(jnp.int32, (HALF, TILE), 0)
                <= lax.broadcasted_iota(jnp.int32, (HALF, TILE), 1))
    tri_half = tri_full[:, :HALF]

    def q_of(h, lo, hi):
        return qt_ref[0, h * HEAD_PAD:(h + 1) * HEAD_PAD, lo:hi]

    def update(h, s, v_t, lo, hi):
        m_old = m_scr[h, :, lo:hi]
        m_new = jnp.maximum(m_old, jnp.max(s, axis=0, keepdims=True))
        pv = _dot(_ones_rows(v_t), jnp.exp2(s - m_new).astype(BF16))
        acc_scr[h, :, lo:hi] = jnp.exp2(m_old - m_new) * acc_scr[h, :, lo:hi] + pv
        m_scr[h, :, lo:hi] = m_new

    for h in range(HEADS):
        s = _dot(km_ref[:, h * HEAD_PAD:(h + 1) * HEAD_PAD], q_of(h, 0, TILE))
        m = jnp.max(s, axis=0, keepdims=True)
        m_scr[h] = m
        acc_scr[h] = _dot(_ones_rows(vtm_ref[h * VDIM:(h + 1) * VDIM, :]), jnp.exp2(s - m).astype(BF16))

    def full_block(j, carry):
        start = pl.multiple_of(j * TILE, TILE)
        for h in range(HEADS):
            k = k_ref[0, pl.ds(start, TILE), h * HEAD_PAD:(h + 1) * HEAD_PAD]
            update(h, _dot(k, q_of(h, 0, TILE)), vt_ref[0, j, h * VDIM:(h + 1) * VDIM, :], 0, TILE)
        return carry

    lax.fori_loop(0, i, full_block, 0)

    d0 = pl.multiple_of(i * TILE, TILE)
    d1 = pl.multiple_of(i * TILE + HALF, HALF)
    for h in range(HEADS):
        k = k_ref[0, pl.ds(d0, HALF), h * HEAD_PAD:(h + 1) * HEAD_PAD]
        s = jnp.where(tri_full, _dot(k, q_of(h, 0, TILE)), NEG_INF)
        update(h, s, vt_ref[0, i, h * VDIM:(h + 1) * VDIM, 0:HALF], 0, TILE)
    for h in range(HEADS):
        k = k_ref[0, pl.ds(d1, HALF), h * HEAD_PAD:(h + 1) * HEAD_PAD]
        s = jnp.where(tri_half, _dot(k, q_of(h, HALF, TILE)), NEG_INF)
        update(h, s, vt_ref[0, i, h * VDIM:(h + 1) * VDIM, HALF:TILE], HALF, TILE)


def _attn_kernel(qt_ref, k_ref, vt_ref, km_ref, vtm_ref, ga_ref, yc_ref, x_ref, w_out_ref,
                 attn_g_ref, final_g_ref, o_ref, q_aug, m_scr, acc_scr, p_scr, rise_scr, y_scr):
    nblk = k_ref.shape[1] // TILE
    for c in range(nblk):
        @pl.when(pl.program_id(1) == c)
        def _(c=c):
            _attention_tile(c, qt_ref, k_ref, vt_ref, km_ref, vtm_ref, q_aug, m_scr, acc_scr, p_scr, rise_scr)

    @pl.when(jnp.max(rise_scr[...]) > RISE_LIMIT)
    def _():
        _attention_tile_two_pass(pl.program_id(1), qt_ref, k_ref, vt_ref, km_ref, vtm_ref, m_scr, acc_scr)

    for h in range(HEADS):
        o_t = acc_scr[h, 0:VDIM, :] * (1.0 / acc_scr[h, VDIM:VDIM + 1, :])
        o = _rms(o_t.T, attn_g_ref[:, h * VDIM:(h + 1) * VDIM])
        y_scr[:, h * VDIM:(h + 1) * VDIM] = (o * ga_ref[0, :, h * VDIM:(h + 1) * VDIM].astype(F32)).astype(BF16)

    y_scr[:, HEADS * VDIM:] = yc_ref[0]
    for r0 in range(0, TILE, OUT_ROWS):
        mix = _dot(y_scr[r0:r0 + OUT_ROWS, :], w_out_ref[...])
        o_ref[0, r0:r0 + OUT_ROWS, :] = _rms(x_ref[0, r0:r0 + OUT_ROWS, :] + mix, final_g_ref[...])


def _rope_angles(pos):
    half = ROPE // 2
    inv_freq = (1.0 / (ROPE_THETA ** (np.arange(half, dtype=np.float32) / half))).astype(np.float32)
    ang = pos.astype(np.float32)[:, None] * inv_freq[None, :]
    return np.cos(ang).astype(np.float32), np.sin(ang).astype(np.float32)


def _swap_halves(w):
    half = w.shape[-1] // 2
    return jnp.concatenate([w[..., half:], w[..., :half]], axis=-1)


def _full(shape):
    return pl.BlockSpec(shape, lambda *_: (0,) * len(shape))


def _layer(x, meta_tokens, norm_g, w_in, q_norm_g, w_q_up, kv_norm_g, w_kv_up, conv_w,
           attn_out_g, conv_out_g, w_out, final_norm_g):
    B, S, D = x.shape
    assert D == D_MODEL and S % TILE == 0
    assert meta_tokens.shape == (N_META, D_MODEL)
    nblk = S // TILE

    sec = lambda lo, n: w_in[:, lo:lo + n].astype(BF16)
    w_in_p = jnp.concatenate([sec(0, 448), _swap_halves(sec(384, 64)), sec(448, 2560)], axis=1)
    wq = w_q_up.reshape(Q_LORA, HEADS, NOPE + ROPE)
    wq_t = jnp.concatenate([wq, _swap_halves(wq[..., NOPE:])], axis=-1).reshape(Q_LORA, HEADS * HEAD_PAD).T.astype(BF16)
    wkv = w_kv_up.reshape(KV_LORA, HEADS, NOPE + VDIM)
    w_k = wkv[..., :NOPE].reshape(KV_LORA, HEADS * NOPE).astype(BF16)
    w_vt = wkv[..., NOPE:].reshape(KV_LORA, HEADS * VDIM).T.astype(BF16)
    w_out_b = w_out.astype(BF16)
    gid = np.arange(CONV_W) // CONV_GROUP
    gmat = jnp.asarray((gid[:, None] == gid[None, :]).astype(np.float32), dtype=BF16)
    row = lambda v: v.reshape(1, -1).astype(F32)

    zpad = lambda n: np.zeros((n, LANES - ROPE), np.float32)
    c_m, s_m = _rope_angles(np.arange(N_META))
    cos_m = np.concatenate([c_m, c_m, zpad(N_META)], axis=-1)
    sin_m = np.concatenate([-s_m, s_m, zpad(N_META)], axis=-1)
    c_r, s_r = _rope_angles(N_META + np.arange(S))
    cos_r = np.concatenate([c_r, c_r, zpad(S)], axis=-1)
    sin_r = np.concatenate([-s_r, s_r, zpad(S)], axis=-1)
    cos_t = np.ascontiguousarray(np.concatenate([c_r, c_r], axis=-1).T)
    sin_t = np.ascontiguousarray(np.concatenate([-s_r, s_r], axis=-1).T)

    k_meta, vt_meta, g_meta = pl.pallas_call(
        _meta_kernel,
        out_shape=(jax.ShapeDtypeStruct((N_META, HEADS * HEAD_PAD), BF16),
                   jax.ShapeDtypeStruct((HEADS * VDIM, N_META), BF16),
                   jax.ShapeDtypeStruct((N_META, CONV_W), F32)),
        name="meta_proj",
    )(meta_tokens.astype(F32), cos_m, sin_m, row(norm_g), w_in_p, row(kv_norm_g), w_k, w_vt)

    tok = lambda w: pl.BlockSpec((1, TILE, w), lambda b, t: (b, t, 0))
    qt_s, k_s, vt_s, ga_s, yc_s = pl.pallas_call(
        _proj_kernel,
        grid=(B, nblk),
        in_specs=[tok(D_MODEL),
                  pl.BlockSpec((TILE, LANES), lambda b, t: (t, 0)),
                  pl.BlockSpec((TILE, LANES), lambda b, t: (t, 0)),
                  pl.BlockSpec((ROPE, TILE), lambda b, t: (0, t)),
                  pl.BlockSpec((ROPE, TILE), lambda b, t: (0, t)),
                  _full((N_META, CONV_W)), _full((1, D_MODEL)), _full((D_MODEL, P_COLS)),
                  _full((1, Q_LORA)), _full((HEADS * HEAD_PAD, Q_LORA)),
                  _full((1, KV_LORA)), _full((KV_LORA, HEADS * NOPE)), _full((HEADS * VDIM, KV_LORA)),
                  _full((3, CONV_W)), _full((1, CONV_W)), _full((CONV_W, CONV_W))],
        out_specs=[pl.BlockSpec((1, HEADS * HEAD_PAD, TILE), lambda b, t: (b, 0, t)),
                   tok(HEADS * HEAD_PAD),
                   pl.BlockSpec((1, 1, HEADS * VDIM, TILE), lambda b, t: (b, t, 0, 0)),
                   tok(HEADS * VDIM), tok(CONV_W)],
        out_shape=[jax.ShapeDtypeStruct((B, HEADS * HEAD_PAD, S), BF16),
                   jax.ShapeDtypeStruct((B, S, HEADS * HEAD_PAD), BF16),
                   jax.ShapeDtypeStruct((B, nblk, HEADS * VDIM, TILE), BF16),
                   jax.ShapeDtypeStruct((B, S, HEADS * VDIM), BF16),
                   jax.ShapeDtypeStruct((B, S, CONV_W), BF16)],
        scratch_shapes=[pltpu.VMEM((TILE + 8, CONV_W), F32)],
        compiler_params=pltpu.CompilerParams(dimension_semantics=("arbitrary", "arbitrary"),
                                             vmem_limit_bytes=VMEM_LIMIT_BYTES),
        name="proj",
    )(x, cos_r, sin_r, cos_t, sin_t, g_meta, row(norm_g), w_in_p, row(q_norm_g), wq_t,
      row(kv_norm_g), w_k, w_vt, conv_w.astype(F32), row(conv_out_g), gmat)

    qtile = lambda w: pl.BlockSpec((1, TILE, w), lambda b, i: (b, i, 0))
    out = pl.pallas_call(
        _attn_kernel,
        grid=(B, nblk),
        in_specs=[pl.BlockSpec((1, HEADS * HEAD_PAD, TILE), lambda b, i: (b, 0, i)),
                  pl.BlockSpec((1, S, HEADS * HEAD_PAD), lambda b, i: (b, 0, 0)),
                  pl.BlockSpec((1, nblk, HEADS * VDIM, TILE), lambda b, i: (b, 0, 0, 0)),
                  _full((N_META, HEADS * HEAD_PAD)), _full((HEADS * VDIM, N_META)),
                  qtile(HEADS * VDIM), qtile(CONV_W), qtile(D_MODEL),
                  _full((D_MODEL, D_MODEL)), _full((1, HEADS * VDIM)), _full((1, D_MODEL))],
        out_specs=qtile(D_MODEL),
        out_shape=jax.ShapeDtypeStruct((B, S, D_MODEL), F32),
        scratch_shapes=[pltpu.VMEM((HEADS, HEAD_PAD, TILE), BF16),
                        pltpu.VMEM((HEADS, 1, TILE), F32),
                        pltpu.VMEM((HEADS, VDIM + ONES_ROWS, TILE), F32),
                        pltpu.VMEM((2, TILE, TILE), BF16),
                        pltpu.VMEM((1, TILE), F32),
                        pltpu.VMEM((TILE, D_MODEL), BF16)],
        compiler_params=pltpu.CompilerParams(dimension_semantics=("arbitrary", "arbitrary"),
                                             vmem_limit_bytes=VMEM_LIMIT_BYTES),
        name="attn_out",
    )(qt_s, k_s, vt_s, k_meta, vt_meta, ga_s, yc_s, x, w_out_b, row(attn_out_g), row(final_norm_g))
    return out


def kernel(x, meta_tokens, norm_g, w_in, q_norm_g, w_q_up, kv_norm_g, w_kv_up, conv_w,
           attn_out_g, conv_out_g, w_out, final_norm_g):
    assert norm_g.shape[0] == 1, "single-layer block"
    return _layer(x, meta_tokens, norm_g[0], w_in[0], q_norm_g[0], w_q_up[0], kv_norm_g[0],
                  w_kv_up[0], conv_w[0], attn_out_g[0], conv_out_g[0], w_out[0], final_norm_g)
```

```python
import functools
import math

import jax
import jax.numpy as jnp
import numpy as np
from jax import lax
from jax.experimental import pallas as pl
from jax.experimental.pallas import tpu as pltpu

F32 = jnp.float32
BF16 = jnp.bfloat16

D_MODEL = 1024
N_META = 16
HEADS = 4
NOPE = 128
ROPE = 64
VDIM = 128
Q_LORA = 256
KV_LORA = 128
CONV_W = 512
CONV_GROUP = 64
ROPE_THETA = 10000.0
ATTN_SCALE = (NOPE + ROPE) ** -0.5
NEG_INF = -1e30
EPS = 1e-6

LANES = 128
HEAD_PAD = 2 * LANES
P_COLS = 3072
C_Q, C_KV, C_KROPE, C_Z_ATTN, C_CONV_B, C_CONV_C, C_CONV_H, C_Z_CONV = (
    0, 256, 384, 512, 1024, 1536, 2048, 2560)

TILE = 512
HALF = TILE // 2
QK_AHEAD = 2
ONES_ROWS = 16
REF_ROW = NOPE + ROPE
REF_ROWS = 16
OUT_ROWS = 256
RISE_LIMIT = 100.0
PROJ_CHUNKS = 6
VMEM_LIMIT_BYTES = 56 * 2 ** 20
Q_SCALE = ATTN_SCALE * math.log2(math.e)


def _rms(x, g):
    ms = jnp.mean(x * x, axis=-1, keepdims=True)
    return x * lax.rsqrt(ms + EPS) * g


def _silu(x):
    hx = (0.5 * x).astype(BF16)
    return hx + hx * jnp.tanh(hx)


def _dot(a, b):
    return jnp.dot(a, b, preferred_element_type=F32)


def _dot_nt(a, b):
    return lax.dot_general(a, b, (((1,), (1,)), ((), ())), preferred_element_type=F32)


def _in_proj(x, norm_g, w_in):
    u = _rms(x, norm_g).astype(BF16)
    return _dot(u, w_in)


def _keys_values(p, cos, sin, kv_g, w_k, w_vt):
    c_kv = _rms(p[:, C_KV:C_KV + KV_LORA], kv_g).astype(BF16)
    k_nope = _dot(c_kv, w_k)
    v_t = _dot_nt(w_vt, c_kv)
    kr = p[:, C_KROPE:C_KROPE + LANES]
    k_pe = kr * cos + pltpu.roll(kr, ROPE, 1) * sin
    lane = lax.broadcasted_iota(jnp.int32, k_pe.shape, 1)
    k_pe = jnp.where(lane == REF_ROW - NOPE, 1.0, k_pe).astype(BF16)
    ks = []
    for h in range(HEADS):
        ks.append(k_nope[:, h * NOPE:(h + 1) * NOPE].astype(BF16))
        ks.append(k_pe)
    return ks, v_t.astype(BF16)


def _meta_kernel(x_ref, cos_ref, sin_ref, norm_g_ref, w_in_ref, kv_g_ref, w_k_ref, w_vt_ref,
                 k_out, vt_out, g_out):
    p = _in_proj(x_ref[...], norm_g_ref[...], w_in_ref[...])
    ks, v_t = _keys_values(p, cos_ref[...], sin_ref[...], kv_g_ref[...], w_k_ref[...], w_vt_ref[...])
    for i, kk in enumerate(ks):
        k_out[:, i * LANES:(i + 1) * LANES] = kk
    vt_out[...] = v_t
    g_out[...] = p[:, C_CONV_C:C_CONV_C + CONV_W] * p[:, C_CONV_H:C_CONV_H + CONV_W]


def _ones_rows(v_t):
    return jnp.concatenate([v_t, jnp.ones((ONES_ROWS, v_t.shape[1]), BF16)], axis=0)


def _bf16_exact(x):
    return x.astype(BF16).astype(F32)


def _projection_pieces(x_ref, cos_ref, sin_ref, cos_t_ref, sin_t_ref, norm_g_ref, w_in_ref,
                       q_g_ref, w_qt_ref, kv_g_ref, w_k_ref, w_vt_ref, conv_w_ref, conv_g_ref, gmat_ref,
                       p_scr, gbuf, k_dst, vt_dst, q_dst, ga_dst, y_dst):
    tm = TILE
    st = {}
    chunk = P_COLS // PROJ_CHUNKS

    def norm_x():
        st["u"] = _rms(x_ref[0], norm_g_ref[...]).astype(BF16)

    def project(i):
        def f():
            p_scr[:, i * chunk:(i + 1) * chunk] = _dot(st["u"], w_in_ref[:, i * chunk:(i + 1) * chunk])
        return f

    def queries():
        c_q = _rms(p_scr[:, C_Q:C_Q + Q_LORA], q_g_ref[...] * Q_SCALE).astype(BF16)
        q_t = _dot_nt(w_qt_ref[...], c_q)
        cos_t = cos_t_ref[...]
        sin_t = sin_t_ref[...]
        for h in range(HEADS):
            r = h * HEAD_PAD
            q_dst[h, 0:NOPE, :] = q_t[r:r + NOPE, :].astype(BF16)
            q_pe = q_t[r + NOPE:r + NOPE + ROPE, :] * cos_t + q_t[r + NOPE + ROPE:r + HEAD_PAD, :] * sin_t
            q_dst[h, NOPE:NOPE + ROPE, :] = q_pe.astype(BF16)
            q_dst[h, NOPE + ROPE:HEAD_PAD, :] = jnp.zeros((ROPE, tm), BF16)

    def keys_values():
        ks, v_t = _keys_values(p_scr, cos_ref[...], sin_ref[...], kv_g_ref[...], w_k_ref[...], w_vt_ref[...])
        for i, kk in enumerate(ks):
            k_dst[:, i * LANES:(i + 1) * LANES] = kk
        vt_dst[...] = v_t

    def gate():
        ga_dst[...] = _silu(p_scr[:, C_Z_ATTN:C_Z_ATTN + HEADS * VDIM])

    def conv():
        g = p_scr[:, C_CONV_C:C_CONV_C + CONV_W] * p_scr[:, C_CONV_H:C_CONV_H + CONV_W]
        gbuf[8:8 + tm, :] = g
        g1 = gbuf[7:7 + tm, :]
        g2 = gbuf[6:6 + tm, :]
        cw = conv_w_ref[...]
        cv = cw[0:1, :] * g2 + cw[1:2, :] * g1 + cw[2:3, :] * g
        gbuf[0:8, :] = gbuf[tm:tm + 8, :]
        yc = p_scr[:, C_CONV_B:C_CONV_B + CONV_W] * cv
        st["yc"] = yc
        st["ssum"] = _dot((yc * yc).astype(BF16), gmat_ref[...])

    def conv_out():
        ycn = st["yc"] * lax.rsqrt(st["ssum"] * (1.0 / CONV_GROUP) + EPS) * conv_g_ref[...]
        y_dst[:, HEADS * VDIM:] = (ycn * _silu(p_scr[:, C_Z_CONV:C_Z_CONV + CONV_W])).astype(BF16)

    assert PROJ_CHUNKS == 6 and chunk == 512
    return [norm_x, project(0), queries, project(1), keys_values, project(2), gate, project(3),
            project(4), conv, project(5), conv_out]


def _attention_tile(c, k_scr, vt_scr, km_ref, vtm_ref, q_aug, m_scr, acc_scr, pr_scr, rise_scr):
    tri_full = (lax.broadcasted_iota(jnp.int32, (HALF, TILE), 0)
                <= lax.broadcasted_iota(jnp.int32, (HALF, TILE), 1))
    tri_half = tri_full[:, :HALF]

    def k_of(h, r0, r1):
        return k_scr[r0:r1, h * HEAD_PAD:(h + 1) * HEAD_PAD]

    def v_of(h, j, lo, hi):
        return vt_scr[j, h * VDIM:(h + 1) * VDIM, lo:hi]

    tasks = []
    for h in range(HEADS):
        tasks.append((h, lambda h=h: km_ref[:, h * HEAD_PAD:(h + 1) * HEAD_PAD],
                      lambda h=h: vtm_ref[h * VDIM:(h + 1) * VDIM, :], 0, TILE, None, True))
    for j in range(c):
        for h in range(HEADS):
            tasks.append((h, lambda h=h, j=j: k_of(h, j * TILE, (j + 1) * TILE),
                          lambda h=h, j=j: v_of(h, j, 0, TILE), 0, TILE, None, False))
    for h in range(HEADS):
        tasks.append((h, lambda h=h: k_of(h, c * TILE, c * TILE + HALF),
                      lambda h=h: v_of(h, c, 0, HALF), 0, TILE, tri_full, False))
    for h in range(HEADS):
        tasks.append((h, lambda h=h: k_of(h, c * TILE + HALF, (c + 1) * TILE),
                      lambda h=h: v_of(h, c, HALF, TILE), HALF, TILE, tri_half, False))
    assert QK_AHEAD < HEADS and all(t[0] == n % HEADS for n, t in enumerate(tasks))

    rise_scr[...] = jnp.zeros(rise_scr.shape, F32)

    def scores(task):
        h, k_fn, _, lo, hi, mask, _ = task
        s = _dot(k_fn(), q_aug[h, :, lo:hi])
        return s if mask is None else jnp.where(mask, s, NEG_INF)

    def softmax(n, task, s):
        h, _, _, lo, hi, _, first = task
        nkeys, ncols = s.shape
        blk_max = jnp.max(s, axis=0, keepdims=True)
        if first:
            m_new, beta = _bf16_exact(blk_max), None
            p = jnp.exp2(s - m_new)
        else:
            rise = jnp.maximum(blk_max, 0.0)
            p = jnp.exp2(s)
            m_old = m_scr[h, :, lo:hi]
            m_new = _bf16_exact(m_old + rise)
            beta = jnp.exp2(m_old - m_new)
            rise_scr[:, lo:hi] = jnp.maximum(rise_scr[:, lo:hi], rise)
        pr_scr[n % 2, 0:nkeys, 0:ncols] = p.astype(BF16)
        m_scr[h, :, lo:hi] = m_new
        q_aug[h, REF_ROW:REF_ROW + REF_ROWS, lo:hi] = jnp.broadcast_to(-m_new, (REF_ROWS, ncols)).astype(BF16)
        return nkeys, ncols, beta

    def values(n, task, nkeys, ncols, beta):
        h, _, v_fn, lo, hi, _, _ = task
        pv = _dot(_ones_rows(v_fn()), pr_scr[n % 2, 0:nkeys, 0:ncols])
        if beta is None:
            acc_scr[h, :, lo:hi] = pv
        else:
            acc_scr[h, :, lo:hi] = (acc_scr[h, :, lo:hi] + pv) * beta

    pending = [scores(t) for t in tasks[:QK_AHEAD]]
    prev = None
    for n, task in enumerate(tasks):
        if prev is not None:
            values(*prev)
        if n + QK_AHEAD < len(tasks):
            pending.append(scores(tasks[n + QK_AHEAD]))
        prev = (n, task) + softmax(n, task, pending.pop(0))
    values(*prev)


def _attention_tile_two_pass(c, k_scr, vt_scr, km_ref, vtm_ref, q_aug, m_scr, acc_scr):
    tri_full = (lax.broadcasted_iota(jnp.int32, (HALF, TILE), 0)
                <= lax.broadcasted_iota(jnp.int32, (HALF, TILE), 1))
    tri_half = tri_full[:, :HALF]
    for h in range(HEADS):
        q_aug[h, REF_ROW:REF_ROW + REF_ROWS, :] = jnp.zeros((REF_ROWS, TILE), BF16)

    def update(h, s, v_t, lo, hi):
        m_old = m_scr[h, :, lo:hi]
        m_new = jnp.maximum(m_old, jnp.max(s, axis=0, keepdims=True))
        pv = _dot(_ones_rows(v_t), jnp.exp2(s - m_new).astype(BF16))
        acc_scr[h, :, lo:hi] = jnp.exp2(m_old - m_new) * acc_scr[h, :, lo:hi] + pv
        m_scr[h, :, lo:hi] = m_new

    for h in range(HEADS):
        s = _dot(km_ref[:, h * HEAD_PAD:(h + 1) * HEAD_PAD], q_aug[h])
        m = jnp.max(s, axis=0, keepdims=True)
        m_scr[h] = m
        acc_scr[h] = _dot(_ones_rows(vtm_ref[h * VDIM:(h + 1) * VDIM, :]), jnp.exp2(s - m).astype(BF16))

    def full_block(j, carry):
        start = pl.multiple_of(j * TILE, TILE)
        for h in range(HEADS):
            k = k_scr[pl.ds(start, TILE), h * HEAD_PAD:(h + 1) * HEAD_PAD]
            update(h, _dot(k, q_aug[h]), vt_scr[j, h * VDIM:(h + 1) * VDIM, :], 0, TILE)
        return carry

    lax.fori_loop(0, c, full_block, 0)

    for h in range(HEADS):
        k = k_scr[c * TILE:c * TILE + HALF, h * HEAD_PAD:(h + 1) * HEAD_PAD]
        s = jnp.where(tri_full, _dot(k, q_aug[h]), NEG_INF)
        update(h, s, vt_scr[c, h * VDIM:(h + 1) * VDIM, 0:HALF], 0, TILE)
    for h in range(HEADS):
        k = k_scr[c * TILE + HALF:(c + 1) * TILE, h * HEAD_PAD:(h + 1) * HEAD_PAD]
        s = jnp.where(tri_half, _dot(k, q_aug[h, :, HALF:TILE]), NEG_INF)
        update(h, s, vt_scr[c, h * VDIM:(h + 1) * VDIM, HALF:TILE], HALF, TILE)


def _finish_tile(acc_scr, ga_src, y_src, x_ref, w_out_ref, attn_g_ref, final_g_ref, o_ref):
    for h in range(HEADS):
        o_t = acc_scr[h, 0:VDIM, :] * (1.0 / acc_scr[h, VDIM:VDIM + 1, :])
        o = _rms(o_t.T, attn_g_ref[:, h * VDIM:(h + 1) * VDIM])
        y_src[:, h * VDIM:(h + 1) * VDIM] = (o * ga_src[:, h * VDIM:(h + 1) * VDIM].astype(F32)).astype(BF16)
    for r0 in range(0, TILE, OUT_ROWS):
        mix = _dot(y_src[r0:r0 + OUT_ROWS, :], w_out_ref[...])
        o_ref[0, r0:r0 + OUT_ROWS, :] = _rms(x_ref[0, r0:r0 + OUT_ROWS, :] + mix, final_g_ref[...])


def _layer_kernel(nblk, x_ref, xres_ref, cos_ref, sin_ref, cos_t_ref, sin_t_ref, ginit_ref, km_ref, vtm_ref,
                  norm_g_ref, w_in_ref, q_g_ref, w_qt_ref, kv_g_ref, w_k_ref, w_vt_ref, conv_w_ref,
                  conv_g_ref, gmat_ref, w_out_ref, attn_g_ref, final_g_ref, o_ref,
                  p_scr, gbuf, k_scr, vt_scr, k_stage, vt_stage, q_a, q_b, ga_a, ga_b, y_a, y_b,
                  m_scr, acc_scr, pr_scr, rise_scr):
    s = pl.program_id(0)
    bufs = ((q_a, ga_a, y_a), (q_b, ga_b, y_b))

    @pl.when(lax.rem(s, nblk) == 0)
    def _():
        gbuf[0:8, :] = ginit_ref[8:16, :]

    def projection(blk):
        q_dst, ga_dst, y_dst = bufs[blk % 2]
        k_dst = k_stage if blk == 0 else k_scr.at[blk * TILE:(blk + 1) * TILE, :]
        vt_dst = vt_stage if blk == 0 else vt_scr.at[blk]
        return _projection_pieces(x_ref, cos_ref, sin_ref, cos_t_ref, sin_t_ref, norm_g_ref, w_in_ref,
                                  q_g_ref, w_qt_ref, kv_g_ref, w_k_ref, w_vt_ref, conv_w_ref, conv_g_ref,
                                  gmat_ref, p_scr, gbuf, k_dst, vt_dst, q_dst, ga_dst, y_dst)

    @pl.when(s == 0)
    def _():
        rise_scr[...] = jnp.zeros(rise_scr.shape, F32)
        for step in projection(0):
            step()

    def attends(c):
        return jnp.logical_and(s > 0, lax.rem(s - 1, nblk) == c)

    def finisher(c):
        _, ga_cur, y_cur = bufs[c % 2]
        return functools.partial(_finish_tile, acc_scr, ga_cur, y_cur, xres_ref, w_out_ref, attn_g_ref,
                                 final_g_ref, o_ref)

    for c in range(nblk):
        @pl.when(attends(c))
        def _(c=c):
            if c == 0:
                k_scr[0:TILE, :] = k_stage[...]
                vt_scr[0] = vt_stage[...]
            _attention_tile(c, k_scr, vt_scr, km_ref, vtm_ref, bufs[c % 2][0], m_scr, acc_scr, pr_scr, rise_scr)
            for step in projection((c + 1) % nblk):
                step()
            finisher(c)()

    overflow = jnp.max(rise_scr[...]) > RISE_LIMIT
    for c in range(nblk):
        @pl.when(jnp.logical_and(attends(c), overflow))
        def _(c=c):
            _attention_tile_two_pass(c, k_scr, vt_scr, km_ref, vtm_ref, bufs[c % 2][0], m_scr, acc_scr)
            finisher(c)()


def _rope_angles(pos):
    half = ROPE // 2
    inv_freq = (1.0 / (ROPE_THETA ** (np.arange(half, dtype=np.float32) / half))).astype(np.float32)
    ang = pos.astype(np.float32)[:, None] * inv_freq[None, :]
    return np.cos(ang).astype(np.float32), np.sin(ang).astype(np.float32)


def _swap_halves(w):
    half = w.shape[-1] // 2
    return jnp.concatenate([w[..., half:], w[..., :half]], axis=-1)


def _layer(x, meta_tokens, norm_g, w_in, q_norm_g, w_q_up, kv_norm_g, w_kv_up, conv_w,
           attn_out_g, conv_out_g, w_out, final_norm_g):
    B, S, D = x.shape
    assert D == D_MODEL and S % TILE == 0
    assert meta_tokens.shape == (N_META, D_MODEL)
    nblk = S // TILE
    assert nblk % 2 == 0, "per-tile buffers alternate a / b by block parity"

    sec = lambda lo, n: w_in[:, lo:lo + n].astype(BF16)
    w_in_p = jnp.concatenate([sec(0, 448), _swap_halves(sec(384, 64)), sec(448, 2560)], axis=1)
    wq = w_q_up.reshape(Q_LORA, HEADS, NOPE + ROPE)
    wq_t = jnp.concatenate([wq, _swap_halves(wq[..., NOPE:])], axis=-1).reshape(Q_LORA, HEADS * HEAD_PAD).T.astype(BF16)
    wkv = w_kv_up.reshape(KV_LORA, HEADS, NOPE + VDIM)
    w_k = wkv[..., :NOPE].reshape(KV_LORA, HEADS * NOPE).astype(BF16)
    w_vt = wkv[..., NOPE:].reshape(KV_LORA, HEADS * VDIM).T.astype(BF16)
    w_out_b = w_out.astype(BF16)
    gid = np.arange(CONV_W) // CONV_GROUP
    gmat = jnp.asarray((gid[:, None] == gid[None, :]).astype(np.float32), dtype=BF16)
    row = lambda v: v.reshape(1, -1).astype(F32)

    zpad = lambda n: np.zeros((n, LANES - ROPE), np.float32)
    c_m, s_m = _rope_angles(np.arange(N_META))
    cos_m = np.concatenate([c_m, c_m, zpad(N_META)], axis=-1)
    sin_m = np.concatenate([-s_m, s_m, zpad(N_META)], axis=-1)
    c_r, s_r = _rope_angles(N_META + np.arange(S))
    cos_r = np.concatenate([c_r, c_r, zpad(S)], axis=-1)
    sin_r = np.concatenate([-s_r, s_r, zpad(S)], axis=-1)
    cos_t = np.ascontiguousarray(np.concatenate([c_r, c_r], axis=-1).T)
    sin_t = np.ascontiguousarray(np.concatenate([-s_r, s_r], axis=-1).T)

    k_meta, vt_meta, g_meta = pl.pallas_call(
        _meta_kernel,
        out_shape=(jax.ShapeDtypeStruct((N_META, HEADS * HEAD_PAD), BF16),
                   jax.ShapeDtypeStruct((HEADS * VDIM, N_META), BF16),
                   jax.ShapeDtypeStruct((N_META, CONV_W), F32)),
        name="meta_proj",
    )(meta_tokens.astype(F32), cos_m, sin_m, row(norm_g), w_in_p, row(kv_norm_g), w_k, w_vt)

    ntiles = B * nblk
    t_in = lambda s: jnp.minimum(s, ntiles - 1)
    t_out = lambda s: jnp.maximum(s - 1, 0)
    once = lambda shape: pl.BlockSpec(shape, lambda s: (0,) * len(shape), pipeline_mode=pl.Buffered(1))
    out = pl.pallas_call(
        functools.partial(_layer_kernel, nblk),
        grid=(ntiles + 1,),
        in_specs=[pl.BlockSpec((1, TILE, D_MODEL), lambda s: (t_in(s) // nblk, t_in(s) % nblk, 0)),
                  pl.BlockSpec((1, TILE, D_MODEL), lambda s: (t_out(s) // nblk, t_out(s) % nblk, 0)),
                  pl.BlockSpec((TILE, LANES), lambda s: (t_in(s) % nblk, 0)),
                  pl.BlockSpec((TILE, LANES), lambda s: (t_in(s) % nblk, 0)),
                  pl.BlockSpec((ROPE, TILE), lambda s: (0, t_in(s) % nblk)),
                  pl.BlockSpec((ROPE, TILE), lambda s: (0, t_in(s) % nblk)),
                  once((N_META, CONV_W)), once((N_META, HEADS * HEAD_PAD)), once((HEADS * VDIM, N_META)),
                  once((1, D_MODEL)), once((D_MODEL, P_COLS)),
                  once((1, Q_LORA)), once((HEADS * HEAD_PAD, Q_LORA)),
                  once((1, KV_LORA)), once((KV_LORA, HEADS * NOPE)), once((HEADS * VDIM, KV_LORA)),
                  once((3, CONV_W)), once((1, CONV_W)), once((CONV_W, CONV_W)),
                  once((D_MODEL, D_MODEL)), once((1, HEADS * VDIM)), once((1, D_MODEL))],
        out_specs=pl.BlockSpec((1, TILE, D_MODEL), lambda s: (t_out(s) // nblk, t_out(s) % nblk, 0)),
        out_shape=jax.ShapeDtypeStruct((B, S, D_MODEL), F32),
        scratch_shapes=[pltpu.VMEM((TILE, P_COLS), F32),
                        pltpu.VMEM((TILE + 8, CONV_W), F32),
                        pltpu.VMEM((S, HEADS * HEAD_PAD), BF16),
                        pltpu.VMEM((nblk, HEADS * VDIM, TILE), BF16),
                        pltpu.VMEM((TILE, HEADS * HEAD_PAD), BF16),
                        pltpu.VMEM((HEADS * VDIM, TILE), BF16),
                        pltpu.VMEM((HEADS, HEAD_PAD, TILE), BF16),
                        pltpu.VMEM((HEADS, HEAD_PAD, TILE), BF16),
                        pltpu.VMEM((TILE, HEADS * VDIM), BF16),
                        pltpu.VMEM((TILE, HEADS * VDIM), BF16),
                        pltpu.VMEM((TILE, D_MODEL), BF16),
                        pltpu.VMEM((TILE, D_MODEL), BF16),
                        pltpu.VMEM((HEADS, 1, TILE), F32),
                        pltpu.VMEM((HEADS, VDIM + ONES_ROWS, TILE), F32),
                        pltpu.VMEM((2, TILE, TILE), BF16),
                        pltpu.VMEM((1, TILE), F32)],
        compiler_params=pltpu.CompilerParams(dimension_semantics=("arbitrary",),
                                             vmem_limit_bytes=VMEM_LIMIT_BYTES),
        name="layer",
    )(x, x, cos_r, sin_r, cos_t, sin_t, g_meta, k_meta, vt_meta, row(norm_g), w_in_p, row(q_norm_g), wq_t,
      row(kv_norm_g), w_k, w_vt, conv_w.astype(F32), row(conv_out_g), gmat, w_out_b, row(attn_out_g),
      row(final_norm_g))
    return out


def kernel(x, meta_tokens, norm_g, w_in, q_norm_g, w_q_up, kv_norm_g, w_kv_up, conv_w,
           attn_out_g, conv_out_g, w_out, final_norm_g):
    assert norm_g.shape[0] == 1, "single-layer block"
    return _layer(x, meta_tokens, norm_g[0], w_in[0], q_norm_g[0], w_q_up[0], kv_norm_g[0],
                  w_kv_up[0], conv_w[0], attn_out_g[0], conv_out_g[0], w_out[0], final_norm_g)
```

```python
import functools
import math

import jax
import jax.numpy as jnp
import numpy as np
from jax import lax
from jax.experimental import pallas as pl
from jax.experimental.pallas import tpu as pltpu

F32 = jnp.float32
BF16 = jnp.bfloat16

D_MODEL = 1024
N_META = 16
HEADS = 4
NOPE = 128
ROPE = 64
VDIM = 128
Q_LORA = 256
KV_LORA = 128
CONV_W = 512
CONV_GROUP = 64
ROPE_THETA = 10000.0
ATTN_SCALE = (NOPE + ROPE) ** -0.5
NEG_INF = -1e30
EPS = 1e-6

LANES = 128
HEAD_PAD = 2 * LANES
P_COLS = 3072
C_Q, C_KV, C_KROPE, C_Z_ATTN, C_CONV_B, C_CONV_C, C_CONV_H, C_Z_CONV = (
    0, 256, 384, 512, 1024, 1536, 2048, 2560)

TILE = 512
HALF = TILE // 2
QK_AHEAD = 2
ONES_ROWS = 16
REF_ROW = NOPE + ROPE
REF_ROWS = 16
OUT_ROWS = 256
RISE_LIMIT = 100.0
PROJ_CHUNKS = 6
VMEM_LIMIT_BYTES = 56 * 2 ** 20
Q_SCALE = ATTN_SCALE * math.log2(math.e)


def _rms(x, g):
    ms = jnp.mean(x * x, axis=-1, keepdims=True)
    return x * lax.rsqrt(ms + EPS) * g


def _silu(x):
    hx = (0.5 * x).astype(BF16)
    return hx + hx * jnp.tanh(hx)


def _dot(a, b):
    return jnp.dot(a, b, preferred_element_type=F32)


def _dot_nt(a, b):
    return lax.dot_general(a, b, (((1,), (1,)), ((), ())), preferred_element_type=F32)


def _in_proj(x, norm_g, w_in):
    u = _rms(x, norm_g).astype(BF16)
    return _dot(u, w_in)


def _keys_values(p, cos, sin, kv_g, w_k, w_vt):
    c_kv = _rms(p[:, C_KV:C_KV + KV_LORA], kv_g).astype(BF16)
    k_nope = _dot(c_kv, w_k)
    v_t = _dot_nt(w_vt, c_kv)
    kr = p[:, C_KROPE:C_KROPE + LANES]
    k_pe = kr * cos + pltpu.roll(kr, ROPE, 1) * sin
    lane = lax.broadcasted_iota(jnp.int32, k_pe.shape, 1)
    k_pe = jnp.where(lane == REF_ROW - NOPE, 1.0, k_pe).astype(BF16)
    ks = []
    for h in range(HEADS):
        ks.append(k_nope[:, h * NOPE:(h + 1) * NOPE].astype(BF16))
        ks.append(k_pe)
    return ks, v_t.astype(BF16)


def _meta_kernel(x_ref, cos_ref, sin_ref, norm_g_ref, w_in_ref, kv_g_ref, w_k_ref, w_vt_ref,
                 k_out, vt_out, g_out):
    p = _in_proj(x_ref[...], norm_g_ref[...], w_in_ref[...])
    ks, v_t = _keys_values(p, cos_ref[...], sin_ref[...], kv_g_ref[...], w_k_ref[...], w_vt_ref[...])
    for i, kk in enumerate(ks):
        k_out[:, i * LANES:(i + 1) * LANES] = kk
    vt_out[...] = v_t
    g_out[...] = p[:, C_CONV_C:C_CONV_C + CONV_W] * p[:, C_CONV_H:C_CONV_H + CONV_W]


def _ones_rows(v_t):
    return jnp.concatenate([v_t, jnp.ones((ONES_ROWS, v_t.shape[1]), BF16)], axis=0)


def _bf16_exact(x):
    return x.astype(BF16).astype(F32)


def _projection_pieces(x_ref, cos_ref, sin_ref, cos_t_ref, sin_t_ref, norm_g_ref, w_in_ref,
                       q_g_ref, w_qt_ref, kv_g_ref, w_k_ref, w_vt_ref, conv_w_ref, conv_g_ref, gmat_ref,
                       p_scr, gbuf, k_dst, vt_dst, q_dst, ga_dst, y_dst):
    tm = TILE
    st = {}
    chunk = P_COLS // PROJ_CHUNKS

    def norm_x():
        st["u"] = _rms(x_ref[0], norm_g_ref[...]).astype(BF16)

    def project(i):
        def f():
            p_scr[:, i * chunk:(i + 1) * chunk] = _dot(st["u"], w_in_ref[:, i * chunk:(i + 1) * chunk])
        return f

    def queries():
        c_q = _rms(p_scr[:, C_Q:C_Q + Q_LORA], q_g_ref[...] * Q_SCALE).astype(BF16)
        q_t = _dot_nt(w_qt_ref[...], c_q)
        cos_t = cos_t_ref[...]
        sin_t = sin_t_ref[...]
        for h in range(HEADS):
            r = h * HEAD_PAD
            q_dst[h, 0:NOPE, :] = q_t[r:r + NOPE, :].astype(BF16)
            q_pe = q_t[r + NOPE:r + NOPE + ROPE, :] * cos_t + q_t[r + NOPE + ROPE:r + HEAD_PAD, :] * sin_t
            q_dst[h, NOPE:NOPE + ROPE, :] = q_pe.astype(BF16)
            q_dst[h, NOPE + ROPE:HEAD_PAD, :] = jnp.zeros((ROPE, tm), BF16)

    def keys_values():
        ks, v_t = _keys_values(p_scr, cos_ref[...], sin_ref[...], kv_g_ref[...], w_k_ref[...], w_vt_ref[...])
        for i, kk in enumerate(ks):
            k_dst[:, i * LANES:(i + 1) * LANES] = kk
        vt_dst[...] = v_t

    def gate():
        ga_dst[...] = _silu(p_scr[:, C_Z_ATTN:C_Z_ATTN + HEADS * VDIM])

    def conv():
        g = p_scr[:, C_CONV_C:C_CONV_C + CONV_W] * p_scr[:, C_CONV_H:C_CONV_H + CONV_W]
        gbuf[8:8 + tm, :] = g
        g1 = gbuf[7:7 + tm, :]
        g2 = gbuf[6:6 + tm, :]
        cw = conv_w_ref[...]
        cv = cw[0:1, :] * g2 + cw[1:2, :] * g1 + cw[2:3, :] * g
        gbuf[0:8, :] = gbuf[tm:tm + 8, :]
        yc = p_scr[:, C_CONV_B:C_CONV_B + CONV_W] * cv
        st["yc"] = yc
        st["ssum"] = _dot((yc * yc).astype(BF16), gmat_ref[...])

    def conv_out():
        ycn = st["yc"] * lax.rsqrt(st["ssum"] * (1.0 / CONV_GROUP) + EPS) * conv_g_ref[...]
        y_dst[:, HEADS * VDIM:] = (ycn * _silu(p_scr[:, C_Z_CONV:C_Z_CONV + CONV_W])).astype(BF16)

    assert PROJ_CHUNKS == 6 and chunk == 512
    return [norm_x, project(0), queries, project(1), keys_values, project(2), gate, project(3),
            project(4), conv, project(5), conv_out]


def _attention_tile(c, k_scr, vt_scr, km_ref, vtm_ref, q_aug, m_scr, acc_scr, pr_scr, rise_scr):
    tri_full = (lax.broadcasted_iota(jnp.int32, (HALF, TILE), 0)
                <= lax.broadcasted_iota(jnp.int32, (HALF, TILE), 1))
    tri_half = tri_full[:, :HALF]

    def k_of(h, r0, r1):
        return k_scr[r0:r1, h * HEAD_PAD:(h + 1) * HEAD_PAD]

    def v_of(h, j, lo, hi):
        return vt_scr[j, h * VDIM:(h + 1) * VDIM, lo:hi]

    tasks = []
    for h in range(HEADS):
        tasks.append((h, lambda h=h: km_ref[:, h * HEAD_PAD:(h + 1) * HEAD_PAD],
                      lambda h=h: vtm_ref[h * VDIM:(h + 1) * VDIM, :], 0, TILE, None, True))
    for j in range(c):
        for h in range(HEADS):
            tasks.append((h, lambda h=h, j=j: k_of(h, j * TILE, (j + 1) * TILE),
                          lambda h=h, j=j: v_of(h, j, 0, TILE), 0, TILE, None, False))
    for h in range(HEADS):
        tasks.append((h, lambda h=h: k_of(h, c * TILE, c * TILE + HALF),
                      lambda h=h: v_of(h, c, 0, HALF), 0, TILE, tri_full, False))
    for h in range(HEADS):
        tasks.append((h, lambda h=h: k_of(h, c * TILE + HALF, (c + 1) * TILE),
                      lambda h=h: v_of(h, c, HALF, TILE), HALF, TILE, tri_half, False))
    assert QK_AHEAD < HEADS and all(t[0] == n % HEADS for n, t in enumerate(tasks))

    rise_scr[...] = jnp.zeros(rise_scr.shape, F32)

    def scores(task):
        h, k_fn, _, lo, hi, mask, _ = task
        s = _dot(k_fn(), q_aug[h, :, lo:hi])
        return s if mask is None else jnp.where(mask, s, NEG_INF)

    def softmax(n, task, s):
        h, _, _, lo, hi, _, first = task
        nkeys, ncols = s.shape
        blk_max = jnp.max(s, axis=0, keepdims=True)
        if first:
            m_new, beta = _bf16_exact(blk_max), None
            p = jnp.exp2(s - m_new)
        else:
            rise = jnp.maximum(blk_max, 0.0)
            p = jnp.exp2(s)
            m_old = m_scr[h, :, lo:hi]
            m_new = _bf16_exact(m_old + rise)
            beta = jnp.exp2(m_old - m_new)
            rise_scr[:, lo:hi] = jnp.maximum(rise_scr[:, lo:hi], rise)
        pr_scr[n % 2, 0:nkeys, 0:ncols] = p.astype(BF16)
        m_scr[h, :, lo:hi] = m_new
        q_aug[h, REF_ROW:REF_ROW + REF_ROWS, lo:hi] = jnp.broadcast_to(-m_new, (REF_ROWS, ncols)).astype(BF16)
        return nkeys, ncols, beta

    def values(n, task, nkeys, ncols, beta):
        h, _, v_fn, lo, hi, _, _ = task
        pv = _dot(_ones_rows(v_fn()), pr_scr[n % 2, 0:nkeys, 0:ncols])
        if beta is None:
            acc_scr[h, :, lo:hi] = pv
        else:
            acc_scr[h, :, lo:hi] = (acc_scr[h, :, lo:hi] + pv) * beta

    pending = [scores(t) for t in tasks[:QK_AHEAD]]
    prev = None
    for n, task in enumerate(tasks):
        if prev is not None:
            values(*prev)
        if n + QK_AHEAD < len(tasks):
            pending.append(scores(tasks[n + QK_AHEAD]))
        prev = (n, task) + softmax(n, task, pending.pop(0))
    values(*prev)


def _attention_tile_two_pass(c, k_scr, vt_scr, km_ref, vtm_ref, q_aug, m_scr, acc_scr):
    tri_full = (lax.broadcasted_iota(jnp.int32, (HALF, TILE), 0)
                <= lax.broadcasted_iota(jnp.int32, (HALF, TILE), 1))
    tri_half = tri_full[:, :HALF]
    for h in range(HEADS):
        q_aug[h, REF_ROW:REF_ROW + REF_ROWS, :] = jnp.zeros((REF_ROWS, TILE), BF16)

    def update(h, s, v_t, lo, hi):
        m_old = m_scr[h, :, lo:hi]
        m_new = jnp.maximum(m_old, jnp.max(s, axis=0, keepdims=True))
        pv = _dot(_ones_rows(v_t), jnp.exp2(s - m_new).astype(BF16))
        acc_scr[h, :, lo:hi] = jnp.exp2(m_old - m_new) * acc_scr[h, :, lo:hi] + pv
        m_scr[h, :, lo:hi] = m_new

    for h in range(HEADS):
        s = _dot(km_ref[:, h * HEAD_PAD:(h + 1) * HEAD_PAD], q_aug[h])
        m = jnp.max(s, axis=0, keepdims=True)
        m_scr[h] = m
        acc_scr[h] = _dot(_ones_rows(vtm_ref[h * VDIM:(h + 1) * VDIM, :]), jnp.exp2(s - m).astype(BF16))

    def full_block(j, carry):
        start = pl.multiple_of(j * TILE, TILE)
        for h in range(HEADS):
            k = k_scr[pl.ds(start, TILE), h * HEAD_PAD:(h + 1) * HEAD_PAD]
            update(h, _dot(k, q_aug[h]), vt_scr[j, h * VDIM:(h + 1) * VDIM, :], 0, TILE)
        return carry

    lax.fori_loop(0, c, full_block, 0)

    for h in range(HEADS):
        k = k_scr[c * TILE:c * TILE + HALF, h * HEAD_PAD:(h + 1) * HEAD_PAD]
        s = jnp.where(tri_full, _dot(k, q_aug[h]), NEG_INF)
        update(h, s, vt_scr[c, h * VDIM:(h + 1) * VDIM, 0:HALF], 0, TILE)
    for h in range(HEADS):
        k = k_scr[c * TILE + HALF:(c + 1) * TILE, h * HEAD_PAD:(h + 1) * HEAD_PAD]
        s = jnp.where(tri_half, _dot(k, q_aug[h, :, HALF:TILE]), NEG_INF)
        update(h, s, vt_scr[c, h * VDIM:(h + 1) * VDIM, HALF:TILE], HALF, TILE)


def _finish_tile(acc_scr, ga_src, y_src, x_ref, w_out_ref, attn_g_ref, final_g_ref, o_ref):
    for h in range(HEADS):
        o_t = acc_scr[h, 0:VDIM, :] * (1.0 / acc_scr[h, VDIM:VDIM + 1, :])
        o = _rms(o_t.T, attn_g_ref[:, h * VDIM:(h + 1) * VDIM])
        y_src[:, h * VDIM:(h + 1) * VDIM] = (o * ga_src[:, h * VDIM:(h + 1) * VDIM].astype(F32)).astype(BF16)
    for r0 in range(0, TILE, OUT_ROWS):
        mix = _dot(y_src[r0:r0 + OUT_ROWS, :], w_out_ref[...])
        o_ref[0, r0:r0 + OUT_ROWS, :] = _rms(x_ref[0, r0:r0 + OUT_ROWS, :] + mix, final_g_ref[...])


def _layer_kernel(nblk, x_ref, xres_ref, cos_ref, sin_ref, cos_t_ref, sin_t_ref, ginit_ref, km_ref, vtm_ref,
                  norm_g_ref, w_in_hbm, q_g_ref, w_qt_hbm, kv_g_ref, w_k_ref, w_vt_ref, conv_w_ref,
                  conv_g_ref, gmat_hbm, w_out_hbm, attn_g_ref, final_g_ref, o_ref,
                  w_in_ref, w_qt_ref, gmat_ref, w_out_ref, load_sem,
                  p_scr, gbuf, k_scr, vt_scr, k_stage, vt_stage, q_a, q_b, ga_a, ga_b, y_a, y_b,
                  m_scr, acc_scr, pr_scr, rise_scr):
    s = pl.program_id(0)
    bufs = ((q_a, ga_a, y_a), (q_b, ga_b, y_b))

    @pl.when(s == 0)
    def _():
        loads = [pltpu.make_async_copy(src, dst, load_sem.at[i]) for i, (src, dst) in enumerate(
            ((w_in_hbm, w_in_ref), (w_qt_hbm, w_qt_ref), (gmat_hbm, gmat_ref), (w_out_hbm, w_out_ref)))]
        for cp in loads:
            cp.start()
        for cp in loads:
            cp.wait()

    @pl.when(lax.rem(s, nblk) == 0)
    def _():
        gbuf[0:8, :] = ginit_ref[8:16, :]

    def projection(blk):
        q_dst, ga_dst, y_dst = bufs[blk % 2]
        k_dst = k_stage if blk == 0 else k_scr.at[blk * TILE:(blk + 1) * TILE, :]
        vt_dst = vt_stage if blk == 0 else vt_scr.at[blk]
        return _projection_pieces(x_ref, cos_ref, sin_ref, cos_t_ref, sin_t_ref, norm_g_ref, w_in_ref,
                                  q_g_ref, w_qt_ref, kv_g_ref, w_k_ref, w_vt_ref, conv_w_ref, conv_g_ref,
                                  gmat_ref, p_scr, gbuf, k_dst, vt_dst, q_dst, ga_dst, y_dst)

    @pl.when(s == 0)
    def _():
        rise_scr[...] = jnp.zeros(rise_scr.shape, F32)
        for step in projection(0):
            step()

    def attends(c):
        return jnp.logical_and(s > 0, lax.rem(s - 1, nblk) == c)

    def finisher(c):
        _, ga_cur, y_cur = bufs[c % 2]
        return functools.partial(_finish_tile, acc_scr, ga_cur, y_cur, xres_ref, w_out_ref, attn_g_ref,
                                 final_g_ref, o_ref)

    for c in range(nblk):
        @pl.when(attends(c))
        def _(c=c):
            if c == 0:
                k_scr[0:TILE, :] = k_stage[...]
                vt_scr[0] = vt_stage[...]
            _attention_tile(c, k_scr, vt_scr, km_ref, vtm_ref, bufs[c % 2][0], m_scr, acc_scr, pr_scr, rise_scr)
            for step in projection((c + 1) % nblk):
                step()
            finisher(c)()

    overflow = jnp.max(rise_scr[...]) > RISE_LIMIT
    for c in range(nblk):
        @pl.when(jnp.logical_and(attends(c), overflow))
        def _(c=c):
            _attention_tile_two_pass(c, k_scr, vt_scr, km_ref, vtm_ref, bufs[c % 2][0], m_scr, acc_scr)
            finisher(c)()


def _rope_angles(pos):
    half = ROPE // 2
    inv_freq = (1.0 / (ROPE_THETA ** (np.arange(half, dtype=np.float32) / half))).astype(np.float32)
    ang = pos.astype(np.float32)[:, None] * inv_freq[None, :]
    return np.cos(ang).astype(np.float32), np.sin(ang).astype(np.float32)


def _swap_halves(w):
    half = w.shape[-1] // 2
    return jnp.concatenate([w[..., half:], w[..., :half]], axis=-1)


def _layer(x, meta_tokens, norm_g, w_in, q_norm_g, w_q_up, kv_norm_g, w_kv_up, conv_w,
           attn_out_g, conv_out_g, w_out, final_norm_g):
    B, S, D = x.shape
    assert D == D_MODEL and S % TILE == 0
    assert meta_tokens.shape == (N_META, D_MODEL)
    nblk = S // TILE
    assert nblk % 2 == 0, "per-tile buffers alternate a / b by block parity"

    sec = lambda lo, n: w_in[:, lo:lo + n].astype(BF16)
    w_in_p = jnp.concatenate([sec(0, 448), _swap_halves(sec(384, 64)), sec(448, 2560)], axis=1)
    wq = w_q_up.reshape(Q_LORA, HEADS, NOPE + ROPE)
    wq_t = jnp.concatenate([wq, _swap_halves(wq[..., NOPE:])], axis=-1).reshape(Q_LORA, HEADS * HEAD_PAD).T.astype(BF16)
    wkv = w_kv_up.reshape(KV_LORA, HEADS, NOPE + VDIM)
    w_k = wkv[..., :NOPE].reshape(KV_LORA, HEADS * NOPE).astype(BF16)
    w_vt = wkv[..., NOPE:].reshape(KV_LORA, HEADS * VDIM).T.astype(BF16)
    w_out_b = w_out.astype(BF16)
    gid = np.arange(CONV_W) // CONV_GROUP
    gmat = jnp.asarray((gid[:, None] == gid[None, :]).astype(np.float32), dtype=BF16)
    row = lambda v: v.reshape(1, -1).astype(F32)

    zpad = lambda n: np.zeros((n, LANES - ROPE), np.float32)
    c_m, s_m = _rope_angles(np.arange(N_META))
    cos_m = np.concatenate([c_m, c_m, zpad(N_META)], axis=-1)
    sin_m = np.concatenate([-s_m, s_m, zpad(N_META)], axis=-1)
    c_r, s_r = _rope_angles(N_META + np.arange(S))
    cos_r = np.concatenate([c_r, c_r, zpad(S)], axis=-1)
    sin_r = np.concatenate([-s_r, s_r, zpad(S)], axis=-1)
    cos_t = np.ascontiguousarray(np.concatenate([c_r, c_r], axis=-1).T)
    sin_t = np.ascontiguousarray(np.concatenate([-s_r, s_r], axis=-1).T)

    k_meta, vt_meta, g_meta = pl.pallas_call(
        _meta_kernel,
        out_shape=(jax.ShapeDtypeStruct((N_META, HEADS * HEAD_PAD), BF16),
                   jax.ShapeDtypeStruct((HEADS * VDIM, N_META), BF16),
                   jax.ShapeDtypeStruct((N_META, CONV_W), F32)),
        name="meta_proj",
    )(meta_tokens.astype(F32), cos_m, sin_m, row(norm_g), w_in_p, row(kv_norm_g), w_k, w_vt)

    ntiles = B * nblk
    t_in = lambda s: jnp.minimum(s, ntiles - 1)
    t_out = lambda s: jnp.maximum(s - 1, 0)
    once = lambda shape: pl.BlockSpec(shape, lambda s: (0,) * len(shape))
    hbm = pl.BlockSpec(memory_space=pl.ANY)
    out = pl.pallas_call(
        functools.partial(_layer_kernel, nblk),
        grid=(ntiles + 1,),
        in_specs=[pl.BlockSpec((1, TILE, D_MODEL), lambda s: (t_in(s) // nblk, t_in(s) % nblk, 0)),
                  pl.BlockSpec((1, TILE, D_MODEL), lambda s: (t_out(s) // nblk, t_out(s) % nblk, 0)),
                  pl.BlockSpec((TILE, LANES), lambda s: (t_in(s) % nblk, 0)),
                  pl.BlockSpec((TILE, LANES), lambda s: (t_in(s) % nblk, 0)),
                  pl.BlockSpec((ROPE, TILE), lambda s: (0, t_in(s) % nblk)),
                  pl.BlockSpec((ROPE, TILE), lambda s: (0, t_in(s) % nblk)),
                  once((N_META, CONV_W)), once((N_META, HEADS * HEAD_PAD)), once((HEADS * VDIM, N_META)),
                  once((1, D_MODEL)), hbm,
                  once((1, Q_LORA)), hbm,
                  once((1, KV_LORA)), once((KV_LORA, HEADS * NOPE)), once((HEADS * VDIM, KV_LORA)),
                  once((3, CONV_W)), once((1, CONV_W)), hbm,
                  hbm, once((1, HEADS * VDIM)), once((1, D_MODEL))],
        out_specs=pl.BlockSpec((1, TILE, D_MODEL), lambda s: (t_out(s) // nblk, t_out(s) % nblk, 0)),
        out_shape=jax.ShapeDtypeStruct((B, S, D_MODEL), F32),
        scratch_shapes=[pltpu.VMEM((D_MODEL, P_COLS), BF16),
                        pltpu.VMEM((HEADS * HEAD_PAD, Q_LORA), BF16),
                        pltpu.VMEM((CONV_W, CONV_W), BF16),
                        pltpu.VMEM((D_MODEL, D_MODEL), BF16),
                        pltpu.SemaphoreType.DMA((4,)),
                        pltpu.VMEM((TILE, P_COLS), F32),
                        pltpu.VMEM((TILE + 8, CONV_W), F32),
                        pltpu.VMEM((S, HEADS * HEAD_PAD), BF16),
                        pltpu.VMEM((nblk, HEADS * VDIM, TILE), BF16),
                        pltpu.VMEM((TILE, HEADS * HEAD_PAD), BF16),
                        pltpu.VMEM((HEADS * VDIM, TILE), BF16),
                        pltpu.VMEM((HEADS, HEAD_PAD, TILE), BF16),
                        pltpu.VMEM((HEADS, HEAD_PAD, TILE), BF16),
                        pltpu.VMEM((TILE, HEADS * VDIM), BF16),
                        pltpu.VMEM((TILE, HEADS * VDIM), BF16),
                        pltpu.VMEM((TILE, D_MODEL), BF16),
                        pltpu.VMEM((TILE, D_MODEL), BF16),
                        pltpu.VMEM((HEADS, 1, TILE), F32),
                        pltpu.VMEM((HEADS, VDIM + ONES_ROWS, TILE), F32),
                        pltpu.VMEM((2, TILE, TILE), BF16),
                        pltpu.VMEM((1, TILE), F32)],
        compiler_params=pltpu.CompilerParams(dimension_semantics=("arbitrary",),
                                             vmem_limit_bytes=VMEM_LIMIT_BYTES),
        name="layer",
    )(x, x, cos_r, sin_r, cos_t, sin_t, g_meta, k_meta, vt_meta, row(norm_g), w_in_p, row(q_norm_g), wq_t,
      row(kv_norm_g), w_k, w_vt, conv_w.astype(F32), row(conv_out_g), gmat, w_out_b, row(attn_out_g),
      row(final_norm_g))
    return out


def kernel(x, meta_tokens, norm_g, w_in, q_norm_g, w_q_up, kv_norm_g, w_kv_up, conv_w,
           attn_out_g, conv_out_g, w_out, final_norm_g):
    assert norm_g.shape[0] == 1, "single-layer block"
    return _layer(x, meta_tokens, norm_g[0], w_in[0], q_norm_g[0], w_q_up[0], kv_norm_g[0],
                  w_kv_up[0], conv_w[0], attn_out_g[0], conv_out_g[0], w_out[0], final_norm_g)
```

```python
import functools
import math

import jax
import jax.numpy as jnp
import numpy as np
from jax import lax
from jax.experimental import pallas as pl
from jax.experimental.pallas import tpu as pltpu

F32 = jnp.float32
BF16 = jnp.bfloat16

D_MODEL = 1024
N_META = 16
HEADS = 4
NOPE = 128
ROPE = 64
VDIM = 128
Q_LORA = 256
KV_LORA = 128
CONV_W = 512
CONV_GROUP = 64
ROPE_THETA = 10000.0
ATTN_SCALE = (NOPE + ROPE) ** -0.5
NEG_INF = -1e30
EPS = 1e-6

LANES = 128
HEAD_PAD = 2 * LANES
P_COLS = 3072
C_Q, C_KV, C_KROPE, C_Z_ATTN, C_CONV_B, C_CONV_C, C_CONV_H, C_Z_CONV = (
    0, 256, 384, 512, 1024, 1536, 2048, 2560)

TILE = 512
HALF = TILE // 2
QK_AHEAD = 2
ONES_ROWS = 16
REF_ROW = NOPE + ROPE
REF_ROWS = 16
OUT_ROWS = 256
RISE_LIMIT = 100.0
VMEM_LIMIT_BYTES = 56 * 2 ** 20
Q_SCALE = ATTN_SCALE * math.log2(math.e)


def _rms(x, g):
    ms = jnp.mean(x * x, axis=-1, keepdims=True)
    return x * lax.rsqrt(ms + EPS) * g


def _silu(x):
    hx = (0.5 * x).astype(BF16)
    return hx + hx * jnp.tanh(hx)


def _dot(a, b):
    return jnp.dot(a, b, preferred_element_type=F32)


def _dot_nt(a, b):
    return lax.dot_general(a, b, (((1,), (1,)), ((), ())), preferred_element_type=F32)


def _in_proj(x, norm_g, w_in):
    u = _rms(x, norm_g).astype(BF16)
    return _dot(u, w_in)


def _keys_values(p, cos, sin, kv_g, w_k, w_vt):
    c_kv = _rms(p[:, C_KV:C_KV + KV_LORA], kv_g).astype(BF16)
    k_nope = _dot(c_kv, w_k)
    v_t = _dot_nt(w_vt, c_kv)
    kr = p[:, C_KROPE:C_KROPE + LANES]
    k_pe = kr * cos + pltpu.roll(kr, ROPE, 1) * sin
    lane = lax.broadcasted_iota(jnp.int32, k_pe.shape, 1)
    k_pe = jnp.where(lane == REF_ROW - NOPE, 1.0, k_pe).astype(BF16)
    ks = []
    for h in range(HEADS):
        ks.append(k_nope[:, h * NOPE:(h + 1) * NOPE].astype(BF16))
        ks.append(k_pe)
    return ks, v_t.astype(BF16)


def _meta_kernel(x_ref, cos_ref, sin_ref, norm_g_ref, w_in_ref, kv_g_ref, w_k_ref, w_vt_ref,
                 k_out, vt_out, g_out):
    p = _in_proj(x_ref[...], norm_g_ref[...], w_in_ref[...])
    ks, v_t = _keys_values(p, cos_ref[...], sin_ref[...], kv_g_ref[...], w_k_ref[...], w_vt_ref[...])
    for i, kk in enumerate(ks):
        k_out[:, i * LANES:(i + 1) * LANES] = kk
    vt_out[...] = v_t
    g_out[...] = p[:, C_CONV_C:C_CONV_C + CONV_W] * p[:, C_CONV_H:C_CONV_H + CONV_W]


def _proj_kernel(x_ref, cos_ref, sin_ref, cos_t_ref, sin_t_ref, ginit_ref, norm_g_ref, w_in_ref,
                 q_g_ref, w_qt_ref, kv_g_ref, w_k_ref, w_vt_ref, conv_w_ref, conv_g_ref, gmat_ref,
                 qt_out, k_out, vt_out, ga_out, yc_out, gbuf):
    tm = x_ref.shape[1]

    @pl.when(pl.program_id(1) == 0)
    def _():
        gbuf[...] = ginit_ref[8:16, :]

    p = _in_proj(x_ref[0], norm_g_ref[...], w_in_ref[...])

    c_q = _rms(p[:, C_Q:C_Q + Q_LORA], q_g_ref[...] * Q_SCALE).astype(BF16)
    q_t = _dot_nt(w_qt_ref[...], c_q)
    cos_t = cos_t_ref[...]
    sin_t = sin_t_ref[...]
    for h in range(HEADS):
        r = h * HEAD_PAD
        qt_out[0, r:r + NOPE, :] = q_t[r:r + NOPE, :].astype(BF16)
        q_pe = q_t[r + NOPE:r + NOPE + ROPE, :] * cos_t + q_t[r + NOPE + ROPE:r + HEAD_PAD, :] * sin_t
        qt_out[0, r + NOPE:r + NOPE + ROPE, :] = q_pe.astype(BF16)
        qt_out[0, r + NOPE + ROPE:r + HEAD_PAD, :] = jnp.zeros((ROPE, tm), BF16)

    ks, v_t = _keys_values(p, cos_ref[...], sin_ref[...], kv_g_ref[...], w_k_ref[...], w_vt_ref[...])
    for i, kk in enumerate(ks):
        k_out[0, :, i * LANES:(i + 1) * LANES] = kk
    vt_out[0, 0] = v_t

    ga_out[0] = _silu(p[:, C_Z_ATTN:C_Z_ATTN + HEADS * VDIM]).astype(BF16)

    g = p[:, C_CONV_C:C_CONV_C + CONV_W] * p[:, C_CONV_H:C_CONV_H + CONV_W]
    carry = gbuf[...]
    first = lax.broadcasted_iota(jnp.int32, (8, CONV_W), 0) == 0

    def shift_down(v, row_before):
        r = pltpu.roll(v, 1, 0)
        return jnp.concatenate([jnp.where(first, row_before, r[0:8, :]), r[8:, :]], axis=0)

    g1 = shift_down(g, carry[7:8, :])
    g2 = shift_down(g1, carry[6:7, :])
    gbuf[...] = g[tm - 8:tm, :]
    cw = conv_w_ref[...]
    conv = cw[0:1, :] * g2 + cw[1:2, :] * g1 + cw[2:3, :] * g
    yc = p[:, C_CONV_B:C_CONV_B + CONV_W] * conv
    ssum = _dot((yc * yc).astype(BF16), gmat_ref[...])
    ycn = yc * lax.rsqrt(ssum * (1.0 / CONV_GROUP) + EPS) * conv_g_ref[...]
    yc_out[0] = (ycn * _silu(p[:, C_Z_CONV:C_Z_CONV + CONV_W])).astype(BF16)


def _ones_rows(v_t):
    return jnp.concatenate([v_t, jnp.ones((ONES_ROWS, v_t.shape[1]), BF16)], axis=0)


def _bf16_exact(x):
    return x.astype(BF16).astype(F32)


def _attention_tile(c, qt_ref, k_ref, vt_ref, km_ref, vtm_ref, q_aug, m_scr, acc_scr, p_scr, rise_scr):
    tri_full = (lax.broadcasted_iota(jnp.int32, (HALF, TILE), 0)
                <= lax.broadcasted_iota(jnp.int32, (HALF, TILE), 1))
    tri_half = tri_full[:, :HALF]

    def k_of(h, r0, r1):
        return k_ref[0, r0:r1, h * HEAD_PAD:(h + 1) * HEAD_PAD]

    def v_of(h, j, lo, hi):
        return vt_ref[0, j, h * VDIM:(h + 1) * VDIM, lo:hi]

    tasks = []
    for h in range(HEADS):
        tasks.append((h, lambda h=h: km_ref[:, h * HEAD_PAD:(h + 1) * HEAD_PAD],
                      lambda h=h: vtm_ref[h * VDIM:(h + 1) * VDIM, :], 0, TILE, None, True))
    for j in range(c):
        for h in range(HEADS):
            tasks.append((h, lambda h=h, j=j: k_of(h, j * TILE, (j + 1) * TILE),
                          lambda h=h, j=j: v_of(h, j, 0, TILE), 0, TILE, None, False))
    for h in range(HEADS):
        tasks.append((h, lambda h=h: k_of(h, c * TILE, c * TILE + HALF),
                      lambda h=h: v_of(h, c, 0, HALF), 0, TILE, tri_full, False))
    for h in range(HEADS):
        tasks.append((h, lambda h=h: k_of(h, c * TILE + HALF, (c + 1) * TILE),
                      lambda h=h: v_of(h, c, HALF, TILE), HALF, TILE, tri_half, False))
    assert QK_AHEAD < HEADS and all(t[0] == n % HEADS for n, t in enumerate(tasks))

    for h in range(HEADS):
        q_aug[h] = qt_ref[0, h * HEAD_PAD:(h + 1) * HEAD_PAD, :]
    rise_scr[...] = jnp.zeros(rise_scr.shape, F32)

    def scores(task):
        h, k_fn, _, lo, hi, mask, _ = task
        s = _dot(k_fn(), q_aug[h, :, lo:hi])
        return s if mask is None else jnp.where(mask, s, NEG_INF)

    def softmax(n, task, s):
        h, _, _, lo, hi, _, first = task
        nkeys, ncols = s.shape
        blk_max = jnp.max(s, axis=0, keepdims=True)
        if first:
            m_new, beta = _bf16_exact(blk_max), None
            p = jnp.exp2(s - m_new)
        else:
            rise = jnp.maximum(blk_max, 0.0)
            p = jnp.exp2(s)
            m_old = m_scr[h, :, lo:hi]
            m_new = _bf16_exact(m_old + rise)
            beta = jnp.exp2(m_old - m_new)
            rise_scr[:, lo:hi] = jnp.maximum(rise_scr[:, lo:hi], rise)
        p_scr[n % 2, 0:nkeys, 0:ncols] = p.astype(BF16)
        m_scr[h, :, lo:hi] = m_new
        q_aug[h, REF_ROW:REF_ROW + REF_ROWS, lo:hi] = jnp.broadcast_to(-m_new, (REF_ROWS, ncols)).astype(BF16)
        return nkeys, ncols, beta

    def values(n, task, nkeys, ncols, beta):
        h, _, v_fn, lo, hi, _, _ = task
        pv = _dot(_ones_rows(v_fn()), p_scr[n % 2, 0:nkeys, 0:ncols])
        if beta is None:
            acc_scr[h, :, lo:hi] = pv
        else:
            acc_scr[h, :, lo:hi] = (acc_scr[h, :, lo:hi] + pv) * beta

    pending = [scores(t) for t in tasks[:QK_AHEAD]]
    prev = None
    for n, task in enumerate(tasks):
        if prev is not None:
            values(*prev)
        if n + QK_AHEAD < len(tasks):
            pending.append(scores(tasks[n + QK_AHEAD]))
        prev = (n, task) + softmax(n, task, pending.pop(0))
    values(*prev)


def _attention_tile_two_pass(i, qt_ref, k_ref, vt_ref, km_ref, vtm_ref, m_scr, acc_scr):
    tri_full = (lax.broadcasted_iota(jnp.int32, (HALF, TILE), 0)
                <= lax.broadcasted_iota(jnp.int32, (HALF, TILE), 1))
    tri_half = tri_full[:, :HALF]

    def q_of(h, lo, hi):
        return qt_ref[0, h * HEAD_PAD:(h + 1) * HEAD_PAD, lo:hi]

    def update(h, s, v_t, lo, hi):
        m_old = m_scr[h, :, lo:hi]
        m_new = jnp.maximum(m_old, jnp.max(s, axis=0, keepdims=True))
        pv = _dot(_ones_rows(v_t), jnp.exp2(s - m_new).astype(BF16))
        acc_scr[h, :, lo:hi] = jnp.exp2(m_old - m_new) * acc_scr[h, :, lo:hi] + pv
        m_scr[h, :, lo:hi] = m_new

    for h in range(HEADS):
        s = _dot(km_ref[:, h * HEAD_PAD:(h + 1) * HEAD_PAD], q_of(h, 0, TILE))
        m = jnp.max(s, axis=0, keepdims=True)
        m_scr[h] = m
        acc_scr[h] = _dot(_ones_rows(vtm_ref[h * VDIM:(h + 1) * VDIM, :]), jnp.exp2(s - m).astype(BF16))

    def full_block(j, carry):
        start = pl.multiple_of(j * TILE, TILE)
        for h in range(HEADS):
            k = k_ref[0, pl.ds(start, TILE), h * HEAD_PAD:(h + 1) * HEAD_PAD]
            update(h, _dot(k, q_of(h, 0, TILE)), vt_ref[0, j, h * VDIM:(h + 1) * VDIM, :], 0, TILE)
        return carry

    lax.fori_loop(0, i, full_block, 0)

    d0 = pl.multiple_of(i * TILE, TILE)
    d1 = pl.multiple_of(i * TILE + HALF, HALF)
    for h in range(HEADS):
        k = k_ref[0, pl.ds(d0, HALF), h * HEAD_PAD:(h + 1) * HEAD_PAD]
        s = jnp.where(tri_full, _dot(k, q_of(h, 0, TILE)), NEG_INF)
        update(h, s, vt_ref[0, i, h * VDIM:(h + 1) * VDIM, 0:HALF], 0, TILE)
    for h in range(HEADS):
        k = k_ref[0, pl.ds(d1, HALF), h * HEAD_PAD:(h + 1) * HEAD_PAD]
        s = jnp.where(tri_half, _dot(k, q_of(h, HALF, TILE)), NEG_INF)
        update(h, s, vt_ref[0, i, h * VDIM:(h + 1) * VDIM, HALF:TILE], HALF, TILE)


def _finish_tile(acc_scr, ga_ref, yc_ref, x_ref, w_out_ref, attn_g_ref, final_g_ref, o_ref, y_scr):
    for h in range(HEADS):
        o_t = acc_scr[h, 0:VDIM, :] * (1.0 / acc_scr[h, VDIM:VDIM + 1, :])
        o = _rms(o_t.T, attn_g_ref[:, h * VDIM:(h + 1) * VDIM])
        y_scr[:, h * VDIM:(h + 1) * VDIM] = (o * ga_ref[0, :, h * VDIM:(h + 1) * VDIM].astype(F32)).astype(BF16)

    y_scr[:, HEADS * VDIM:] = yc_ref[0]
    for r0 in range(0, TILE, OUT_ROWS):
        mix = _dot(y_scr[r0:r0 + OUT_ROWS, :], w_out_ref[...])
        o_ref[0, r0:r0 + OUT_ROWS, :] = _rms(x_ref[0, r0:r0 + OUT_ROWS, :] + mix, final_g_ref[...])


def _attn_kernel(qt_ref, k_ref, vt_ref, km_ref, vtm_ref, ga_ref, yc_ref, x_ref, w_out_ref,
                 attn_g_ref, final_g_ref, o_ref, q_aug, m_scr, acc_scr, p_scr, rise_scr, y_scr):
    nblk = k_ref.shape[1] // TILE
    finish = functools.partial(_finish_tile, acc_scr, ga_ref, yc_ref, x_ref, w_out_ref, attn_g_ref,
                               final_g_ref, o_ref, y_scr)
    for c in range(nblk):
        @pl.when(pl.program_id(1) == c)
        def _(c=c):
            _attention_tile(c, qt_ref, k_ref, vt_ref, km_ref, vtm_ref, q_aug, m_scr, acc_scr, p_scr, rise_scr)
            finish()

    @pl.when(jnp.max(rise_scr[...]) > RISE_LIMIT)
    def _():
        _attention_tile_two_pass(pl.program_id(1), qt_ref, k_ref, vt_ref, km_ref, vtm_ref, m_scr, acc_scr)
        finish()


def _rope_angles(pos):
    half = ROPE // 2
    inv_freq = (1.0 / (ROPE_THETA ** (np.arange(half, dtype=np.float32) / half))).astype(np.float32)
    ang = pos.astype(np.float32)[:, None] * inv_freq[None, :]
    return np.cos(ang).astype(np.float32), np.sin(ang).astype(np.float32)


def _swap_halves(w):
    half = w.shape[-1] // 2
    return jnp.concatenate([w[..., half:], w[..., :half]], axis=-1)


def _full(shape):
    return pl.BlockSpec(shape, lambda *_: (0,) * len(shape))


def _layer(x, meta_tokens, norm_g, w_in, q_norm_g, w_q_up, kv_norm_g, w_kv_up, conv_w,
           attn_out_g, conv_out_g, w_out, final_norm_g):
    B, S, D = x.shape
    assert D == D_MODEL and S % TILE == 0
    assert meta_tokens.shape == (N_META, D_MODEL)
    nblk = S // TILE

    sec = lambda lo, n: w_in[:, lo:lo + n].astype(BF16)
    w_in_p = jnp.concatenate([sec(0, 448), _swap_halves(sec(384, 64)), sec(448, 2560)], axis=1)
    wq = w_q_up.reshape(Q_LORA, HEADS, NOPE + ROPE)
    wq_t = jnp.concatenate([wq, _swap_halves(wq[..., NOPE:])], axis=-1).reshape(Q_LORA, HEADS * HEAD_PAD).T.astype(BF16)
    wkv = w_kv_up.reshape(KV_LORA, HEADS, NOPE + VDIM)
    w_k = wkv[..., :NOPE].reshape(KV_LORA, HEADS * NOPE).astype(BF16)
    w_vt = wkv[..., NOPE:].reshape(KV_LORA, HEADS * VDIM).T.astype(BF16)
    w_out_b = w_out.astype(BF16)
    gid = np.arange(CONV_W) // CONV_GROUP
    gmat = jnp.asarray((gid[:, None] == gid[None, :]).astype(np.float32), dtype=BF16)
    row = lambda v: v.reshape(1, -1).astype(F32)

    zpad = lambda n: np.zeros((n, LANES - ROPE), np.float32)
    c_m, s_m = _rope_angles(np.arange(N_META))
    cos_m = np.concatenate([c_m, c_m, zpad(N_META)], axis=-1)
    sin_m = np.concatenate([-s_m, s_m, zpad(N_META)], axis=-1)
    c_r, s_r = _rope_angles(N_META + np.arange(S))
    cos_r = np.concatenate([c_r, c_r, zpad(S)], axis=-1)
    sin_r = np.concatenate([-s_r, s_r, zpad(S)], axis=-1)
    cos_t = np.ascontiguousarray(np.concatenate([c_r, c_r], axis=-1).T)
    sin_t = np.ascontiguousarray(np.concatenate([-s_r, s_r], axis=-1).T)

    k_meta, vt_meta, g_meta = pl.pallas_call(
        _meta_kernel,
        out_shape=(jax.ShapeDtypeStruct((N_META, HEADS * HEAD_PAD), BF16),
                   jax.ShapeDtypeStruct((HEADS * VDIM, N_META), BF16),
                   jax.ShapeDtypeStruct((N_META, CONV_W), F32)),
        name="meta_proj",
    )(meta_tokens.astype(F32), cos_m, sin_m, row(norm_g), w_in_p, row(kv_norm_g), w_k, w_vt)

    tok = lambda w: pl.BlockSpec((1, TILE, w), lambda b, t: (b, t, 0))
    qt_s, k_s, vt_s, ga_s, yc_s = pl.pallas_call(
        _proj_kernel,
        grid=(B, nblk),
        in_specs=[tok(D_MODEL),
                  pl.BlockSpec((TILE, LANES), lambda b, t: (t, 0)),
                  pl.BlockSpec((TILE, LANES), lambda b, t: (t, 0)),
                  pl.BlockSpec((ROPE, TILE), lambda b, t: (0, t)),
                  pl.BlockSpec((ROPE, TILE), lambda b, t: (0, t)),
                  _full((N_META, CONV_W)), _full((1, D_MODEL)), _full((D_MODEL, P_COLS)),
                  _full((1, Q_LORA)), _full((HEADS * HEAD_PAD, Q_LORA)),
                  _full((1, KV_LORA)), _full((KV_LORA, HEADS * NOPE)), _full((HEADS * VDIM, KV_LORA)),
                  _full((3, CONV_W)), _full((1, CONV_W)), _full((CONV_W, CONV_W))],
        out_specs=[pl.BlockSpec((1, HEADS * HEAD_PAD, TILE), lambda b, t: (b, 0, t)),
                   tok(HEADS * HEAD_PAD),
                   pl.BlockSpec((1, 1, HEADS * VDIM, TILE), lambda b, t: (b, t, 0, 0)),
                   tok(HEADS * VDIM), tok(CONV_W)],
        out_shape=[jax.ShapeDtypeStruct((B, HEADS * HEAD_PAD, S), BF16),
                   jax.ShapeDtypeStruct((B, S, HEADS * HEAD_PAD), BF16),
                   jax.ShapeDtypeStruct((B, nblk, HEADS * VDIM, TILE), BF16),
                   jax.ShapeDtypeStruct((B, S, HEADS * VDIM), BF16),
                   jax.ShapeDtypeStruct((B, S, CONV_W), BF16)],
        scratch_shapes=[pltpu.VMEM((8, CONV_W), F32)],
        compiler_params=pltpu.CompilerParams(dimension_semantics=("arbitrary", "arbitrary"),
                                             vmem_limit_bytes=VMEM_LIMIT_BYTES),
        name="proj",
    )(x, cos_r, sin_r, cos_t, sin_t, g_meta, row(norm_g), w_in_p, row(q_norm_g), wq_t,
      row(kv_norm_g), w_k, w_vt, conv_w.astype(F32), row(conv_out_g), gmat)

    qtile = lambda w: pl.BlockSpec((1, TILE, w), lambda b, i: (b, i, 0))
    out = pl.pallas_call(
        _attn_kernel,
        grid=(B, nblk),
        in_specs=[pl.BlockSpec((1, HEADS * HEAD_PAD, TILE), lambda b, i: (b, 0, i)),
                  pl.BlockSpec((1, S, HEADS * HEAD_PAD), lambda b, i: (b, 0, 0)),
                  pl.BlockSpec((1, nblk, HEADS * VDIM, TILE), lambda b, i: (b, 0, 0, 0)),
                  _full((N_META, HEADS * HEAD_PAD)), _full((HEADS * VDIM, N_META)),
                  qtile(HEADS * VDIM), qtile(CONV_W), qtile(D_MODEL),
                  _full((D_MODEL, D_MODEL)), _full((1, HEADS * VDIM)), _full((1, D_MODEL))],
        out_specs=qtile(D_MODEL),
        out_shape=jax.ShapeDtypeStruct((B, S, D_MODEL), F32),
        scratch_shapes=[pltpu.VMEM((HEADS, HEAD_PAD, TILE), BF16),
                        pltpu.VMEM((HEADS, 1, TILE), F32),
                        pltpu.VMEM((HEADS, VDIM + ONES_ROWS, TILE), F32),
                        pltpu.VMEM((2, TILE, TILE), BF16),
                        pltpu.VMEM((1, TILE), F32),
                        pltpu.VMEM((TILE, D_MODEL), BF16)],
        compiler_params=pltpu.CompilerParams(dimension_semantics=("arbitrary", "arbitrary"),
                                             vmem_limit_bytes=VMEM_LIMIT_BYTES),
        name="attn_out",
    )(qt_s, k_s, vt_s, k_meta, vt_meta, ga_s, yc_s, x, w_out_b, row(attn_out_g), row(final_norm_g))
    return out


def kernel(x, meta_tokens, norm_g, w_in, q_norm_g, w_q_up, kv_norm_g, w_kv_up, conv_w,
           attn_out_g, conv_out_g, w_out, final_norm_g):
    assert norm_g.shape[0] == 1, "single-layer block"
    return _layer(x, meta_tokens, norm_g[0], w_in[0], q_norm_g[0], w_q_up[0], kv_norm_g[0],
                  w_kv_up[0], conv_w[0], attn_out_g[0], conv_out_g[0], w_out[0], final_norm_g)
```

```python
import functools
import math

import jax
import jax.numpy as jnp
import numpy as np
from jax import lax
from jax.experimental import pallas as pl
from jax.experimental.pallas import tpu as pltpu

F32 = jnp.float32
BF16 = jnp.bfloat16

D_MODEL = 1024
N_META = 16
HEADS = 4
NOPE = 128
ROPE = 64
VDIM = 128
Q_LORA = 256
KV_LORA = 128
CONV_W = 512
CONV_GROUP = 64
ROPE_THETA = 10000.0
ATTN_SCALE = (NOPE + ROPE) ** -0.5
NEG_INF = -1e30
EPS = 1e-6

LANES = 128
HEAD_PAD = 2 * LANES
HEAD_COLS, TAIL_COLS = 512, 2560
C_Q, C_KV, C_KROPE = 0, 256, 384
C_Z_ATTN, C_CONV_B, C_CONV_C, C_CONV_H, C_Z_CONV = 0, 512, 1024, 1536, 2048

TILE = 512
HALF = TILE // 2
QK_AHEAD = 2
ONES_ROWS = 16
REF_ROW = NOPE + ROPE
REF_ROWS = 16
OUT_ROWS = 256
RISE_LIMIT = 100.0
VMEM_LIMIT_BYTES = 56 * 2 ** 20
Q_SCALE = ATTN_SCALE * math.log2(math.e)


def _rms(x, g):
    ms = jnp.mean(x * x, axis=-1, keepdims=True)
    return x * lax.rsqrt(ms + EPS) * g


def _silu(x):
    hx = (0.5 * x).astype(BF16)
    return hx + hx * jnp.tanh(hx)


def _dot(a, b):
    return jnp.dot(a, b, preferred_element_type=F32)


def _dot_nt(a, b):
    return lax.dot_general(a, b, (((1,), (1,)), ((), ())), preferred_element_type=F32)


def _in_proj(x, norm_g, w_head, w_tail):
    u = _rms(x, norm_g).astype(BF16)
    return _dot(u, w_head), _dot(u, w_tail)


def _keys_values(p, cos, sin, kv_g, w_k, w_vt):
    c_kv = _rms(p[:, C_KV:C_KV + KV_LORA], kv_g).astype(BF16)
    k_nope = _dot(c_kv, w_k)
    v_t = _dot_nt(w_vt, c_kv)
    kr = p[:, C_KROPE:C_KROPE + LANES]
    k_pe = kr * cos + pltpu.roll(kr, ROPE, 1) * sin
    lane = lax.broadcasted_iota(jnp.int32, k_pe.shape, 1)
    k_pe = jnp.where(lane == REF_ROW - NOPE, 1.0, k_pe).astype(BF16)
    ks = []
    for h in range(HEADS):
        ks.append(k_nope[:, h * NOPE:(h + 1) * NOPE].astype(BF16))
        ks.append(k_pe)
    return ks, v_t.astype(BF16)


def _meta_kernel(x_ref, cos_ref, sin_ref, norm_g_ref, w_head_ref, w_tail_ref, kv_g_ref, w_k_ref, w_vt_ref,
                 k_out, vt_out, g_out):
    ph, p = _in_proj(x_ref[...], norm_g_ref[...], w_head_ref[...], w_tail_ref[...])
    ks, v_t = _keys_values(ph, cos_ref[...], sin_ref[...], kv_g_ref[...], w_k_ref[...], w_vt_ref[...])
    for i, kk in enumerate(ks):
        k_out[:, i * LANES:(i + 1) * LANES] = kk
    vt_out[...] = v_t
    g_out[...] = p[:, C_CONV_C:C_CONV_C + CONV_W] * p[:, C_CONV_H:C_CONV_H + CONV_W]


def _proj_kernel(x_ref, cos_ref, sin_ref, cos_t_ref, sin_t_ref, ginit_ref, norm_g_ref, w_head_ref, w_tail_ref,
                 q_g_ref, w_qt_ref, kv_g_ref, w_k_ref, w_vt_ref, conv_w_ref, conv_g_ref, gmat_ref,
                 qt_out, k_out, vt_out, ga_out, yc_out, gbuf):
    tm = x_ref.shape[1]

    @pl.when(pl.program_id(1) == 0)
    def _():
        gbuf[...] = ginit_ref[8:16, :]

    ph, p = _in_proj(x_ref[0], norm_g_ref[...], w_head_ref[...], w_tail_ref[...])

    c_q = _rms(ph[:, C_Q:C_Q + Q_LORA], q_g_ref[...] * Q_SCALE).astype(BF16)
    q_t = _dot_nt(w_qt_ref[...], c_q)
    cos_t = cos_t_ref[...]
    sin_t = sin_t_ref[...]
    for h in range(HEADS):
        r = h * HEAD_PAD
        qt_out[0, r:r + NOPE, :] = q_t[r:r + NOPE, :].astype(BF16)
        q_pe = q_t[r + NOPE:r + NOPE + ROPE, :] * cos_t + q_t[r + NOPE + ROPE:r + HEAD_PAD, :] * sin_t
        qt_out[0, r + NOPE:r + NOPE + ROPE, :] = q_pe.astype(BF16)
        qt_out[0, r + NOPE + ROPE:r + HEAD_PAD, :] = jnp.zeros((ROPE, tm), BF16)

    ks, v_t = _keys_values(ph, cos_ref[...], sin_ref[...], kv_g_ref[...], w_k_ref[...], w_vt_ref[...])
    for i, kk in enumerate(ks):
        k_out[0, :, i * LANES:(i + 1) * LANES] = kk
    vt_out[0, 0] = v_t

    ga_out[0] = _silu(p[:, C_Z_ATTN:C_Z_ATTN + HEADS * VDIM]).astype(BF16)

    g = p[:, C_CONV_C:C_CONV_C + CONV_W] * p[:, C_CONV_H:C_CONV_H + CONV_W]
    carry = gbuf[...]
    first = lax.broadcasted_iota(jnp.int32, (8, CONV_W), 0) == 0

    def shift_down(v, row_before):
        r = pltpu.roll(v, 1, 0)
        return jnp.concatenate([jnp.where(first, row_before, r[0:8, :]), r[8:, :]], axis=0)

    g1 = shift_down(g, carry[7:8, :])
    g2 = shift_down(g1, carry[6:7, :])
    gbuf[...] = g[tm - 8:tm, :]
    cw = conv_w_ref[...]
    conv = cw[0:1, :] * g2 + cw[1:2, :] * g1 + cw[2:3, :] * g
    yc = p[:, C_CONV_B:C_CONV_B + CONV_W] * conv
    ssum = _dot((yc * yc).astype(BF16), gmat_ref[...])
    ycn = yc * lax.rsqrt(ssum * (1.0 / CONV_GROUP) + EPS) * conv_g_ref[...]
    yc_out[0] = (ycn * _silu(p[:, C_Z_CONV:C_Z_CONV + CONV_W])).astype(BF16)


def _ones_rows(v_t):
    return jnp.concatenate([v_t, jnp.ones((ONES_ROWS, v_t.shape[1]), BF16)], axis=0)


def _bf16_exact(x):
    return x.astype(BF16).astype(F32)


def _attention_tile(c, qt_ref, k_ref, vt_ref, km_ref, vtm_ref, q_aug, m_scr, acc_scr, p_scr, rise_scr):
    tri_full = (lax.broadcasted_iota(jnp.int32, (HALF, TILE), 0)
                <= lax.broadcasted_iota(jnp.int32, (HALF, TILE), 1))
    tri_half = tri_full[:, :HALF]

    def k_of(h, r0, r1):
        return k_ref[0, r0:r1, h * HEAD_PAD:(h + 1) * HEAD_PAD]

    def v_of(h, j, lo, hi):
        return vt_ref[0, j, h * VDIM:(h + 1) * VDIM, lo:hi]

    tasks = []
    for h in range(HEADS):
        tasks.append((h, lambda h=h: km_ref[:, h * HEAD_PAD:(h + 1) * HEAD_PAD],
                      lambda h=h: vtm_ref[h * VDIM:(h + 1) * VDIM, :], 0, TILE, None, True))
    for j in range(c):
        for h in range(HEADS):
            tasks.append((h, lambda h=h, j=j: k_of(h, j * TILE, (j + 1) * TILE),
                          lambda h=h, j=j: v_of(h, j, 0, TILE), 0, TILE, None, False))
    for h in range(HEADS):
        tasks.append((h, lambda h=h: k_of(h, c * TILE, c * TILE + HALF),
                      lambda h=h: v_of(h, c, 0, HALF), 0, TILE, tri_full, False))
    for h in range(HEADS):
        tasks.append((h, lambda h=h: k_of(h, c * TILE + HALF, (c + 1) * TILE),
                      lambda h=h: v_of(h, c, HALF, TILE), HALF, TILE, tri_half, False))
    assert QK_AHEAD <= 2 and all(t[0] == n % HEADS for n, t in enumerate(tasks))

    for h in range(HEADS):
        q_aug[h] = qt_ref[0, h * HEAD_PAD:(h + 1) * HEAD_PAD, :]
    rise_scr[...] = jnp.zeros(rise_scr.shape, F32)

    def scores(task):
        h, k_fn, _, lo, hi, mask, _ = task
        s = _dot(k_fn(), q_aug[h, :, lo:hi])
        return s if mask is None else jnp.where(mask, s, NEG_INF)

    def softmax(n, task, s):
        h, _, _, lo, hi, _, first = task
        nkeys, ncols = s.shape
        blk_max = jnp.max(s, axis=0, keepdims=True)
        if first:
            m_new, beta = _bf16_exact(blk_max), None
            p = jnp.exp2(s - m_new)
        else:
            rise = jnp.maximum(blk_max, 0.0)
            p = jnp.exp2(s)
            m_old = m_scr[h, :, lo:hi]
            m_new = _bf16_exact(m_old + rise)
            beta = jnp.exp2(m_old - m_new)
            rise_scr[:, lo:hi] = jnp.maximum(rise_scr[:, lo:hi], rise)
        p_scr[n % 2, 0:nkeys, 0:ncols] = p.astype(BF16)
        m_scr[h, :, lo:hi] = m_new
        q_aug[h, REF_ROW:REF_ROW + REF_ROWS, lo:hi] = jnp.broadcast_to(-m_new, (REF_ROWS, ncols)).astype(BF16)
        return nkeys, ncols, beta

    def values(n, task, nkeys, ncols, beta):
        h, _, v_fn, lo, hi, _, _ = task
        pv = _dot(_ones_rows(v_fn()), p_scr[n % 2, 0:nkeys, 0:ncols])
        if beta is None:
            acc_scr[h, :, lo:hi] = pv
        else:
            acc_scr[h, :, lo:hi] = (acc_scr[h, :, lo:hi] + pv) * beta

    pending = [scores(t) for t in tasks[:QK_AHEAD]]
    prev = None
    for n, task in enumerate(tasks):
        if prev is not None:
            values(*prev)
        if n + QK_AHEAD < len(tasks):
            pending.append(scores(tasks[n + QK_AHEAD]))
        prev = (n, task) + softmax(n, task, pending.pop(0))
    values(*prev)


def _attention_tile_two_pass(i, qt_ref, k_ref, vt_ref, km_ref, vtm_ref, m_scr, acc_scr):
    tri_full = (lax.broadcasted_iota(jnp.int32, (HALF, TILE), 0)
                <= lax.broadcasted_iota(jnp.int32, (HALF, TILE), 1))
    tri_half = tri_full[:, :HALF]

    def q_of(h, lo, hi):
        return qt_ref[0, h * HEAD_PAD:(h + 1) * HEAD_PAD, lo:hi]

    def update(h, s, v_t, lo, hi):
        m_old = m_scr[h, :, lo:hi]
        m_new = jnp.maximum(m_old, jnp.max(s, axis=0, keepdims=True))
        pv = _dot(_ones_rows(v_t), jnp.exp2(s - m_new).astype(BF16))
        acc_scr[h, :, lo:hi] = jnp.exp2(m_old - m_new) * acc_scr[h, :, lo:hi] + pv
        m_scr[h, :, lo:hi] = m_new

    for h in range(HEADS):
        s = _dot(km_ref[:, h * HEAD_PAD:(h + 1) * HEAD_PAD], q_of(h, 0, TILE))
        m = jnp.max(s, axis=0, keepdims=True)
        m_scr[h] = m
        acc_scr[h] = _dot(_ones_rows(vtm_ref[h * VDIM:(h + 1) * VDIM, :]), jnp.exp2(s - m).astype(BF16))

    def full_block(j, carry):
        start = pl.multiple_of(j * TILE, TILE)
        for h in range(HEADS):
            k = k_ref[0, pl.ds(start, TILE), h * HEAD_PAD:(h + 1) * HEAD_PAD]
            update(h, _dot(k, q_of(h, 0, TILE)), vt_ref[0, j, h * VDIM:(h + 1) * VDIM, :], 0, TILE)
        return carry

    lax.fori_loop(0, i, full_block, 0)

    d0 = pl.multiple_of(i * TILE, TILE)
    d1 = pl.multiple_of(i * TILE + HALF, HALF)
    for h in range(HEADS):
        k = k_ref[0, pl.ds(d0, HALF), h * HEAD_PAD:(h + 1) * HEAD_PAD]
        s = jnp.where(tri_full, _dot(k, q_of(h, 0, TILE)), NEG_INF)
        update(h, s, vt_ref[0, i, h * VDIM:(h + 1) * VDIM, 0:HALF], 0, TILE)
    for h in range(HEADS):
        k = k_ref[0, pl.ds(d1, HALF), h * HEAD_PAD:(h + 1) * HEAD_PAD]
        s = jnp.where(tri_half, _dot(k, q_of(h, HALF, TILE)), NEG_INF)
        update(h, s, vt_ref[0, i, h * VDIM:(h + 1) * VDIM, HALF:TILE], HALF, TILE)


def _finish_tile(acc_scr, ga_ref, yc_ref, x_ref, w_out_ref, attn_g_ref, final_g_ref, o_ref, y_scr):
    for h in range(HEADS):
        o_t = acc_scr[h, 0:VDIM, :] * (1.0 / acc_scr[h, VDIM:VDIM + 1, :])
        o = _rms(o_t.T, attn_g_ref[:, h * VDIM:(h + 1) * VDIM])
        y_scr[:, h * VDIM:(h + 1) * VDIM] = (o * ga_ref[0, :, h * VDIM:(h + 1) * VDIM].astype(F32)).astype(BF16)

    y_scr[:, HEADS * VDIM:] = yc_ref[0]
    for r0 in range(0, TILE, OUT_ROWS):
        mix = _dot(y_scr[r0:r0 + OUT_ROWS, :], w_out_ref[...])
        o_ref[0, r0:r0 + OUT_ROWS, :] = _rms(x_ref[0, r0:r0 + OUT_ROWS, :] + mix, final_g_ref[...])


def _attn_kernel(qt_ref, k_ref, vt_ref, km_ref, vtm_ref, ga_ref, yc_ref, x_ref, w_out_ref,
                 attn_g_ref, final_g_ref, o_ref, q_aug, m_scr, acc_scr, p_scr, rise_scr, y_scr):
    nblk = k_ref.shape[1] // TILE
    finish = functools.partial(_finish_tile, acc_scr, ga_ref, yc_ref, x_ref, w_out_ref, attn_g_ref,
                               final_g_ref, o_ref, y_scr)
    for c in range(nblk):
        @pl.when(pl.program_id(1) == c)
        def _(c=c):
            _attention_tile(c, qt_ref, k_ref, vt_ref, km_ref, vtm_ref, q_aug, m_scr, acc_scr, p_scr, rise_scr)
            finish()

    @pl.when(jnp.max(rise_scr[...]) > RISE_LIMIT)
    def _():
        _attention_tile_two_pass(pl.program_id(1), qt_ref, k_ref, vt_ref, km_ref, vtm_ref, m_scr, acc_scr)
        finish()


def _rope_angles(pos):
    half = ROPE // 2
    inv_freq = (1.0 / (ROPE_THETA ** (np.arange(half, dtype=np.float32) / half))).astype(np.float32)
    ang = pos.astype(np.float32)[:, None] * inv_freq[None, :]
    return np.cos(ang).astype(np.float32), np.sin(ang).astype(np.float32)


def _swap_halves(w):
    half = w.shape[-1] // 2
    return jnp.concatenate([w[..., half:], w[..., :half]], axis=-1)


def _full(shape):
    return pl.BlockSpec(shape, lambda *_: (0,) * len(shape))


def _layer(x, meta_tokens, norm_g, w_in, q_norm_g, w_q_up, kv_norm_g, w_kv_up, conv_w,
           attn_out_g, conv_out_g, w_out, final_norm_g):
    B, S, D = x.shape
    assert D == D_MODEL and S % TILE == 0
    assert meta_tokens.shape == (N_META, D_MODEL)
    nblk = S // TILE

    w_head = jnp.concatenate([w_in[:, :448], _swap_halves(w_in[:, 384:448])], axis=1).astype(BF16)
    w_tail = w_in[:, 448:].astype(BF16)
    wq = w_q_up.reshape(Q_LORA, HEADS, NOPE + ROPE)
    wq_t = jnp.concatenate([wq, _swap_halves(wq[..., NOPE:])], axis=-1).reshape(Q_LORA, HEADS * HEAD_PAD).T.astype(BF16)
    wkv = w_kv_up.reshape(KV_LORA, HEADS, NOPE + VDIM)
    w_k = wkv[..., :NOPE].reshape(KV_LORA, HEADS * NOPE).astype(BF16)
    w_vt = wkv[..., NOPE:].reshape(KV_LORA, HEADS * VDIM).T.astype(BF16)
    w_out_b = w_out.astype(BF16)
    gid = np.arange(CONV_W) // CONV_GROUP
    gmat = jnp.asarray((gid[:, None] == gid[None, :]).astype(np.float32), dtype=BF16)
    row = lambda v: v.reshape(1, -1).astype(F32)

    zpad = lambda n: np.zeros((n, LANES - ROPE), np.float32)
    c_m, s_m = _rope_angles(np.arange(N_META))
    cos_m = np.concatenate([c_m, c_m, zpad(N_META)], axis=-1)
    sin_m = np.concatenate([-s_m, s_m, zpad(N_META)], axis=-1)
    c_r, s_r = _rope_angles(N_META + np.arange(S))
    cos_r = np.concatenate([c_r, c_r, zpad(S)], axis=-1)
    sin_r = np.concatenate([-s_r, s_r, zpad(S)], axis=-1)
    cos_t = np.ascontiguousarray(np.concatenate([c_r, c_r], axis=-1).T)
    sin_t = np.ascontiguousarray(np.concatenate([-s_r, s_r], axis=-1).T)

    k_meta, vt_meta, g_meta = pl.pallas_call(
        _meta_kernel,
        out_shape=(jax.ShapeDtypeStruct((N_META, HEADS * HEAD_PAD), BF16),
                   jax.ShapeDtypeStruct((HEADS * VDIM, N_META), BF16),
                   jax.ShapeDtypeStruct((N_META, CONV_W), F32)),
        name="meta_proj",
    )(meta_tokens.astype(F32), cos_m, sin_m, row(norm_g), w_head, w_tail, row(kv_norm_g), w_k, w_vt)

    tok = lambda w: pl.BlockSpec((1, TILE, w), lambda b, t: (b, t, 0))
    qt_s, k_s, vt_s, ga_s, yc_s = pl.pallas_call(
        _proj_kernel,
        grid=(B, nblk),
        in_specs=[tok(D_MODEL),
                  pl.BlockSpec((TILE, LANES), lambda b, t: (t, 0)),
                  pl.BlockSpec((TILE, LANES), lambda b, t: (t, 0)),
                  pl.BlockSpec((ROPE, TILE), lambda b, t: (0, t)),
                  pl.BlockSpec((ROPE, TILE), lambda b, t: (0, t)),
                  _full((N_META, CONV_W)), _full((1, D_MODEL)),
                  _full((D_MODEL, HEAD_COLS)), _full((D_MODEL, TAIL_COLS)),
                  _full((1, Q_LORA)), _full((HEADS * HEAD_PAD, Q_LORA)),
                  _full((1, KV_LORA)), _full((KV_LORA, HEADS * NOPE)), _full((HEADS * VDIM, KV_LORA)),
                  _full((3, CONV_W)), _full((1, CONV_W)), _full((CONV_W, CONV_W))],
        out_specs=[pl.BlockSpec((1, HEADS * HEAD_PAD, TILE), lambda b, t: (b, 0, t)),
                   tok(HEADS * HEAD_PAD),
                   pl.BlockSpec((1, 1, HEADS * VDIM, TILE), lambda b, t: (b, t, 0, 0)),
                   tok(HEADS * VDIM), tok(CONV_W)],
        out_shape=[jax.ShapeDtypeStruct((B, HEADS * HEAD_PAD, S), BF16),
                   jax.ShapeDtypeStruct((B, S, HEADS * HEAD_PAD), BF16),
                   jax.ShapeDtypeStruct((B, nblk, HEADS * VDIM, TILE), BF16),
                   jax.ShapeDtypeStruct((B, S, HEADS * VDIM), BF16),
                   jax.ShapeDtypeStruct((B, S, CONV_W), BF16)],
        scratch_shapes=[pltpu.VMEM((8, CONV_W), F32)],
        compiler_params=pltpu.CompilerParams(dimension_semantics=("arbitrary", "arbitrary"),
                                             vmem_limit_bytes=VMEM_LIMIT_BYTES),
        name="proj",
    )(x, cos_r, sin_r, cos_t, sin_t, g_meta, row(norm_g), w_head, w_tail, row(q_norm_g), wq_t,
      row(kv_norm_g), w_k, w_vt, conv_w.astype(F32), row(conv_out_g), gmat)

    qtile = lambda w: pl.BlockSpec((1, TILE, w), lambda b, i: (b, i, 0))
    out = pl.pallas_call(
        _attn_kernel,
        grid=(B, nblk),
        in_specs=[pl.BlockSpec((1, HEADS * HEAD_PAD, TILE), lambda b, i: (b, 0, i)),
                  pl.BlockSpec((1, S, HEADS * HEAD_PAD), lambda b, i: (b, 0, 0)),
                  pl.BlockSpec((1, nblk, HEADS * VDIM, TILE), lambda b, i: (b, 0, 0, 0)),
                  _full((N_META, HEADS * HEAD_PAD)), _full((HEADS * VDIM, N_META)),
                  qtile(HEADS * VDIM), qtile(CONV_W), qtile(D_MODEL),
                  _full((D_MODEL, D_MODEL)), _full((1, HEADS * VDIM)), _full((1, D_MODEL))],
        out_specs=qtile(D_MODEL),
        out_shape=jax.ShapeDtypeStruct((B, S, D_MODEL), F32),
        scratch_shapes=[pltpu.VMEM((HEADS, HEAD_PAD, TILE), BF16),
                        pltpu.VMEM((HEADS, 1, TILE), F32),
                        pltpu.VMEM((HEADS, VDIM + ONES_ROWS, TILE), F32),
                        pltpu.VMEM((2, TILE, TILE), BF16),
                        pltpu.VMEM((1, TILE), F32),
                        pltpu.VMEM((TILE, D_MODEL), BF16)],
        compiler_params=pltpu.CompilerParams(dimension_semantics=("arbitrary", "arbitrary"),
                                             vmem_limit_bytes=VMEM_LIMIT_BYTES),
        name="attn_out",
    )(qt_s, k_s, vt_s, k_meta, vt_meta, ga_s, yc_s, x, w_out_b, row(attn_out_g), row(final_norm_g))
    return out


def kernel(x, meta_tokens, norm_g, w_in, q_norm_g, w_q_up, kv_norm_g, w_kv_up, conv_w,
           attn_out_g, conv_out_g, w_out, final_norm_g):
    assert norm_g.shape[0] == 1, "single-layer block"
    return _layer(x, meta_tokens, norm_g[0], w_in[0], q_norm_g[0], w_q_up[0], kv_norm_g[0],
                  w_kv_up[0], conv_w[0], attn_out_g[0], conv_out_g[0], w_out[0], final_norm_g)
```

```python
import functools
import math

import jax
import jax.numpy as jnp
import numpy as np
from jax import lax
from jax.experimental import pallas as pl
from jax.experimental.pallas import tpu as pltpu

F32 = jnp.float32
BF16 = jnp.bfloat16

D_MODEL = 1024
N_META = 16
HEADS = 4
NOPE = 128
ROPE = 64
VDIM = 128
Q_LORA = 256
KV_LORA = 128
CONV_W = 512
CONV_GROUP = 64
ROPE_THETA = 10000.0
ATTN_SCALE = (NOPE + ROPE) ** -0.5
NEG_INF = -1e30
EPS = 1e-6

LANES = 128
HEAD_PAD = 2 * LANES
HEAD_COLS, TAIL_COLS = 512, 2560
C_Q, C_KV, C_KROPE = 0, 256, 384
C_Z_ATTN, C_CONV_B, C_CONV_C, C_CONV_H, C_Z_CONV = 0, 512, 1024, 1536, 2048

TILE = 512
HALF = TILE // 2
QK_AHEAD = 2
ONES_ROWS = 16
REF_ROW = NOPE + ROPE
REF_ROWS = 16
OUT_ROWS = 256
RISE_LIMIT = 100.0
VMEM_LIMIT_BYTES = 56 * 2 ** 20
Q_SCALE = ATTN_SCALE * math.log2(math.e)


def _rms(x, g):
    ms = jnp.mean(x * x, axis=-1, keepdims=True)
    return x * lax.rsqrt(ms + EPS) * g


def _silu(x):
    hx = (0.5 * x).astype(BF16)
    return hx + hx * jnp.tanh(hx)


def _dot(a, b):
    return jnp.dot(a, b, preferred_element_type=F32)


def _dot_nt(a, b):
    return lax.dot_general(a, b, (((1,), (1,)), ((), ())), preferred_element_type=F32)


def _in_proj(x, norm_g, w_head, w_tail):
    u = _rms(x, norm_g).astype(BF16)
    return _dot(u, w_head), _dot(u, w_tail)


def _keys_values(p, cos, sin, kv_g, w_k, w_vt):
    c_kv = _rms(p[:, C_KV:C_KV + KV_LORA], kv_g).astype(BF16)
    k_nope = _dot(c_kv, w_k)
    v_t = _dot_nt(w_vt, c_kv)
    kr = p[:, C_KROPE:C_KROPE + LANES]
    k_pe = kr * cos + pltpu.roll(kr, ROPE, 1) * sin
    lane = lax.broadcasted_iota(jnp.int32, k_pe.shape, 1)
    k_pe = jnp.where(lane == REF_ROW - NOPE, 1.0, k_pe).astype(BF16)
    ks = []
    for h in range(HEADS):
        ks.append(k_nope[:, h * NOPE:(h + 1) * NOPE].astype(BF16))
        ks.append(k_pe)
    return ks, v_t.astype(BF16)


def _meta_kernel(x_ref, cos_ref, sin_ref, norm_g_ref, w_head_ref, w_tail_ref, kv_g_ref, w_k_ref, w_vt_ref,
                 k_out, vt_out, g_out):
    ph, p = _in_proj(x_ref[...], norm_g_ref[...], w_head_ref[...], w_tail_ref[...])
    ks, v_t = _keys_values(ph, cos_ref[...], sin_ref[...], kv_g_ref[...], w_k_ref[...], w_vt_ref[...])
    for i, kk in enumerate(ks):
        k_out[:, i * LANES:(i + 1) * LANES] = kk
    vt_out[...] = v_t
    g_out[...] = p[:, C_CONV_C:C_CONV_C + CONV_W] * p[:, C_CONV_H:C_CONV_H + CONV_W]


def _proj_kernel(x_ref, cos_ref, sin_ref, cos_t_ref, sin_t_ref, ginit_ref, norm_g_ref, w_head_ref, w_tail_ref,
                 q_g_ref, w_qt_ref, kv_g_ref, w_k_ref, w_vt_ref, conv_w_ref, conv_g_ref, gmat_ref,
                 qt_out, k_out, vt_out, ga_out, yc_out, gbuf):
    tm = x_ref.shape[1]

    @pl.when(pl.program_id(1) == 0)
    def _():
        gbuf[...] = ginit_ref[8:16, :]

    ph, p = _in_proj(x_ref[0], norm_g_ref[...], w_head_ref[...], w_tail_ref[...])

    c_q = _rms(ph[:, C_Q:C_Q + Q_LORA], q_g_ref[...] * Q_SCALE).astype(BF16)
    q_t = _dot_nt(w_qt_ref[...], c_q)
    cos_t = cos_t_ref[...]
    sin_t = sin_t_ref[...]
    for h in range(HEADS):
        r = h * HEAD_PAD
        qt_out[0, r:r + NOPE, :] = q_t[r:r + NOPE, :].astype(BF16)
        q_pe = q_t[r + NOPE:r + NOPE + ROPE, :] * cos_t + q_t[r + NOPE + ROPE:r + HEAD_PAD, :] * sin_t
        qt_out[0, r + NOPE:r + NOPE + ROPE, :] = q_pe.astype(BF16)
        qt_out[0, r + NOPE + ROPE:r + HEAD_PAD, :] = jnp.zeros((ROPE, tm), BF16)

    ks, v_t = _keys_values(ph, cos_ref[...], sin_ref[...], kv_g_ref[...], w_k_ref[...], w_vt_ref[...])
    for i, kk in enumerate(ks):
        k_out[0, :, i * LANES:(i + 1) * LANES] = kk
    vt_out[0, 0] = v_t

    ga_out[0] = _silu(p[:, C_Z_ATTN:C_Z_ATTN + HEADS * VDIM]).astype(BF16)

    g = p[:, C_CONV_C:C_CONV_C + CONV_W] * p[:, C_CONV_H:C_CONV_H + CONV_W]
    carry = gbuf[...]
    first = lax.broadcasted_iota(jnp.int32, (8, CONV_W), 0) == 0

    def shift_down(v, row_before):
        r = pltpu.roll(v, 1, 0)
        return jnp.concatenate([jnp.where(first, row_before, r[0:8, :]), r[8:, :]], axis=0)

    g1 = shift_down(g, carry[7:8, :])
    g2 = shift_down(g1, carry[6:7, :])
    gbuf[...] = g[tm - 8:tm, :]
    cw = conv_w_ref[...]
    conv = cw[0:1, :] * g2 + cw[1:2, :] * g1 + cw[2:3, :] * g
    yc = p[:, C_CONV_B:C_CONV_B + CONV_W] * conv
    ssum = _dot((yc * yc).astype(BF16), gmat_ref[...])
    ycn = yc * lax.rsqrt(ssum * (1.0 / CONV_GROUP) + EPS) * conv_g_ref[...]
    yc_out[0] = (ycn * _silu(p[:, C_Z_CONV:C_Z_CONV + CONV_W])).astype(BF16)


def _ones_rows(v_t):
    return jnp.concatenate([v_t, jnp.ones((ONES_ROWS, v_t.shape[1]), BF16)], axis=0)


def _bf16_exact(x):
    return x.astype(BF16).astype(F32)


def _attention_tile(c, qt_ref, k_ref, vt_ref, km_ref, vtm_ref, q_aug, m_scr, acc_scr, p_scr, rise_scr):
    tri_half = (lax.broadcasted_iota(jnp.int32, (HALF, HALF), 0)
                <= lax.broadcasted_iota(jnp.int32, (HALF, HALF), 1))

    def k_of(h, r0, r1):
        return k_ref[0, r0:r1, h * HEAD_PAD:(h + 1) * HEAD_PAD]

    def v_of(h, j, lo, hi):
        return vt_ref[0, j, h * VDIM:(h + 1) * VDIM, lo:hi]

    row = lax.broadcasted_iota(jnp.int32, (HALF + N_META, TILE), 0)
    col = lax.broadcasted_iota(jnp.int32, (HALF + N_META, TILE), 1)
    first_mask = jnp.logical_or(row >= HALF, row <= col)
    tasks = []
    for h in range(HEADS):
        tasks.append((h, lambda h=h: jnp.concatenate([k_of(h, c * TILE, c * TILE + HALF),
                                                      km_ref[:, h * HEAD_PAD:(h + 1) * HEAD_PAD]], axis=0),
                      lambda h=h: jnp.concatenate([v_of(h, c, 0, HALF), vtm_ref[h * VDIM:(h + 1) * VDIM, :]], axis=1),
                      0, TILE, first_mask, True))
    for j in range(c):
        for h in range(HEADS):
            tasks.append((h, lambda h=h, j=j: k_of(h, j * TILE, (j + 1) * TILE),
                          lambda h=h, j=j: v_of(h, j, 0, TILE), 0, TILE, None, False))
    for h in range(HEADS):
        tasks.append((h, lambda h=h: k_of(h, c * TILE + HALF, (c + 1) * TILE),
                      lambda h=h: v_of(h, c, HALF, TILE), HALF, TILE, tri_half, False))
    assert QK_AHEAD <= 2 and all(t[0] == n % HEADS for n, t in enumerate(tasks))

    for h in range(HEADS):
        q_aug[h] = qt_ref[0, h * HEAD_PAD:(h + 1) * HEAD_PAD, :]
    rise_scr[...] = jnp.zeros(rise_scr.shape, F32)

    def scores(task):
        h, k_fn, _, lo, hi, mask, _ = task
        s = _dot(k_fn(), q_aug[h, :, lo:hi])
        return s if mask is None else jnp.where(mask, s, NEG_INF)

    def softmax(n, task, s):
        h, _, _, lo, hi, _, first = task
        nkeys, ncols = s.shape
        blk_max = jnp.max(s, axis=0, keepdims=True)
        if first:
            m_new, beta = _bf16_exact(blk_max), None
            p = jnp.exp2(s - m_new)
        else:
            rise = jnp.maximum(blk_max, 0.0)
            p = jnp.exp2(s)
            m_old = m_scr[h, :, lo:hi]
            m_new = _bf16_exact(m_old + rise)
            beta = jnp.exp2(m_old - m_new)
            rise_scr[:, lo:hi] = jnp.maximum(rise_scr[:, lo:hi], rise)
        p_scr[n % 2, 0:nkeys, 0:ncols] = p.astype(BF16)
        m_scr[h, :, lo:hi] = m_new
        q_aug[h, REF_ROW:REF_ROW + REF_ROWS, lo:hi] = jnp.broadcast_to(-m_new, (REF_ROWS, ncols)).astype(BF16)
        return nkeys, ncols, beta

    def values(n, task, nkeys, ncols, beta):
        h, _, v_fn, lo, hi, _, _ = task
        pv = _dot(_ones_rows(v_fn()), p_scr[n % 2, 0:nkeys, 0:ncols])
        if beta is None:
            acc_scr[h, :, lo:hi] = pv
        else:
            acc_scr[h, :, lo:hi] = (acc_scr[h, :, lo:hi] + pv) * beta

    pending = [scores(t) for t in tasks[:QK_AHEAD]]
    prev = None
    for n, task in enumerate(tasks):
        if prev is not None:
            values(*prev)
        if n + QK_AHEAD < len(tasks):
            pending.append(scores(tasks[n + QK_AHEAD]))
        prev = (n, task) + softmax(n, task, pending.pop(0))
    values(*prev)


def _attention_tile_two_pass(i, qt_ref, k_ref, vt_ref, km_ref, vtm_ref, m_scr, acc_scr):
    tri_full = (lax.broadcasted_iota(jnp.int32, (HALF, TILE), 0)
                <= lax.broadcasted_iota(jnp.int32, (HALF, TILE), 1))
    tri_half = tri_full[:, :HALF]

    def q_of(h, lo, hi):
        return qt_ref[0, h * HEAD_PAD:(h + 1) * HEAD_PAD, lo:hi]

    def update(h, s, v_t, lo, hi):
        m_old = m_scr[h, :, lo:hi]
        m_new = jnp.maximum(m_old, jnp.max(s, axis=0, keepdims=True))
        pv = _dot(_ones_rows(v_t), jnp.exp2(s - m_new).astype(BF16))
        acc_scr[h, :, lo:hi] = jnp.exp2(m_old - m_new) * acc_scr[h, :, lo:hi] + pv
        m_scr[h, :, lo:hi] = m_new

    for h in range(HEADS):
        s = _dot(km_ref[:, h * HEAD_PAD:(h + 1) * HEAD_PAD], q_of(h, 0, TILE))
        m = jnp.max(s, axis=0, keepdims=True)
        m_scr[h] = m
        acc_scr[h] = _dot(_ones_rows(vtm_ref[h * VDIM:(h + 1) * VDIM, :]), jnp.exp2(s - m).astype(BF16))

    def full_block(j, carry):
        start = pl.multiple_of(j * TILE, TILE)
        for h in range(HEADS):
            k = k_ref[0, pl.ds(start, TILE), h * HEAD_PAD:(h + 1) * HEAD_PAD]
            update(h, _dot(k, q_of(h, 0, TILE)), vt_ref[0, j, h * VDIM:(h + 1) * VDIM, :], 0, TILE)
        return carry

    lax.fori_loop(0, i, full_block, 0)

    d0 = pl.multiple_of(i * TILE, TILE)
    d1 = pl.multiple_of(i * TILE + HALF, HALF)
    for h in range(HEADS):
        k = k_ref[0, pl.ds(d0, HALF), h * HEAD_PAD:(h + 1) * HEAD_PAD]
        s = jnp.where(tri_full, _dot(k, q_of(h, 0, TILE)), NEG_INF)
        update(h, s, vt_ref[0, i, h * VDIM:(h + 1) * VDIM, 0:HALF], 0, TILE)
    for h in range(HEADS):
        k = k_ref[0, pl.ds(d1, HALF), h * HEAD_PAD:(h + 1) * HEAD_PAD]
        s = jnp.where(tri_half, _dot(k, q_of(h, HALF, TILE)), NEG_INF)
        update(h, s, vt_ref[0, i, h * VDIM:(h + 1) * VDIM, HALF:TILE], HALF, TILE)


def _finish_tile(acc_scr, ga_ref, yc_ref, x_ref, w_out_ref, attn_g_ref, final_g_ref, o_ref, y_scr):
    for h in range(HEADS):
        o_t = acc_scr[h, 0:VDIM, :] * (1.0 / acc_scr[h, VDIM:VDIM + 1, :])
        o = _rms(o_t.T, attn_g_ref[:, h * VDIM:(h + 1) * VDIM])
        y_scr[:, h * VDIM:(h + 1) * VDIM] = (o * ga_ref[0, :, h * VDIM:(h + 1) * VDIM].astype(F32)).astype(BF16)

    y_scr[:, HEADS * VDIM:] = yc_ref[0]
    for r0 in range(0, TILE, OUT_ROWS):
        mix = _dot(y_scr[r0:r0 + OUT_ROWS, :], w_out_ref[...])
        o_ref[0, r0:r0 + OUT_ROWS, :] = _rms(x_ref[0, r0:r0 + OUT_ROWS, :] + mix, final_g_ref[...])


def _attn_kernel(qt_ref, k_ref, vt_ref, km_ref, vtm_ref, ga_ref, yc_ref, x_ref, w_out_ref,
                 attn_g_ref, final_g_ref, o_ref, q_aug, m_scr, acc_scr, p_scr, rise_scr, y_scr):
    nblk = k_ref.shape[1] // TILE
    finish = functools.partial(_finish_tile, acc_scr, ga_ref, yc_ref, x_ref, w_out_ref, attn_g_ref,
                               final_g_ref, o_ref, y_scr)
    for c in range(nblk):
        @pl.when(pl.program_id(1) == c)
        def _(c=c):
            _attention_tile(c, qt_ref, k_ref, vt_ref, km_ref, vtm_ref, q_aug, m_scr, acc_scr, p_scr, rise_scr)
            finish()

    @pl.when(jnp.max(rise_scr[...]) > RISE_LIMIT)
    def _():
        _attention_tile_two_pass(pl.program_id(1), qt_ref, k_ref, vt_ref, km_ref, vtm_ref, m_scr, acc_scr)
        finish()


def _rope_angles(pos):
    half = ROPE // 2
    inv_freq = (1.0 / (ROPE_THETA ** (np.arange(half, dtype=np.float32) / half))).astype(np.float32)
    ang = pos.astype(np.float32)[:, None] * inv_freq[None, :]
    return np.cos(ang).astype(np.float32), np.sin(ang).astype(np.float32)


def _swap_halves(w):
    half = w.shape[-1] // 2
    return jnp.concatenate([w[..., half:], w[..., :half]], axis=-1)


def _full(shape):
    return pl.BlockSpec(shape, lambda *_: (0,) * len(shape))


def _layer(x, meta_tokens, norm_g, w_in, q_norm_g, w_q_up, kv_norm_g, w_kv_up, conv_w,
           attn_out_g, conv_out_g, w_out, final_norm_g):
    B, S, D = x.shape
    assert D == D_MODEL and S % TILE == 0
    assert meta_tokens.shape == (N_META, D_MODEL)
    nblk = S // TILE

    w_head = jnp.concatenate([w_in[:, :448], _swap_halves(w_in[:, 384:448])], axis=1).astype(BF16)
    w_tail = w_in[:, 448:].astype(BF16)
    wq = w_q_up.reshape(Q_LORA, HEADS, NOPE + ROPE)
    wq_t = jnp.concatenate([wq, _swap_halves(wq[..., NOPE:])], axis=-1).reshape(Q_LORA, HEADS * HEAD_PAD).T.astype(BF16)
    wkv = w_kv_up.reshape(KV_LORA, HEADS, NOPE + VDIM)
    w_k = wkv[..., :NOPE].reshape(KV_LORA, HEADS * NOPE).astype(BF16)
    w_vt = wkv[..., NOPE:].reshape(KV_LORA, HEADS * VDIM).T.astype(BF16)
    w_out_b = w_out.astype(BF16)
    gid = np.arange(CONV_W) // CONV_GROUP
    gmat = jnp.asarray((gid[:, None] == gid[None, :]).astype(np.float32), dtype=BF16)
    row = lambda v: v.reshape(1, -1).astype(F32)

    zpad = lambda n: np.zeros((n, LANES - ROPE), np.float32)
    c_m, s_m = _rope_angles(np.arange(N_META))
    cos_m = np.concatenate([c_m, c_m, zpad(N_META)], axis=-1)
    sin_m = np.concatenate([-s_m, s_m, zpad(N_META)], axis=-1)
    c_r, s_r = _rope_angles(N_META + np.arange(S))
    cos_r = np.concatenate([c_r, c_r, zpad(S)], axis=-1)
    sin_r = np.concatenate([-s_r, s_r, zpad(S)], axis=-1)
    cos_t = np.ascontiguousarray(np.concatenate([c_r, c_r], axis=-1).T)
    sin_t = np.ascontiguousarray(np.concatenate([-s_r, s_r], axis=-1).T)

    k_meta, vt_meta, g_meta = pl.pallas_call(
        _meta_kernel,
        out_shape=(jax.ShapeDtypeStruct((N_META, HEADS * HEAD_PAD), BF16),
                   jax.ShapeDtypeStruct((HEADS * VDIM, N_META), BF16),
                   jax.ShapeDtypeStruct((N_META, CONV_W), F32)),
        name="meta_proj",
    )(meta_tokens.astype(F32), cos_m, sin_m, row(norm_g), w_head, w_tail, row(kv_norm_g), w_k, w_vt)

    tok = lambda w: pl.BlockSpec((1, TILE, w), lambda b, t: (b, t, 0))
    qt_s, k_s, vt_s, ga_s, yc_s = pl.pallas_call(
        _proj_kernel,
        grid=(B, nblk),
        in_specs=[tok(D_MODEL),
                  pl.BlockSpec((TILE, LANES), lambda b, t: (t, 0)),
                  pl.BlockSpec((TILE, LANES), lambda b, t: (t, 0)),
                  pl.BlockSpec((ROPE, TILE), lambda b, t: (0, t)),
                  pl.BlockSpec((ROPE, TILE), lambda b, t: (0, t)),
                  _full((N_META, CONV_W)), _full((1, D_MODEL)),
                  _full((D_MODEL, HEAD_COLS)), _full((D_MODEL, TAIL_COLS)),
                  _full((1, Q_LORA)), _full((HEADS * HEAD_PAD, Q_LORA)),
                  _full((1, KV_LORA)), _full((KV_LORA, HEADS * NOPE)), _full((HEADS * VDIM, KV_LORA)),
                  _full((3, CONV_W)), _full((1, CONV_W)), _full((CONV_W, CONV_W))],
        out_specs=[pl.BlockSpec((1, HEADS * HEAD_PAD, TILE), lambda b, t: (b, 0, t)),
                   tok(HEADS * HEAD_PAD),
                   pl.BlockSpec((1, 1, HEADS * VDIM, TILE), lambda b, t: (b, t, 0, 0)),
                   tok(HEADS * VDIM), tok(CONV_W)],
        out_shape=[jax.ShapeDtypeStruct((B, HEADS * HEAD_PAD, S), BF16),
                   jax.ShapeDtypeStruct((B, S, HEADS * HEAD_PAD), BF16),
                   jax.ShapeDtypeStruct((B, nblk, HEADS * VDIM, TILE), BF16),
                   jax.ShapeDtypeStruct((B, S, HEADS * VDIM), BF16),
                   jax.ShapeDtypeStruct((B, S, CONV_W), BF16)],
        scratch_shapes=[pltpu.VMEM((8, CONV_W), F32)],
        compiler_params=pltpu.CompilerParams(dimension_semantics=("arbitrary", "arbitrary"),
                                             vmem_limit_bytes=VMEM_LIMIT_BYTES),
        name="proj",
    )(x, cos_r, sin_r, cos_t, sin_t, g_meta, row(norm_g), w_head, w_tail, row(q_norm_g), wq_t,
      row(kv_norm_g), w_k, w_vt, conv_w.astype(F32), row(conv_out_g), gmat)

    qtile = lambda w: pl.BlockSpec((1, TILE, w), lambda b, i: (b, i, 0))
    out = pl.pallas_call(
        _attn_kernel,
        grid=(B, nblk),
        in_specs=[pl.BlockSpec((1, HEADS * HEAD_PAD, TILE), lambda b, i: (b, 0, i)),
                  pl.BlockSpec((1, S, HEADS * HEAD_PAD), lambda b, i: (b, 0, 0)),
                  pl.BlockSpec((1, nblk, HEADS * VDIM, TILE), lambda b, i: (b, 0, 0, 0)),
                  _full((N_META, HEADS * HEAD_PAD)), _full((HEADS * VDIM, N_META)),
                  qtile(HEADS * VDIM), qtile(CONV_W), qtile(D_MODEL),
                  _full((D_MODEL, D_MODEL)), _full((1, HEADS * VDIM)), _full((1, D_MODEL))],
        out_specs=qtile(D_MODEL),
        out_shape=jax.ShapeDtypeStruct((B, S, D_MODEL), F32),
        scratch_shapes=[pltpu.VMEM((HEADS, HEAD_PAD, TILE), BF16),
                        pltpu.VMEM((HEADS, 1, TILE), F32),
                        pltpu.VMEM((HEADS, VDIM + ONES_ROWS, TILE), F32),
                        pltpu.VMEM((2, TILE, TILE), BF16),
                        pltpu.VMEM((1, TILE), F32),
                        pltpu.VMEM((TILE, D_MODEL), BF16)],
        compiler_params=pltpu.CompilerParams(dimension_semantics=("arbitrary", "arbitrary"),
                                             vmem_limit_bytes=VMEM_LIMIT_BYTES),
        name="attn_out",
    )(qt_s, k_s, vt_s, k_meta, vt_meta, ga_s, yc_s, x, w_out_b, row(attn_out_g), row(final_norm_g))
    return out


def kernel(x, meta_tokens, norm_g, w_in, q_norm_g, w_q_up, kv_norm_g, w_kv_up, conv_w,
           attn_out_g, conv_out_g, w_out, final_norm_g):
    assert norm_g.shape[0] == 1, "single-layer block"
    return _layer(x, meta_tokens, norm_g[0], w_in[0], q_norm_g[0], w_q_up[0], kv_norm_g[0],
                  w_kv_up[0], conv_w[0], attn_out_g[0], conv_out_g[0], w_out[0], final_norm_g)
```

```python
import functools
import math

import jax
import jax.numpy as jnp
import numpy as np
from jax import lax
from jax.experimental import pallas as pl
from jax.experimental.pallas import tpu as pltpu

F32 = jnp.float32
BF16 = jnp.bfloat16

D_MODEL = 1024
N_META = 16
HEADS = 4
NOPE = 128
ROPE = 64
VDIM = 128
Q_LORA = 256
KV_LORA = 128
CONV_W = 512
CONV_GROUP = 64
ROPE_THETA = 10000.0
ATTN_SCALE = (NOPE + ROPE) ** -0.5
NEG_INF = -1e30
EPS = 1e-6

LANES = 128
HEAD_PAD = 2 * LANES
HEAD_COLS, TAIL_COLS = 512, 2560
C_Q, C_KV, C_KROPE = 0, 256, 384
C_Z_ATTN, C_CONV_B, C_CONV_C, C_CONV_H, C_Z_CONV = 0, 512, 1024, 1536, 2048

TILE = 512
HALF = TILE // 2
QK_AHEAD = 2
ONES_ROWS = 16
REF_ROW = NOPE + ROPE
REF_ROWS = 16
OUT_ROWS = 256
RISE_LIMIT = 100.0
VMEM_LIMIT_BYTES = 56 * 2 ** 20
Q_SCALE = ATTN_SCALE * math.log2(math.e)


def _rms(x, g):
    ms = jnp.mean(x * x, axis=-1, keepdims=True)
    return x * lax.rsqrt(ms + EPS) * g


def _silu(x):
    hx = (0.5 * x).astype(BF16)
    return hx + hx * jnp.tanh(hx)


def _dot(a, b):
    return jnp.dot(a, b, preferred_element_type=F32)


def _dot_nt(a, b):
    return lax.dot_general(a, b, (((1,), (1,)), ((), ())), preferred_element_type=F32)


def _in_proj(x, norm_g, w_head, w_tail):
    u = _rms(x, norm_g).astype(BF16)
    return _dot(u, w_head), _dot(u, w_tail)


def _keys_values(p, cos, sin, kv_g, w_k, w_vt):
    c_kv = _rms(p[:, C_KV:C_KV + KV_LORA], kv_g).astype(BF16)
    k_nope = _dot(c_kv, w_k)
    v_t = _dot_nt(w_vt, c_kv)
    kr = p[:, C_KROPE:C_KROPE + LANES]
    k_pe = kr * cos + pltpu.roll(kr, ROPE, 1) * sin
    lane = lax.broadcasted_iota(jnp.int32, k_pe.shape, 1)
    k_pe = jnp.where(lane == REF_ROW - NOPE, 1.0, k_pe).astype(BF16)
    ks = []
    for h in range(HEADS):
        ks.append(k_nope[:, h * NOPE:(h + 1) * NOPE].astype(BF16))
        ks.append(k_pe)
    return ks, v_t.astype(BF16)


def _meta_kernel(x_ref, cos_ref, sin_ref, norm_g_ref, w_head_ref, w_tail_ref, kv_g_ref, w_k_ref, w_vt_ref,
                 k_out, vt_out, g_out):
    ph, p = _in_proj(x_ref[...], norm_g_ref[...], w_head_ref[...], w_tail_ref[...])
    ks, v_t = _keys_values(ph, cos_ref[...], sin_ref[...], kv_g_ref[...], w_k_ref[...], w_vt_ref[...])
    for i, kk in enumerate(ks):
        k_out[:, i * LANES:(i + 1) * LANES] = kk
    vt_out[...] = v_t
    g_out[...] = p[:, C_CONV_C:C_CONV_C + CONV_W] * p[:, C_CONV_H:C_CONV_H + CONV_W]


def _proj_kernel(x_ref, cos_ref, sin_ref, cos_t_ref, sin_t_ref, ginit_ref, norm_g_ref, w_head_ref, w_tail_ref,
                 q_g_ref, w_qt_ref, kv_g_ref, w_k_ref, w_vt_ref, conv_w_ref, conv_g_ref, gmat_ref,
                 qt_out, k_out, vt_out, ga_out, yc_out, gbuf):
    tm = x_ref.shape[1]

    @pl.when(pl.program_id(1) == 0)
    def _():
        gbuf[...] = ginit_ref[8:16, :]

    ph, p = _in_proj(x_ref[0], norm_g_ref[...], w_head_ref[...], w_tail_ref[...])

    c_q = _rms(ph[:, C_Q:C_Q + Q_LORA], q_g_ref[...] * Q_SCALE).astype(BF16)
    q_t = _dot_nt(w_qt_ref[...], c_q)
    cos_t = cos_t_ref[...]
    sin_t = sin_t_ref[...]
    for h in range(HEADS):
        r = h * HEAD_PAD
        qt_out[0, r:r + NOPE, :] = q_t[r:r + NOPE, :].astype(BF16)
        q_pe = q_t[r + NOPE:r + NOPE + ROPE, :] * cos_t + q_t[r + NOPE + ROPE:r + HEAD_PAD, :] * sin_t
        qt_out[0, r + NOPE:r + NOPE + ROPE, :] = q_pe.astype(BF16)
        qt_out[0, r + NOPE + ROPE:r + HEAD_PAD, :] = jnp.zeros((ROPE, tm), BF16)

    ks, v_t = _keys_values(ph, cos_ref[...], sin_ref[...], kv_g_ref[...], w_k_ref[...], w_vt_ref[...])
    for i, kk in enumerate(ks):
        k_out[0, :, i * LANES:(i + 1) * LANES] = kk
    vt_out[0, 0] = v_t

    ga_out[0] = _silu(p[:, C_Z_ATTN:C_Z_ATTN + HEADS * VDIM]).astype(BF16)

    g = p[:, C_CONV_C:C_CONV_C + CONV_W] * p[:, C_CONV_H:C_CONV_H + CONV_W]
    carry = gbuf[...]
    first = lax.broadcasted_iota(jnp.int32, (8, CONV_W), 0) == 0

    def shift_down(v, row_before):
        r = pltpu.roll(v, 1, 0)
        return jnp.concatenate([jnp.where(first, row_before, r[0:8, :]), r[8:, :]], axis=0)

    g1 = shift_down(g, carry[7:8, :])
    g2 = shift_down(g1, carry[6:7, :])
    gbuf[...] = g[tm - 8:tm, :]
    cw = conv_w_ref[...]
    conv = cw[0:1, :] * g2 + cw[1:2, :] * g1 + cw[2:3, :] * g
    yc = p[:, C_CONV_B:C_CONV_B + CONV_W] * conv
    ssum = _dot((yc * yc).astype(BF16), gmat_ref[...])
    ycn = yc * lax.rsqrt(ssum * (1.0 / CONV_GROUP) + EPS) * conv_g_ref[...]
    yc_out[0] = (ycn * _silu(p[:, C_Z_CONV:C_Z_CONV + CONV_W])).astype(BF16)


def _ones_rows(v_t):
    return jnp.concatenate([v_t, jnp.ones((ONES_ROWS, v_t.shape[1]), BF16)], axis=0)


def _bf16_exact(x):
    return x.astype(BF16).astype(F32)


def _attention_tile(c, qt_ref, k_ref, vt_ref, km_ref, vtm_ref, q_aug, m_scr, acc_scr, p_scr, rise_scr):
    tri_full = (lax.broadcasted_iota(jnp.int32, (HALF, TILE), 0)
                <= lax.broadcasted_iota(jnp.int32, (HALF, TILE), 1))
    tri_half = tri_full[:, :HALF]

    def k_of(h, r0, r1):
        return k_ref[0, r0:r1, h * HEAD_PAD:(h + 1) * HEAD_PAD]

    def v_of(h, j, lo, hi):
        return vt_ref[0, j, h * VDIM:(h + 1) * VDIM, lo:hi]

    tasks = []
    for h in range(HEADS):
        tasks.append((h, lambda h=h: km_ref[:, h * HEAD_PAD:(h + 1) * HEAD_PAD],
                      lambda h=h: vtm_ref[h * VDIM:(h + 1) * VDIM, :], 0, TILE, None, True))
    for j in range(c):
        for h in range(HEADS):
            tasks.append((h, lambda h=h, j=j: k_of(h, j * TILE, (j + 1) * TILE),
                          lambda h=h, j=j: v_of(h, j, 0, TILE), 0, TILE, None, False))
    for h in range(HEADS):
        tasks.append((h, lambda h=h: k_of(h, c * TILE, c * TILE + HALF),
                      lambda h=h: v_of(h, c, 0, HALF), 0, TILE, tri_full, False))
    for h in range(HEADS):
        tasks.append((h, lambda h=h: k_of(h, c * TILE + HALF, (c + 1) * TILE),
                      lambda h=h: v_of(h, c, HALF, TILE), HALF, TILE, tri_half, False))
    assert QK_AHEAD <= 2 and all(t[0] == n % HEADS for n, t in enumerate(tasks))

    for h in range(HEADS):
        q_aug[h] = qt_ref[0, h * HEAD_PAD:(h + 1) * HEAD_PAD, :]
    rise_scr[...] = jnp.zeros(rise_scr.shape, F32)

    def scores(task):
        h, k_fn, _, lo, hi, mask, _ = task
        s = _dot(k_fn(), q_aug[h, :, lo:hi])
        return s if mask is None else jnp.where(mask, s, NEG_INF)

    def softmax(n, task, s):
        h, _, _, lo, hi, _, first = task
        nkeys, ncols = s.shape
        blk_max = jnp.max(s, axis=0, keepdims=True)
        if first:
            m_new, beta = _bf16_exact(blk_max), None
            p = jnp.exp2(s - m_new)
        else:
            rise = jnp.maximum(blk_max, 0.0)
            p = jnp.exp2(s)
            m_old = m_scr[h, :, lo:hi]
            m_new = _bf16_exact(m_old + rise)
            beta = jnp.exp2(m_old - m_new)
            rise_scr[:, lo:hi] = jnp.maximum(rise_scr[:, lo:hi], rise)
        p_scr[n % 2, 0:nkeys, 0:ncols] = p.astype(BF16)
        m_scr[h, :, lo:hi] = m_new
        q_aug[h, REF_ROW:REF_ROW + REF_ROWS, lo:hi] = jnp.broadcast_to(-m_new, (REF_ROWS, ncols)).astype(BF16)
        return nkeys, ncols, beta

    def values(n, task, nkeys, ncols, beta):
        h, _, v_fn, lo, hi, _, _ = task
        pv = _dot(_ones_rows(v_fn()), p_scr[n % 2, 0:nkeys, 0:ncols])
        if beta is None:
            acc_scr[h, :, lo:hi] = pv
        else:
            acc_scr[h, :, lo:hi] = (acc_scr[h, :, lo:hi] + pv) * beta

    pending = [scores(t) for t in tasks[:QK_AHEAD]]
    prev = None
    for n, task in enumerate(tasks):
        if prev is not None:
            values(*prev)
        if n + QK_AHEAD < len(tasks):
            pending.append(scores(tasks[n + QK_AHEAD]))
        prev = (n, task) + softmax(n, task, pending.pop(0))
    values(*prev)


def _attention_tile_two_pass(i, qt_ref, k_ref, vt_ref, km_ref, vtm_ref, m_scr, acc_scr):
    tri_full = (lax.broadcasted_iota(jnp.int32, (HALF, TILE), 0)
                <= lax.broadcasted_iota(jnp.int32, (HALF, TILE), 1))
    tri_half = tri_full[:, :HALF]

    def q_of(h, lo, hi):
        return qt_ref[0, h * HEAD_PAD:(h + 1) * HEAD_PAD, lo:hi]

    def update(h, s, v_t, lo, hi):
        m_old = m_scr[h, :, lo:hi]
        m_new = jnp.maximum(m_old, jnp.max(s, axis=0, keepdims=True))
        pv = _dot(_ones_rows(v_t), jnp.exp2(s - m_new).astype(BF16))
        acc_scr[h, :, lo:hi] = jnp.exp2(m_old - m_new) * acc_scr[h, :, lo:hi] + pv
        m_scr[h, :, lo:hi] = m_new

    for h in range(HEADS):
        s = _dot(km_ref[:, h * HEAD_PAD:(h + 1) * HEAD_PAD], q_of(h, 0, TILE))
        m = jnp.max(s, axis=0, keepdims=True)
        m_scr[h] = m
        acc_scr[h] = _dot(_ones_rows(vtm_ref[h * VDIM:(h + 1) * VDIM, :]), jnp.exp2(s - m).astype(BF16))

    def full_block(j, carry):
        start = pl.multiple_of(j * TILE, TILE)
        for h in range(HEADS):
            k = k_ref[0, pl.ds(start, TILE), h * HEAD_PAD:(h + 1) * HEAD_PAD]
            update(h, _dot(k, q_of(h, 0, TILE)), vt_ref[0, j, h * VDIM:(h + 1) * VDIM, :], 0, TILE)
        return carry

    lax.fori_loop(0, i, full_block, 0)

    d0 = pl.multiple_of(i * TILE, TILE)
    d1 = pl.multiple_of(i * TILE + HALF, HALF)
    for h in range(HEADS):
        k = k_ref[0, pl.ds(d0, HALF), h * HEAD_PAD:(h + 1) * HEAD_PAD]
        s = jnp.where(tri_full, _dot(k, q_of(h, 0, TILE)), NEG_INF)
        update(h, s, vt_ref[0, i, h * VDIM:(h + 1) * VDIM, 0:HALF], 0, TILE)
    for h in range(HEADS):
        k = k_ref[0, pl.ds(d1, HALF), h * HEAD_PAD:(h + 1) * HEAD_PAD]
        s = jnp.where(tri_half, _dot(k, q_of(h, HALF, TILE)), NEG_INF)
        update(h, s, vt_ref[0, i, h * VDIM:(h + 1) * VDIM, HALF:TILE], HALF, TILE)


def _finish_tile(acc_scr, ga_ref, yc_ref, x_ref, w_out_ref, attn_g_ref, final_g_ref, o_ref, y_scr):
    for h in range(HEADS):
        o_t = acc_scr[h, 0:VDIM, :] * (1.0 / acc_scr[h, VDIM:VDIM + 1, :])
        o_t = o_t * lax.rsqrt(jnp.mean(o_t * o_t, axis=0, keepdims=True) + EPS)
        gain = attn_g_ref[:, h * VDIM:(h + 1) * VDIM] * ga_ref[0, :, h * VDIM:(h + 1) * VDIM].astype(F32)
        y_scr[:, h * VDIM:(h + 1) * VDIM] = (o_t.T * gain).astype(BF16)

    y_scr[:, HEADS * VDIM:] = yc_ref[0]
    for r0 in range(0, TILE, OUT_ROWS):
        mix = _dot(y_scr[r0:r0 + OUT_ROWS, :], w_out_ref[...])
        o_ref[0, r0:r0 + OUT_ROWS, :] = _rms(x_ref[0, r0:r0 + OUT_ROWS, :] + mix, final_g_ref[...])


def _attn_kernel(qt_ref, k_ref, vt_ref, km_ref, vtm_ref, ga_ref, yc_ref, x_ref, w_out_ref,
                 attn_g_ref, final_g_ref, o_ref, q_aug, m_scr, acc_scr, p_scr, rise_scr, y_scr):
    nblk = k_ref.shape[1] // TILE
    finish = functools.partial(_finish_tile, acc_scr, ga_ref, yc_ref, x_ref, w_out_ref, attn_g_ref,
                               final_g_ref, o_ref, y_scr)
    for c in range(nblk):
        @pl.when(pl.program_id(1) == c)
        def _(c=c):
            _attention_tile(c, qt_ref, k_ref, vt_ref, km_ref, vtm_ref, q_aug, m_scr, acc_scr, p_scr, rise_scr)
            finish()

    @pl.when(jnp.max(rise_scr[...]) > RISE_LIMIT)
    def _():
        _attention_tile_two_pass(pl.program_id(1), qt_ref, k_ref, vt_ref, km_ref, vtm_ref, m_scr, acc_scr)
        finish()


def _rope_angles(pos):
    half = ROPE // 2
    inv_freq = (1.0 / (ROPE_THETA ** (np.arange(half, dtype=np.float32) / half))).astype(np.float32)
    ang = pos.astype(np.float32)[:, None] * inv_freq[None, :]
    return np.cos(ang).astype(np.float32), np.sin(ang).astype(np.float32)


def _swap_halves(w):
    half = w.shape[-1] // 2
    return jnp.concatenate([w[..., half:], w[..., :half]], axis=-1)


def _full(shape):
    return pl.BlockSpec(shape, lambda *_: (0,) * len(shape))


def _layer(x, meta_tokens, norm_g, w_in, q_norm_g, w_q_up, kv_norm_g, w_kv_up, conv_w,
           attn_out_g, conv_out_g, w_out, final_norm_g):
    B, S, D = x.shape
    assert D == D_MODEL and S % TILE == 0
    assert meta_tokens.shape == (N_META, D_MODEL)
    nblk = S // TILE

    w_head = jnp.concatenate([w_in[:, :448], _swap_halves(w_in[:, 384:448])], axis=1).astype(BF16)
    w_tail = w_in[:, 448:].astype(BF16)
    wq = w_q_up.reshape(Q_LORA, HEADS, NOPE + ROPE)
    wq_t = jnp.concatenate([wq, _swap_halves(wq[..., NOPE:])], axis=-1).reshape(Q_LORA, HEADS * HEAD_PAD).T.astype(BF16)
    wkv = w_kv_up.reshape(KV_LORA, HEADS, NOPE + VDIM)
    w_k = wkv[..., :NOPE].reshape(KV_LORA, HEADS * NOPE).astype(BF16)
    w_vt = wkv[..., NOPE:].reshape(KV_LORA, HEADS * VDIM).T.astype(BF16)
    w_out_b = w_out.astype(BF16)
    gid = np.arange(CONV_W) // CONV_GROUP
    gmat = jnp.asarray((gid[:, None] == gid[None, :]).astype(np.float32), dtype=BF16)
    row = lambda v: v.reshape(1, -1).astype(F32)

    zpad = lambda n: np.zeros((n, LANES - ROPE), np.float32)
    c_m, s_m = _rope_angles(np.arange(N_META))
    cos_m = np.concatenate([c_m, c_m, zpad(N_META)], axis=-1)
    sin_m = np.concatenate([-s_m, s_m, zpad(N_META)], axis=-1)
    c_r, s_r = _rope_angles(N_META + np.arange(S))
    cos_r = np.concatenate([c_r, c_r, zpad(S)], axis=-1)
    sin_r = np.concatenate([-s_r, s_r, zpad(S)], axis=-1)
    cos_t = np.ascontiguousarray(np.concatenate([c_r, c_r], axis=-1).T)
    sin_t = np.ascontiguousarray(np.concatenate([-s_r, s_r], axis=-1).T)

    k_meta, vt_meta, g_meta = pl.pallas_call(
        _meta_kernel,
        out_shape=(jax.ShapeDtypeStruct((N_META, HEADS * HEAD_PAD), BF16),
                   jax.ShapeDtypeStruct((HEADS * VDIM, N_META), BF16),
                   jax.ShapeDtypeStruct((N_META, CONV_W), F32)),
        name="meta_proj",
    )(meta_tokens.astype(F32), cos_m, sin_m, row(norm_g), w_head, w_tail, row(kv_norm_g), w_k, w_vt)

    tok = lambda w: pl.BlockSpec((1, TILE, w), lambda b, t: (b, t, 0))
    qt_s, k_s, vt_s, ga_s, yc_s = pl.pallas_call(
        _proj_kernel,
        grid=(B, nblk),
        in_specs=[tok(D_MODEL),
                  pl.BlockSpec((TILE, LANES), lambda b, t: (t, 0)),
                  pl.BlockSpec((TILE, LANES), lambda b, t: (t, 0)),
                  pl.BlockSpec((ROPE, TILE), lambda b, t: (0, t)),
                  pl.BlockSpec((ROPE, TILE), lambda b, t: (0, t)),
                  _full((N_META, CONV_W)), _full((1, D_MODEL)),
                  _full((D_MODEL, HEAD_COLS)), _full((D_MODEL, TAIL_COLS)),
                  _full((1, Q_LORA)), _full((HEADS * HEAD_PAD, Q_LORA)),
                  _full((1, KV_LORA)), _full((KV_LORA, HEADS * NOPE)), _full((HEADS * VDIM, KV_LORA)),
                  _full((3, CONV_W)), _full((1, CONV_W)), _full((CONV_W, CONV_W))],
        out_specs=[pl.BlockSpec((1, HEADS * HEAD_PAD, TILE), lambda b, t: (b, 0, t)),
                   tok(HEADS * HEAD_PAD),
                   pl.BlockSpec((1, 1, HEADS * VDIM, TILE), lambda b, t: (b, t, 0, 0)),
                   tok(HEADS * VDIM), tok(CONV_W)],
        out_shape=[jax.ShapeDtypeStruct((B, HEADS * HEAD_PAD, S), BF16),
                   jax.ShapeDtypeStruct((B, S, HEADS * HEAD_PAD), BF16),
                   jax.ShapeDtypeStruct((B, nblk, HEADS * VDIM, TILE), BF16),
                   jax.ShapeDtypeStruct((B, S, HEADS * VDIM), BF16),
                   jax.ShapeDtypeStruct((B, S, CONV_W), BF16)],
        scratch_shapes=[pltpu.VMEM((8, CONV_W), F32)],
        compiler_params=pltpu.CompilerParams(dimension_semantics=("arbitrary", "arbitrary"),
                                             vmem_limit_bytes=VMEM_LIMIT_BYTES),
        name="proj",
    )(x, cos_r, sin_r, cos_t, sin_t, g_meta, row(norm_g), w_head, w_tail, row(q_norm_g), wq_t,
      row(kv_norm_g), w_k, w_vt, conv_w.astype(F32), row(conv_out_g), gmat)

    qtile = lambda w: pl.BlockSpec((1, TILE, w), lambda b, i: (b, i, 0))
    out = pl.pallas_call(
        _attn_kernel,
        grid=(B, nblk),
        in_specs=[pl.BlockSpec((1, HEADS * HEAD_PAD, TILE), lambda b, i: (b, 0, i)),
                  pl.BlockSpec((1, S, HEADS * HEAD_PAD), lambda b, i: (b, 0, 0)),
                  pl.BlockSpec((1, nblk, HEADS * VDIM, TILE), lambda b, i: (b, 0, 0, 0)),
                  _full((N_META, HEADS * HEAD_PAD)), _full((HEADS * VDIM, N_META)),
                  qtile(HEADS * VDIM), qtile(CONV_W), qtile(D_MODEL),
                  _full((D_MODEL, D_MODEL)), _full((1, HEADS * VDIM)), _full((1, D_MODEL))],
        out_specs=qtile(D_MODEL),
        out_shape=jax.ShapeDtypeStruct((B, S, D_MODEL), F32),
        scratch_shapes=[pltpu.VMEM((HEADS, HEAD_PAD, TILE), BF16),
                        pltpu.VMEM((HEADS, 1, TILE), F32),
                        pltpu.VMEM((HEADS, VDIM + ONES_ROWS, TILE), F32),
                        pltpu.VMEM((2, TILE, TILE), BF16),
                        pltpu.VMEM((1, TILE), F32),
                        pltpu.VMEM((TILE, D_MODEL), BF16)],
        compiler_params=pltpu.CompilerParams(dimension_semantics=("arbitrary", "arbitrary"),
                                             vmem_limit_bytes=VMEM_LIMIT_BYTES),
        name="attn_out",
    )(qt_s, k_s, vt_s, k_meta, vt_meta, ga_s, yc_s, x, w_out_b, row(attn_out_g), row(final_norm_g))
    return out


def kernel(x, meta_tokens, norm_g, w_in, q_norm_g, w_q_up, kv_norm_g, w_kv_up, conv_w,
           attn_out_g, conv_out_g, w_out, final_norm_g):
    assert norm_g.shape[0] == 1, "single-layer block"
    return _layer(x, meta_tokens, norm_g[0], w_in[0], q_norm_g[0], w_q_up[0], kv_norm_g[0],
                  w_kv_up[0], conv_w[0], attn_out_g[0], conv_out_g[0], w_out[0], final_norm_g)
```

```python
import functools
import math

import jax
import jax.numpy as jnp
import numpy as np
from jax import lax
from jax.experimental import pallas as pl
from jax.experimental.pallas import tpu as pltpu

F32 = jnp.float32
BF16 = jnp.bfloat16

D_MODEL = 1024
N_META = 16
HEADS = 4
NOPE = 128
ROPE = 64
VDIM = 128
Q_LORA = 256
KV_LORA = 128
CONV_W = 512
CONV_GROUP = 64
ROPE_THETA = 10000.0
ATTN_SCALE = (NOPE + ROPE) ** -0.5
NEG_INF = -1e30
EPS = 1e-6

LANES = 128
HEAD_PAD = 2 * LANES
HEAD_COLS, TAIL_COLS = 512, 2560
C_Q, C_KV, C_KROPE = 0, 256, 384
C_Z_ATTN, C_CONV_B, C_CONV_C, C_CONV_H, C_Z_CONV = 0, 512, 1024, 1536, 2048

TILE = 512
HALF = TILE // 2
QK_AHEAD = 2
ONES_ROWS = 16
REF_ROW = NOPE + ROPE
REF_ROWS = 16
OUT_ROWS = 256
RISE_LIMIT = 100.0
VMEM_LIMIT_BYTES = 56 * 2 ** 20
Q_SCALE = ATTN_SCALE * math.log2(math.e)


def _rms(x, g):
    ms = jnp.mean(x * x, axis=-1, keepdims=True)
    return x * lax.rsqrt(ms + EPS) * g


def _silu(x):
    hx = (0.5 * x).astype(BF16)
    return hx + hx * jnp.tanh(hx)


def _dot(a, b):
    return jnp.dot(a, b, preferred_element_type=F32)


def _dot_nt(a, b):
    return lax.dot_general(a, b, (((1,), (1,)), ((), ())), preferred_element_type=F32)


def _in_proj(x, norm_g, w_head, w_tail):
    u = _rms(x, norm_g).astype(BF16)
    return _dot(u, w_head), _dot(u, w_tail)


def _keys_values(p, cos, sin, kv_g, w_k, w_vt):
    c_kv = _rms(p[:, C_KV:C_KV + KV_LORA], kv_g).astype(BF16)
    k_nope = _dot(c_kv, w_k)
    v_t = _dot_nt(w_vt, c_kv)
    kr = p[:, C_KROPE:C_KROPE + LANES]
    k_pe = kr * cos + pltpu.roll(kr, ROPE, 1) * sin
    lane = lax.broadcasted_iota(jnp.int32, k_pe.shape, 1)
    k_pe = jnp.where(lane == REF_ROW - NOPE, 1.0, k_pe).astype(BF16)
    ks = []
    for h in range(HEADS):
        ks.append(k_nope[:, h * NOPE:(h + 1) * NOPE].astype(BF16))
        ks.append(k_pe)
    return ks, v_t.astype(BF16)


def _meta_kernel(x_ref, cos_ref, sin_ref, norm_g_ref, w_head_ref, w_tail_ref, kv_g_ref, w_k_ref, w_vt_ref,
                 k_out, vt_out, g_out):
    ph, p = _in_proj(x_ref[...], norm_g_ref[...], w_head_ref[...], w_tail_ref[...])
    ks, v_t = _keys_values(ph, cos_ref[...], sin_ref[...], kv_g_ref[...], w_k_ref[...], w_vt_ref[...])
    for i, kk in enumerate(ks):
        k_out[:, i * LANES:(i + 1) * LANES] = kk
    vt_out[...] = v_t
    g_out[...] = p[:, C_CONV_C:C_CONV_C + CONV_W] * p[:, C_CONV_H:C_CONV_H + CONV_W]


def _proj_kernel(x_ref, cos_ref, sin_ref, cos_t_ref, sin_t_ref, ginit_ref, norm_g_ref, w_head_ref, w_tail_ref,
                 q_g_ref, w_qt_ref, kv_g_ref, w_k_ref, w_vt_ref, conv_w_ref, conv_g_ref, gmat_ref,
                 qt_out, k_out, vt_out, ga_out, yc_out, gbuf):
    tm = x_ref.shape[1]

    @pl.when(pl.program_id(1) == 0)
    def _():
        gbuf[...] = ginit_ref[8:16, :]

    ph, p = _in_proj(x_ref[0], norm_g_ref[...], w_head_ref[...], w_tail_ref[...])

    c_q = _rms(ph[:, C_Q:C_Q + Q_LORA], q_g_ref[...] * Q_SCALE).astype(BF16)
    q_t = _dot_nt(w_qt_ref[...], c_q)
    cos_t = cos_t_ref[...]
    sin_t = sin_t_ref[...]
    for h in range(HEADS):
        r = h * HEAD_PAD
        qt_out[0, r:r + NOPE, :] = q_t[r:r + NOPE, :].astype(BF16)
        q_pe = q_t[r + NOPE:r + NOPE + ROPE, :] * cos_t + q_t[r + NOPE + ROPE:r + HEAD_PAD, :] * sin_t
        qt_out[0, r + NOPE:r + NOPE + ROPE, :] = q_pe.astype(BF16)
        qt_out[0, r + NOPE + ROPE:r + HEAD_PAD, :] = jnp.zeros((ROPE, tm), BF16)

    ks, v_t = _keys_values(ph, cos_ref[...], sin_ref[...], kv_g_ref[...], w_k_ref[...], w_vt_ref[...])
    for i, kk in enumerate(ks):
        k_out[0, :, i * LANES:(i + 1) * LANES] = kk
    vt_out[0, 0] = v_t

    ga_out[0] = _silu(p[:, C_Z_ATTN:C_Z_ATTN + HEADS * VDIM]).astype(BF16)

    g = p[:, C_CONV_C:C_CONV_C + CONV_W] * p[:, C_CONV_H:C_CONV_H + CONV_W]
    carry = gbuf[...]
    first = lax.broadcasted_iota(jnp.int32, (8, CONV_W), 0) == 0

    def shift_down(v, row_before):
        r = pltpu.roll(v, 1, 0)
        return jnp.concatenate([jnp.where(first, row_before, r[0:8, :]), r[8:, :]], axis=0)

    g1 = shift_down(g, carry[7:8, :])
    g2 = shift_down(g1, carry[6:7, :])
    gbuf[...] = g[tm - 8:tm, :]
    cw = conv_w_ref[...]
    conv = cw[0:1, :] * g2 + cw[1:2, :] * g1 + cw[2:3, :] * g
    yc = p[:, C_CONV_B:C_CONV_B + CONV_W] * conv
    ssum = _dot((yc * yc).astype(BF16), gmat_ref[...])
    scaled = yc * (conv_g_ref[...] * _silu(p[:, C_Z_CONV:C_Z_CONV + CONV_W]).astype(F32))
    yc_out[0] = (scaled * lax.rsqrt(ssum * (1.0 / CONV_GROUP) + EPS)).astype(BF16)


def _ones_rows(v_t):
    return jnp.concatenate([v_t, jnp.ones((ONES_ROWS, v_t.shape[1]), BF16)], axis=0)


def _bf16_exact(x):
    return x.astype(BF16).astype(F32)


def _attention_tile(c, qt_ref, k_ref, vt_ref, km_ref, vtm_ref, q_aug, m_scr, acc_scr, p_scr, rise_scr):
    tri_full = (lax.broadcasted_iota(jnp.int32, (HALF, TILE), 0)
                <= lax.broadcasted_iota(jnp.int32, (HALF, TILE), 1))
    tri_half = tri_full[:, :HALF]

    def k_of(h, r0, r1):
        return k_ref[0, r0:r1, h * HEAD_PAD:(h + 1) * HEAD_PAD]

    def v_of(h, j, lo, hi):
        return vt_ref[0, j, h * VDIM:(h + 1) * VDIM, lo:hi]

    tasks = []
    for h in range(HEADS):
        tasks.append((h, lambda h=h: km_ref[:, h * HEAD_PAD:(h + 1) * HEAD_PAD],
                      lambda h=h: vtm_ref[h * VDIM:(h + 1) * VDIM, :], 0, TILE, None, True))
    for j in range(c):
        for h in range(HEADS):
            tasks.append((h, lambda h=h, j=j: k_of(h, j * TILE, (j + 1) * TILE),
                          lambda h=h, j=j: v_of(h, j, 0, TILE), 0, TILE, None, False))
    for h in range(HEADS):
        tasks.append((h, lambda h=h: k_of(h, c * TILE, c * TILE + HALF),
                      lambda h=h: v_of(h, c, 0, HALF), 0, TILE, tri_full, False))
    for h in range(HEADS):
        tasks.append((h, lambda h=h: k_of(h, c * TILE + HALF, (c + 1) * TILE),
                      lambda h=h: v_of(h, c, HALF, TILE), HALF, TILE, tri_half, False))
    assert QK_AHEAD <= 2 and all(t[0] == n % HEADS for n, t in enumerate(tasks))

    for h in range(HEADS):
        q_aug[h] = qt_ref[0, h * HEAD_PAD:(h + 1) * HEAD_PAD, :]
    rise_scr[...] = jnp.zeros(rise_scr.shape, F32)

    def scores(task):
        h, k_fn, _, lo, hi, mask, _ = task
        s = _dot(k_fn(), q_aug[h, :, lo:hi])
        return s if mask is None else jnp.where(mask, s, NEG_INF)

    def softmax(n, task, s):
        h, _, _, lo, hi, _, first = task
        nkeys, ncols = s.shape
        blk_max = jnp.max(s, axis=0, keepdims=True)
        if first:
            m_new, beta = _bf16_exact(blk_max), None
            p = jnp.exp2(s - m_new)
        else:
            rise = jnp.maximum(blk_max, 0.0)
            p = jnp.exp2(s)
            m_old = m_scr[h, :, lo:hi]
            m_new = _bf16_exact(m_old + rise)
            beta = jnp.exp2(m_old - m_new)
            rise_scr[:, lo:hi] = jnp.maximum(rise_scr[:, lo:hi], rise)
        p_scr[n % 2, 0:nkeys, 0:ncols] = p.astype(BF16)
        m_scr[h, :, lo:hi] = m_new
        q_aug[h, REF_ROW:REF_ROW + REF_ROWS, lo:hi] = jnp.broadcast_to(-m_new, (REF_ROWS, ncols)).astype(BF16)
        return nkeys, ncols, beta

    def values(n, task, nkeys, ncols, beta):
        h, _, v_fn, lo, hi, _, _ = task
        pv = _dot(_ones_rows(v_fn()), p_scr[n % 2, 0:nkeys, 0:ncols])
        if beta is None:
            acc_scr[h, :, lo:hi] = pv
        else:
            acc_scr[h, :, lo:hi] = (acc_scr[h, :, lo:hi] + pv) * beta

    pending = [scores(t) for t in tasks[:QK_AHEAD]]
    prev = None
    for n, task in enumerate(tasks):
        if prev is not None:
            values(*prev)
        if n + QK_AHEAD < len(tasks):
            pending.append(scores(tasks[n + QK_AHEAD]))
        prev = (n, task) + softmax(n, task, pending.pop(0))
    values(*prev)


def _attention_tile_two_pass(i, qt_ref, k_ref, vt_ref, km_ref, vtm_ref, m_scr, acc_scr):
    tri_full = (lax.broadcasted_iota(jnp.int32, (HALF, TILE), 0)
                <= lax.broadcasted_iota(jnp.int32, (HALF, TILE), 1))
    tri_half = tri_full[:, :HALF]

    def q_of(h, lo, hi):
        return qt_ref[0, h * HEAD_PAD:(h + 1) * HEAD_PAD, lo:hi]

    def update(h, s, v_t, lo, hi):
        m_old = m_scr[h, :, lo:hi]
        m_new = jnp.maximum(m_old, jnp.max(s, axis=0, keepdims=True))
        pv = _dot(_ones_rows(v_t), jnp.exp2(s - m_new).astype(BF16))
        acc_scr[h, :, lo:hi] = jnp.exp2(m_old - m_new) * acc_scr[h, :, lo:hi] + pv
        m_scr[h, :, lo:hi] = m_new

    for h in range(HEADS):
        s = _dot(km_ref[:, h * HEAD_PAD:(h + 1) * HEAD_PAD], q_of(h, 0, TILE))
        m = jnp.max(s, axis=0, keepdims=True)
        m_scr[h] = m
        acc_scr[h] = _dot(_ones_rows(vtm_ref[h * VDIM:(h + 1) * VDIM, :]), jnp.exp2(s - m).astype(BF16))

    def full_block(j, carry):
        start = pl.multiple_of(j * TILE, TILE)
        for h in range(HEADS):
            k = k_ref[0, pl.ds(start, TILE), h * HEAD_PAD:(h + 1) * HEAD_PAD]
            update(h, _dot(k, q_of(h, 0, TILE)), vt_ref[0, j, h * VDIM:(h + 1) * VDIM, :], 0, TILE)
        return carry

    lax.fori_loop(0, i, full_block, 0)

    d0 = pl.multiple_of(i * TILE, TILE)
    d1 = pl.multiple_of(i * TILE + HALF, HALF)
    for h in range(HEADS):
        k = k_ref[0, pl.ds(d0, HALF), h * HEAD_PAD:(h + 1) * HEAD_PAD]
        s = jnp.where(tri_full, _dot(k, q_of(h, 0, TILE)), NEG_INF)
        update(h, s, vt_ref[0, i, h * VDIM:(h + 1) * VDIM, 0:HALF], 0, TILE)
    for h in range(HEADS):
        k = k_ref[0, pl.ds(d1, HALF), h * HEAD_PAD:(h + 1) * HEAD_PAD]
        s = jnp.where(tri_half, _dot(k, q_of(h, HALF, TILE)), NEG_INF)
        update(h, s, vt_ref[0, i, h * VDIM:(h + 1) * VDIM, HALF:TILE], HALF, TILE)


def _finish_tile(acc_scr, ga_ref, yc_ref, x_ref, w_out_ref, attn_g_ref, final_g_ref, o_ref, y_scr):
    for h in range(HEADS):
        a = acc_scr[h, 0:VDIM, :]
        l = acc_scr[h, VDIM:VDIM + 1, :]
        o_t = a * lax.rsqrt(jnp.mean(a * a, axis=0, keepdims=True) + EPS * (l * l))
        gain = attn_g_ref[:, h * VDIM:(h + 1) * VDIM] * ga_ref[0, :, h * VDIM:(h + 1) * VDIM].astype(F32)
        y_scr[:, h * VDIM:(h + 1) * VDIM] = (o_t.T * gain).astype(BF16)

    y_scr[:, HEADS * VDIM:] = yc_ref[0]
    for r0 in range(0, TILE, OUT_ROWS):
        mix = _dot(y_scr[r0:r0 + OUT_ROWS, :], w_out_ref[...])
        o_ref[0, r0:r0 + OUT_ROWS, :] = _rms(x_ref[0, r0:r0 + OUT_ROWS, :] + mix, final_g_ref[...])


def _attn_kernel(qt_ref, k_ref, vt_ref, km_ref, vtm_ref, ga_ref, yc_ref, x_ref, w_out_ref,
                 attn_g_ref, final_g_ref, o_ref, q_aug, m_scr, acc_scr, p_scr, rise_scr, y_scr):
    nblk = k_ref.shape[1] // TILE
    finish = functools.partial(_finish_tile, acc_scr, ga_ref, yc_ref, x_ref, w_out_ref, attn_g_ref,
                               final_g_ref, o_ref, y_scr)
    for c in range(nblk):
        @pl.when(pl.program_id(1) == c)
        def _(c=c):
            _attention_tile(c, qt_ref, k_ref, vt_ref, km_ref, vtm_ref, q_aug, m_scr, acc_scr, p_scr, rise_scr)
            finish()

    @pl.when(jnp.max(rise_scr[...]) > RISE_LIMIT)
    def _():
        _attention_tile_two_pass(pl.program_id(1), qt_ref, k_ref, vt_ref, km_ref, vtm_ref, m_scr, acc_scr)
        finish()


def _rope_angles(pos):
    half = ROPE // 2
    inv_freq = (1.0 / (ROPE_THETA ** (np.arange(half, dtype=np.float32) / half))).astype(np.float32)
    ang = pos.astype(np.float32)[:, None] * inv_freq[None, :]
    return np.cos(ang).astype(np.float32), np.sin(ang).astype(np.float32)


def _swap_halves(w):
    half = w.shape[-1] // 2
    return jnp.concatenate([w[..., half:], w[..., :half]], axis=-1)


def _full(shape):
    return pl.BlockSpec(shape, lambda *_: (0,) * len(shape))


def _layer(x, meta_tokens, norm_g, w_in, q_norm_g, w_q_up, kv_norm_g, w_kv_up, conv_w,
           attn_out_g, conv_out_g, w_out, final_norm_g):
    B, S, D = x.shape
    assert D == D_MODEL and S % TILE == 0
    assert meta_tokens.shape == (N_META, D_MODEL)
    nblk = S // TILE

    w_head = jnp.concatenate([w_in[:, :448], _swap_halves(w_in[:, 384:448])], axis=1).astype(BF16)
    w_tail = w_in[:, 448:].astype(BF16)
    wq = w_q_up.reshape(Q_LORA, HEADS, NOPE + ROPE)
    wq_t = jnp.concatenate([wq, _swap_halves(wq[..., NOPE:])], axis=-1).reshape(Q_LORA, HEADS * HEAD_PAD).T.astype(BF16)
    wkv = w_kv_up.reshape(KV_LORA, HEADS, NOPE + VDIM)
    w_k = wkv[..., :NOPE].reshape(KV_LORA, HEADS * NOPE).astype(BF16)
    w_vt = wkv[..., NOPE:].reshape(KV_LORA, HEADS * VDIM).T.astype(BF16)
    w_out_b = w_out.astype(BF16)
    gid = np.arange(CONV_W) // CONV_GROUP
    gmat = jnp.asarray((gid[:, None] == gid[None, :]).astype(np.float32), dtype=BF16)
    row = lambda v: v.reshape(1, -1).astype(F32)

    zpad = lambda n: np.zeros((n, LANES - ROPE), np.float32)
    c_m, s_m = _rope_angles(np.arange(N_META))
    cos_m = np.concatenate([c_m, c_m, zpad(N_META)], axis=-1)
    sin_m = np.concatenate([-s_m, s_m, zpad(N_META)], axis=-1)
    c_r, s_r = _rope_angles(N_META + np.arange(S))
    cos_r = np.concatenate([c_r, c_r, zpad(S)], axis=-1)
    sin_r = np.concatenate([-s_r, s_r, zpad(S)], axis=-1)
    cos_t = np.ascontiguousarray(np.concatenate([c_r, c_r], axis=-1).T)
    sin_t = np.ascontiguousarray(np.concatenate([-s_r, s_r], axis=-1).T)

    k_meta, vt_meta, g_meta = pl.pallas_call(
        _meta_kernel,
        out_shape=(jax.ShapeDtypeStruct((N_META, HEADS * HEAD_PAD), BF16),
                   jax.ShapeDtypeStruct((HEADS * VDIM, N_META), BF16),
                   jax.ShapeDtypeStruct((N_META, CONV_W), F32)),
        name="meta_proj",
    )(meta_tokens.astype(F32), cos_m, sin_m, row(norm_g), w_head, w_tail, row(kv_norm_g), w_k, w_vt)

    tok = lambda w: pl.BlockSpec((1, TILE, w), lambda b, t: (b, t, 0))
    qt_s, k_s, vt_s, ga_s, yc_s = pl.pallas_call(
        _proj_kernel,
        grid=(B, nblk),
        in_specs=[tok(D_MODEL),
                  pl.BlockSpec((TILE, LANES), lambda b, t: (t, 0)),
                  pl.BlockSpec((TILE, LANES), lambda b, t: (t, 0)),
                  pl.BlockSpec((ROPE, TILE), lambda b, t: (0, t)),
                  pl.BlockSpec((ROPE, TILE), lambda b, t: (0, t)),
                  _full((N_META, CONV_W)), _full((1, D_MODEL)),
                  _full((D_MODEL, HEAD_COLS)), _full((D_MODEL, TAIL_COLS)),
                  _full((1, Q_LORA)), _full((HEADS * HEAD_PAD, Q_LORA)),
                  _full((1, KV_LORA)), _full((KV_LORA, HEADS * NOPE)), _full((HEADS * VDIM, KV_LORA)),
                  _full((3, CONV_W)), _full((1, CONV_W)), _full((CONV_W, CONV_W))],
        out_specs=[pl.BlockSpec((1, HEADS * HEAD_PAD, TILE), lambda b, t: (b, 0, t)),
                   tok(HEADS * HEAD_PAD),
                   pl.BlockSpec((1, 1, HEADS * VDIM, TILE), lambda b, t: (b, t, 0, 0)),
                   tok(HEADS * VDIM), tok(CONV_W)],
        out_shape=[jax.ShapeDtypeStruct((B, HEADS * HEAD_PAD, S), BF16),
                   jax.ShapeDtypeStruct((B, S, HEADS * HEAD_PAD), BF16),
                   jax.ShapeDtypeStruct((B, nblk, HEADS * VDIM, TILE), BF16),
                   jax.ShapeDtypeStruct((B, S, HEADS * VDIM), BF16),
                   jax.ShapeDtypeStruct((B, S, CONV_W), BF16)],
        scratch_shapes=[pltpu.VMEM((8, CONV_W), F32)],
        compiler_params=pltpu.CompilerParams(dimension_semantics=("arbitrary", "arbitrary"),
                                             vmem_limit_bytes=VMEM_LIMIT_BYTES),
        name="proj",
    )(x, cos_r, sin_r, cos_t, sin_t, g_meta, row(norm_g), w_head, w_tail, row(q_norm_g), wq_t,
      row(kv_norm_g), w_k, w_vt, conv_w.astype(F32), row(conv_out_g), gmat)

    qtile = lambda w: pl.BlockSpec((1, TILE, w), lambda b, i: (b, i, 0))
    out = pl.pallas_call(
        _attn_kernel,
        grid=(B, nblk),
        in_specs=[pl.BlockSpec((1, HEADS * HEAD_PAD, TILE), lambda b, i: (b, 0, i)),
                  pl.BlockSpec((1, S, HEADS * HEAD_PAD), lambda b, i: (b, 0, 0)),
                  pl.BlockSpec((1, nblk, HEADS * VDIM, TILE), lambda b, i: (b, 0, 0, 0)),
                  _full((N_META, HEADS * HEAD_PAD)), _full((HEADS * VDIM, N_META)),
                  qtile(HEADS * VDIM), qtile(CONV_W), qtile(D_MODEL),
                  _full((D_MODEL, D_MODEL)), _full((1, HEADS * VDIM)), _full((1, D_MODEL))],
        out_specs=qtile(D_MODEL),
        out_shape=jax.ShapeDtypeStruct((B, S, D_MODEL), F32),
        scratch_shapes=[pltpu.VMEM((HEADS, HEAD_PAD, TILE), BF16),
                        pltpu.VMEM((HEADS, 1, TILE), F32),
                        pltpu.VMEM((HEADS, VDIM + ONES_ROWS, TILE), F32),
                        pltpu.VMEM((2, TILE, TILE), BF16),
                        pltpu.VMEM((1, TILE), F32),
                        pltpu.VMEM((TILE, D_MODEL), BF16)],
        compiler_params=pltpu.CompilerParams(dimension_semantics=("arbitrary", "arbitrary"),
                                             vmem_limit_bytes=VMEM_LIMIT_BYTES),
        name="attn_out",
    )(qt_s, k_s, vt_s, k_meta, vt_meta, ga_s, yc_s, x, w_out_b, row(attn_out_g), row(final_norm_g))
    return out


def kernel(x, meta_tokens, norm_g, w_in, q_norm_g, w_q_up, kv_norm_g, w_kv_up, conv_w,
           attn_out_g, conv_out_g, w_out, final_norm_g):
    assert norm_g.shape[0] == 1, "single-layer block"
    return _layer(x, meta_tokens, norm_g[0], w_in[0], q_norm_g[0], w_q_up[0], kv_norm_g[0],
                  w_kv_up[0], conv_w[0], attn_out_g[0], conv_out_g[0], w_out[0], final_norm_g)
```

```python
import functools
import math

import jax
import jax.numpy as jnp
import numpy as np
from jax import lax
from jax.experimental import pallas as pl
from jax.experimental.pallas import tpu as pltpu

F32 = jnp.float32
BF16 = jnp.bfloat16

D_MODEL = 1024
N_META = 16
HEADS = 4
NOPE = 128
ROPE = 64
VDIM = 128
Q_LORA = 256
KV_LORA = 128
CONV_W = 512
CONV_GROUP = 64
ROPE_THETA = 10000.0
ATTN_SCALE = (NOPE + ROPE) ** -0.5
NEG_INF = -1e30
EPS = 1e-6

LANES = 128
HEAD_PAD = 2 * LANES
HEAD_COLS, TAIL_COLS = 512, 2560
C_Q, C_KV, C_KROPE = 0, 256, 384
C_Z_ATTN, C_CONV_B, C_CONV_C, C_CONV_H, C_Z_CONV = 0, 512, 1024, 1536, 2048

TILE = 512
HALF = TILE // 2
QK_AHEAD = 2
ONES_ROWS = 16
REF_ROW = NOPE + ROPE
REF_ROWS = 16
OUT_ROWS = 256
RISE_LIMIT = 100.0
W_PREP_STEPS = 4
VMEM_LIMIT_BYTES = 56 * 2 ** 20
Q_SCALE = ATTN_SCALE * math.log2(math.e)


def _rms(x, g):
    ms = jnp.mean(x * x, axis=-1, keepdims=True)
    return x * lax.rsqrt(ms + EPS) * g


def _silu(x):
    hx = (0.5 * x).astype(BF16)
    return hx + hx * jnp.tanh(hx)


def _dot(a, b):
    return jnp.dot(a, b, preferred_element_type=F32)


def _dot_nt(a, b):
    return lax.dot_general(a, b, (((1,), (1,)), ((), ())), preferred_element_type=F32)


def _in_proj(x, norm_g, w_head, w_tail):
    u = _rms(x, norm_g).astype(BF16)
    return _dot(u, w_head), _dot(u, w_tail)


def _keys_values(p, cos, sin, kv_g, w_k, w_vt):
    c_kv = _rms(p[:, C_KV:C_KV + KV_LORA], kv_g).astype(BF16)
    k_nope = _dot(c_kv, w_k)
    v_t = _dot_nt(w_vt, c_kv)
    kr = p[:, C_KROPE:C_KROPE + LANES]
    k_pe = kr * cos + pltpu.roll(kr, ROPE, 1) * sin
    lane = lax.broadcasted_iota(jnp.int32, k_pe.shape, 1)
    k_pe = jnp.where(lane == REF_ROW - NOPE, 1.0, k_pe).astype(BF16)
    ks = []
    for h in range(HEADS):
        ks.append(k_nope[:, h * NOPE:(h + 1) * NOPE].astype(BF16))
        ks.append(k_pe)
    return ks, v_t.astype(BF16)


def _weight_prep_kernel(w_ref, head_ref, tail_ref):
    k_rope = HEAD_COLS - 2 * ROPE
    half = ROPE // 2
    head_ref[...] = jnp.concatenate(
        [w_ref[:, 0:k_rope + ROPE], w_ref[:, k_rope + half:k_rope + ROPE], w_ref[:, k_rope:k_rope + half]],
        axis=1).astype(BF16)
    tail_ref[...] = w_ref[:, k_rope + ROPE:].astype(BF16)


def _meta_kernel(x_ref, cos_ref, sin_ref, norm_g_ref, w_head_ref, w_tail_ref, kv_g_ref, w_k_ref, w_vt_ref,
                 k_out, vt_out, g_out):
    ph, p = _in_proj(x_ref[...], norm_g_ref[...], w_head_ref[...], w_tail_ref[...])
    ks, v_t = _keys_values(ph, cos_ref[...], sin_ref[...], kv_g_ref[...], w_k_ref[...], w_vt_ref[...])
    for i, kk in enumerate(ks):
        k_out[:, i * LANES:(i + 1) * LANES] = kk
    vt_out[...] = v_t
    g_out[...] = p[:, C_CONV_C:C_CONV_C + CONV_W] * p[:, C_CONV_H:C_CONV_H + CONV_W]


def _proj_kernel(x_ref, cos_ref, sin_ref, cos_t_ref, sin_t_ref, ginit_ref, norm_g_ref, w_head_ref, w_tail_ref,
                 q_g_ref, w_qt_ref, kv_g_ref, w_k_ref, w_vt_ref, conv_w_ref, conv_g_ref, gmat_ref,
                 qt_out, k_out, vt_out, ga_out, yc_out, gbuf):
    tm = x_ref.shape[1]

    @pl.when(pl.program_id(1) == 0)
    def _():
        gbuf[...] = ginit_ref[8:16, :]

    ph, p = _in_proj(x_ref[0], norm_g_ref[...], w_head_ref[...], w_tail_ref[...])

    c_q = _rms(ph[:, C_Q:C_Q + Q_LORA], q_g_ref[...] * Q_SCALE).astype(BF16)
    q_t = _dot_nt(w_qt_ref[...], c_q)
    cos_t = cos_t_ref[...]
    sin_t = sin_t_ref[...]
    for h in range(HEADS):
        r = h * HEAD_PAD
        qt_out[0, r:r + NOPE, :] = q_t[r:r + NOPE, :].astype(BF16)
        q_pe = q_t[r + NOPE:r + NOPE + ROPE, :] * cos_t + q_t[r + NOPE + ROPE:r + HEAD_PAD, :] * sin_t
        qt_out[0, r + NOPE:r + NOPE + ROPE, :] = q_pe.astype(BF16)
        qt_out[0, r + NOPE + ROPE:r + HEAD_PAD, :] = jnp.zeros((ROPE, tm), BF16)

    ks, v_t = _keys_values(ph, cos_ref[...], sin_ref[...], kv_g_ref[...], w_k_ref[...], w_vt_ref[...])
    for i, kk in enumerate(ks):
        k_out[0, :, i * LANES:(i + 1) * LANES] = kk
    vt_out[0, 0] = v_t

    ga_out[0] = _silu(p[:, C_Z_ATTN:C_Z_ATTN + HEADS * VDIM]).astype(BF16)

    g = p[:, C_CONV_C:C_CONV_C + CONV_W] * p[:, C_CONV_H:C_CONV_H + CONV_W]
    carry = gbuf[...]
    first = lax.broadcasted_iota(jnp.int32, (8, CONV_W), 0) == 0

    def shift_down(v, row_before):
        r = pltpu.roll(v, 1, 0)
        return jnp.concatenate([jnp.where(first, row_before, r[0:8, :]), r[8:, :]], axis=0)

    g1 = shift_down(g, carry[7:8, :])
    g2 = shift_down(g1, carry[6:7, :])
    gbuf[...] = g[tm - 8:tm, :]
    cw = conv_w_ref[...]
    conv = cw[0:1, :] * g2 + cw[1:2, :] * g1 + cw[2:3, :] * g
    yc = p[:, C_CONV_B:C_CONV_B + CONV_W] * conv
    ssum = _dot((yc * yc).astype(BF16), gmat_ref[...])
    scaled = yc * (conv_g_ref[...] * _silu(p[:, C_Z_CONV:C_Z_CONV + CONV_W]).astype(F32))
    yc_out[0] = (scaled * lax.rsqrt(ssum * (1.0 / CONV_GROUP) + EPS)).astype(BF16)


def _ones_rows(v_t):
    return jnp.concatenate([v_t, jnp.ones((ONES_ROWS, v_t.shape[1]), BF16)], axis=0)


def _bf16_exact(x):
    return x.astype(BF16).astype(F32)


def _attention_tile(c, qt_ref, k_ref, vt_ref, km_ref, vtm_ref, q_aug, m_scr, acc_scr, p_scr, rise_scr):
    tri_full = (lax.broadcasted_iota(jnp.int32, (HALF, TILE), 0)
                <= lax.broadcasted_iota(jnp.int32, (HALF, TILE), 1))
    tri_half = tri_full[:, :HALF]

    def k_of(h, r0, r1):
        return k_ref[0, r0:r1, h * HEAD_PAD:(h + 1) * HEAD_PAD]

    def v_of(h, j, lo, hi):
        return vt_ref[0, j, h * VDIM:(h + 1) * VDIM, lo:hi]

    tasks = []
    for h in range(HEADS):
        tasks.append((h, lambda h=h: km_ref[:, h * HEAD_PAD:(h + 1) * HEAD_PAD],
                      lambda h=h: vtm_ref[h * VDIM:(h + 1) * VDIM, :], 0, TILE, None, True))
    for j in range(c):
        for h in range(HEADS):
            tasks.append((h, lambda h=h, j=j: k_of(h, j * TILE, (j + 1) * TILE),
                          lambda h=h, j=j: v_of(h, j, 0, TILE), 0, TILE, None, False))
    for h in range(HEADS):
        tasks.append((h, lambda h=h: k_of(h, c * TILE, c * TILE + HALF),
                      lambda h=h: v_of(h, c, 0, HALF), 0, TILE, tri_full, False))
    for h in range(HEADS):
        tasks.append((h, lambda h=h: k_of(h, c * TILE + HALF, (c + 1) * TILE),
                      lambda h=h: v_of(h, c, HALF, TILE), HALF, TILE, tri_half, False))
    assert QK_AHEAD <= 2 and all(t[0] == n % HEADS for n, t in enumerate(tasks))

    for h in range(HEADS):
        q_aug[h] = qt_ref[0, h * HEAD_PAD:(h + 1) * HEAD_PAD, :]
    rise_scr[...] = jnp.zeros(rise_scr.shape, F32)

    def scores(task):
        h, k_fn, _, lo, hi, mask, _ = task
        s = _dot(k_fn(), q_aug[h, :, lo:hi])
        return s if mask is None else jnp.where(mask, s, NEG_INF)

    def softmax(n, task, s):
        h, _, _, lo, hi, _, first = task
        nkeys, ncols = s.shape
        blk_max = jnp.max(s, axis=0, keepdims=True)
        if first:
            m_new, beta = _bf16_exact(blk_max), None
            p = jnp.exp2(s - m_new)
        else:
            rise = jnp.maximum(blk_max, 0.0)
            p = jnp.exp2(s)
            m_old = m_scr[h, :, lo:hi]
            m_new = _bf16_exact(m_old + rise)
            beta = jnp.exp2(m_old - m_new)
            rise_scr[:, lo:hi] = jnp.maximum(rise_scr[:, lo:hi], rise)
        p_scr[n % 2, 0:nkeys, 0:ncols] = p.astype(BF16)
        m_scr[h, :, lo:hi] = m_new
        q_aug[h, REF_ROW:REF_ROW + REF_ROWS, lo:hi] = jnp.broadcast_to(-m_new, (REF_ROWS, ncols)).astype(BF16)
        return nkeys, ncols, beta

    def values(n, task, nkeys, ncols, beta):
        h, _, v_fn, lo, hi, _, _ = task
        pv = _dot(_ones_rows(v_fn()), p_scr[n % 2, 0:nkeys, 0:ncols])
        if beta is None:
            acc_scr[h, :, lo:hi] = pv
        else:
            acc_scr[h, :, lo:hi] = (acc_scr[h, :, lo:hi] + pv) * beta

    pending = [scores(t) for t in tasks[:QK_AHEAD]]
    prev = None
    for n, task in enumerate(tasks):
        if prev is not None:
            values(*prev)
        if n + QK_AHEAD < len(tasks):
            pending.append(scores(tasks[n + QK_AHEAD]))
        prev = (n, task) + softmax(n, task, pending.pop(0))
    values(*prev)


def _attention_tile_two_pass(i, qt_ref, k_ref, vt_ref, km_ref, vtm_ref, m_scr, acc_scr):
    tri_full = (lax.broadcasted_iota(jnp.int32, (HALF, TILE), 0)
                <= lax.broadcasted_iota(jnp.int32, (HALF, TILE), 1))
    tri_half = tri_full[:, :HALF]

    def q_of(h, lo, hi):
        return qt_ref[0, h * HEAD_PAD:(h + 1) * HEAD_PAD, lo:hi]

    def update(h, s, v_t, lo, hi):
        m_old = m_scr[h, :, lo:hi]
        m_new = jnp.maximum(m_old, jnp.max(s, axis=0, keepdims=True))
        pv = _dot(_ones_rows(v_t), jnp.exp2(s - m_new).astype(BF16))
        acc_scr[h, :, lo:hi] = jnp.exp2(m_old - m_new) * acc_scr[h, :, lo:hi] + pv
        m_scr[h, :, lo:hi] = m_new

    for h in range(HEADS):
        s = _dot(km_ref[:, h * HEAD_PAD:(h + 1) * HEAD_PAD], q_of(h, 0, TILE))
        m = jnp.max(s, axis=0, keepdims=True)
        m_scr[h] = m
        acc_scr[h] = _dot(_ones_rows(vtm_ref[h * VDIM:(h + 1) * VDIM, :]), jnp.exp2(s - m).astype(BF16))

    def full_block(j, carry):
        start = pl.multiple_of(j * TILE, TILE)
        for h in range(HEADS):
            k = k_ref[0, pl.ds(start, TILE), h * HEAD_PAD:(h + 1) * HEAD_PAD]
            update(h, _dot(k, q_of(h, 0, TILE)), vt_ref[0, j, h * VDIM:(h + 1) * VDIM, :], 0, TILE)
        return carry

    lax.fori_loop(0, i, full_block, 0)

    d0 = pl.multiple_of(i * TILE, TILE)
    d1 = pl.multiple_of(i * TILE + HALF, HALF)
    for h in range(HEADS):
        k = k_ref[0, pl.ds(d0, HALF), h * HEAD_PAD:(h + 1) * HEAD_PAD]
        s = jnp.where(tri_full, _dot(k, q_of(h, 0, TILE)), NEG_INF)
        update(h, s, vt_ref[0, i, h * VDIM:(h + 1) * VDIM, 0:HALF], 0, TILE)
    for h in range(HEADS):
        k = k_ref[0, pl.ds(d1, HALF), h * HEAD_PAD:(h + 1) * HEAD_PAD]
        s = jnp.where(tri_half, _dot(k, q_of(h, HALF, TILE)), NEG_INF)
        update(h, s, vt_ref[0, i, h * VDIM:(h + 1) * VDIM, HALF:TILE], HALF, TILE)


def _finish_tile(acc_scr, ga_ref, yc_ref, x_ref, w_out_ref, attn_g_ref, final_g_ref, o_ref, y_scr):
    for h in range(HEADS):
        a = acc_scr[h, 0:VDIM, :]
        l = acc_scr[h, VDIM:VDIM + 1, :]
        o_t = a * lax.rsqrt(jnp.mean(a * a, axis=0, keepdims=True) + EPS * (l * l))
        gain = attn_g_ref[:, h * VDIM:(h + 1) * VDIM] * ga_ref[0, :, h * VDIM:(h + 1) * VDIM].astype(F32)
        y_scr[:, h * VDIM:(h + 1) * VDIM] = (o_t.T * gain).astype(BF16)

    y_scr[:, HEADS * VDIM:] = yc_ref[0]
    for r0 in range(0, TILE, OUT_ROWS):
        mix = _dot(y_scr[r0:r0 + OUT_ROWS, :], w_out_ref[...])
        o_ref[0, r0:r0 + OUT_ROWS, :] = _rms(x_ref[0, r0:r0 + OUT_ROWS, :] + mix, final_g_ref[...])


def _attn_kernel(qt_ref, k_ref, vt_ref, km_ref, vtm_ref, ga_ref, yc_ref, x_ref, w_out_ref,
                 attn_g_ref, final_g_ref, o_ref, q_aug, m_scr, acc_scr, p_scr, rise_scr, y_scr):
    nblk = k_ref.shape[1] // TILE
    finish = functools.partial(_finish_tile, acc_scr, ga_ref, yc_ref, x_ref, w_out_ref, attn_g_ref,
                               final_g_ref, o_ref, y_scr)
    for c in range(nblk):
        @pl.when(pl.program_id(1) == c)
        def _(c=c):
            _attention_tile(c, qt_ref, k_ref, vt_ref, km_ref, vtm_ref, q_aug, m_scr, acc_scr, p_scr, rise_scr)
            finish()

    @pl.when(jnp.max(rise_scr[...]) > RISE_LIMIT)
    def _():
        _attention_tile_two_pass(pl.program_id(1), qt_ref, k_ref, vt_ref, km_ref, vtm_ref, m_scr, acc_scr)
        finish()


def _rope_angles(pos):
    half = ROPE // 2
    inv_freq = (1.0 / (ROPE_THETA ** (np.arange(half, dtype=np.float32) / half))).astype(np.float32)
    ang = pos.astype(np.float32)[:, None] * inv_freq[None, :]
    return np.cos(ang).astype(np.float32), np.sin(ang).astype(np.float32)


def _swap_halves(w):
    half = w.shape[-1] // 2
    return jnp.concatenate([w[..., half:], w[..., :half]], axis=-1)


def _full(shape):
    return pl.BlockSpec(shape, lambda *_: (0,) * len(shape))


def _layer(x, meta_tokens, norm_g, w_in, q_norm_g, w_q_up, kv_norm_g, w_kv_up, conv_w,
           attn_out_g, conv_out_g, w_out, final_norm_g):
    B, S, D = x.shape
    assert D == D_MODEL and S % TILE == 0
    assert meta_tokens.shape == (N_META, D_MODEL)
    nblk = S // TILE

    rows = D_MODEL // W_PREP_STEPS
    w_head, w_tail = pl.pallas_call(
        _weight_prep_kernel,
        grid=(W_PREP_STEPS,),
        in_specs=[pl.BlockSpec((rows, w_in.shape[1]), lambda i: (i, 0))],
        out_specs=[pl.BlockSpec((rows, HEAD_COLS), lambda i: (i, 0)),
                   pl.BlockSpec((rows, TAIL_COLS), lambda i: (i, 0))],
        out_shape=[jax.ShapeDtypeStruct((D_MODEL, HEAD_COLS), BF16),
                   jax.ShapeDtypeStruct((D_MODEL, TAIL_COLS), BF16)],
        name="weight_prep",
    )(w_in)
    wq = w_q_up.reshape(Q_LORA, HEADS, NOPE + ROPE)
    wq_t = jnp.concatenate([wq, _swap_halves(wq[..., NOPE:])], axis=-1).reshape(Q_LORA, HEADS * HEAD_PAD).T.astype(BF16)
    wkv = w_kv_up.reshape(KV_LORA, HEADS, NOPE + VDIM)
    w_k = wkv[..., :NOPE].reshape(KV_LORA, HEADS * NOPE).astype(BF16)
    w_vt = wkv[..., NOPE:].reshape(KV_LORA, HEADS * VDIM).T.astype(BF16)
    w_out_b = w_out.astype(BF16)
    gid = np.arange(CONV_W) // CONV_GROUP
    gmat = jnp.asarray((gid[:, None] == gid[None, :]).astype(np.float32), dtype=BF16)
    row = lambda v: v.reshape(1, -1).astype(F32)

    zpad = lambda n: np.zeros((n, LANES - ROPE), np.float32)
    c_m, s_m = _rope_angles(np.arange(N_META))
    cos_m = np.concatenate([c_m, c_m, zpad(N_META)], axis=-1)
    sin_m = np.concatenate([-s_m, s_m, zpad(N_META)], axis=-1)
    c_r, s_r = _rope_angles(N_META + np.arange(S))
    cos_r = np.concatenate([c_r, c_r, zpad(S)], axis=-1)
    sin_r = np.concatenate([-s_r, s_r, zpad(S)], axis=-1)
    cos_t = np.ascontiguousarray(np.concatenate([c_r, c_r], axis=-1).T)
    sin_t = np.ascontiguousarray(np.concatenate([-s_r, s_r], axis=-1).T)

    k_meta, vt_meta, g_meta = pl.pallas_call(
        _meta_kernel,
        out_shape=(jax.ShapeDtypeStruct((N_META, HEADS * HEAD_PAD), BF16),
                   jax.ShapeDtypeStruct((HEADS * VDIM, N_META), BF16),
                   jax.ShapeDtypeStruct((N_META, CONV_W), F32)),
        name="meta_proj",
    )(meta_tokens.astype(F32), cos_m, sin_m, row(norm_g), w_head, w_tail, row(kv_norm_g), w_k, w_vt)

    tok = lambda w: pl.BlockSpec((1, TILE, w), lambda b, t: (b, t, 0))
    qt_s, k_s, vt_s, ga_s, yc_s = pl.pallas_call(
        _proj_kernel,
        grid=(B, nblk),
        in_specs=[tok(D_MODEL),
                  pl.BlockSpec((TILE, LANES), lambda b, t: (t, 0)),
                  pl.BlockSpec((TILE, LANES), lambda b, t: (t, 0)),
                  pl.BlockSpec((ROPE, TILE), lambda b, t: (0, t)),
                  pl.BlockSpec((ROPE, TILE), lambda b, t: (0, t)),
                  _full((N_META, CONV_W)), _full((1, D_MODEL)),
                  _full((D_MODEL, HEAD_COLS)), _full((D_MODEL, TAIL_COLS)),
                  _full((1, Q_LORA)), _full((HEADS * HEAD_PAD, Q_LORA)),
                  _full((1, KV_LORA)), _full((KV_LORA, HEADS * NOPE)), _full((HEADS * VDIM, KV_LORA)),
                  _full((3, CONV_W)), _full((1, CONV_W)), _full((CONV_W, CONV_W))],
        out_specs=[pl.BlockSpec((1, HEADS * HEAD_PAD, TILE), lambda b, t: (b, 0, t)),
                   tok(HEADS * HEAD_PAD),
                   pl.BlockSpec((1, 1, HEADS * VDIM, TILE), lambda b, t: (b, t, 0, 0)),
                   tok(HEADS * VDIM), tok(CONV_W)],
        out_shape=[jax.ShapeDtypeStruct((B, HEADS * HEAD_PAD, S), BF16),
                   jax.ShapeDtypeStruct((B, S, HEADS * HEAD_PAD), BF16),
                   jax.ShapeDtypeStruct((B, nblk, HEADS * VDIM, TILE), BF16),
                   jax.ShapeDtypeStruct((B, S, HEADS * VDIM), BF16),
                   jax.ShapeDtypeStruct((B, S, CONV_W), BF16)],
        scratch_shapes=[pltpu.VMEM((8, CONV_W), F32)],
        compiler_params=pltpu.CompilerParams(dimension_semantics=("arbitrary", "arbitrary"),
                                             vmem_limit_bytes=VMEM_LIMIT_BYTES),
        name="proj",
    )(x, cos_r, sin_r, cos_t, sin_t, g_meta, row(norm_g), w_head, w_tail, row(q_norm_g), wq_t,
      row(kv_norm_g), w_k, w_vt, conv_w.astype(F32), row(conv_out_g), gmat)

    qtile = lambda w: pl.BlockSpec((1, TILE, w), lambda b, i: (b, i, 0))
    out = pl.pallas_call(
        _attn_kernel,
        grid=(B, nblk),
        in_specs=[pl.BlockSpec((1, HEADS * HEAD_PAD, TILE), lambda b, i: (b, 0, i)),
                  pl.BlockSpec((1, S, HEADS * HEAD_PAD), lambda b, i: (b, 0, 0)),
                  pl.BlockSpec((1, nblk, HEADS * VDIM, TILE), lambda b, i: (b, 0, 0, 0)),
                  _full((N_META, HEADS * HEAD_PAD)), _full((HEADS * VDIM, N_META)),
                  qtile(HEADS * VDIM), qtile(CONV_W), qtile(D_MODEL),
                  _full((D_MODEL, D_MODEL)), _full((1, HEADS * VDIM)), _full((1, D_MODEL))],
        out_specs=qtile(D_MODEL),
        out_shape=jax.ShapeDtypeStruct((B, S, D_MODEL), F32),
        scratch_shapes=[pltpu.VMEM((HEADS, HEAD_PAD, TILE), BF16),
                        pltpu.VMEM((HEADS, 1, TILE), F32),
                        pltpu.VMEM((HEADS, VDIM + ONES_ROWS, TILE), F32),
                        pltpu.VMEM((2, TILE, TILE), BF16),
                        pltpu.VMEM((1, TILE), F32),
                        pltpu.VMEM((TILE, D_MODEL), BF16)],
        compiler_params=pltpu.CompilerParams(dimension_semantics=("arbitrary", "arbitrary"),
                                             vmem_limit_bytes=VMEM_LIMIT_BYTES),
        name="attn_out",
    )(qt_s, k_s, vt_s, k_meta, vt_meta, ga_s, yc_s, x, w_out_b, row(attn_out_g), row(final_norm_g))
    return out


def kernel(x, meta_tokens, norm_g, w_in, q_norm_g, w_q_up, kv_norm_g, w_kv_up, conv_w,
           attn_out_g, conv_out_g, w_out, final_norm_g):
    assert norm_g.shape[0] == 1, "single-layer block"
    return _layer(x, meta_tokens, norm_g[0], w_in[0], q_norm_g[0], w_q_up[0], kv_norm_g[0],
                  w_kv_up[0], conv_w[0], attn_out_g[0], conv_out_g[0], w_out[0], final_norm_g)
```

```python
import functools
import math

import jax
import jax.numpy as jnp
import numpy as np
from jax import lax
from jax.experimental import pallas as pl
from jax.experimental.pallas import tpu as pltpu

F32 = jnp.float32
BF16 = jnp.bfloat16

D_MODEL = 1024
N_META = 16
HEADS = 4
NOPE = 128
ROPE = 64
VDIM = 128
Q_LORA = 256
KV_LORA = 128
CONV_W = 512
CONV_GROUP = 64
ROPE_THETA = 10000.0
ATTN_SCALE = (NOPE + ROPE) ** -0.5
NEG_INF = -1e30
EPS = 1e-6

LANES = 128
HEAD_PAD = 2 * LANES
HEAD_COLS, TAIL_COLS = 512, 2560
C_Q, C_KV, C_KROPE = 0, 256, 384
C_CONV_C, C_CONV_H, C_CONV_B, C_Z_CONV, C_Z_ATTN = 0, 512, 1024, 1536, 2048

TILE = 512
HALF = TILE // 2
QK_AHEAD = 2
ONES_ROWS = 16
REF_ROW = NOPE + ROPE
REF_ROWS = 16
OUT_ROWS = 256
RISE_LIMIT = 100.0
VMEM_LIMIT_BYTES = 56 * 2 ** 20
Q_SCALE = ATTN_SCALE * math.log2(math.e)


def _rms(x, g):
    ms = jnp.mean(x * x, axis=-1, keepdims=True)
    return x * lax.rsqrt(ms + EPS) * g


def _silu(x):
    hx = (0.5 * x).astype(BF16)
    return hx + hx * jnp.tanh(hx)


def _dot(a, b):
    return jnp.dot(a, b, preferred_element_type=F32)


def _dot_nt(a, b):
    return lax.dot_general(a, b, (((1,), (1,)), ((), ())), preferred_element_type=F32)


def _in_proj(x, norm_g, w_head, w_tail):
    u = _rms(x, norm_g).astype(BF16)
    return _dot(u, w_head), _dot(u, w_tail)


def _keys_values(p, cos, sin, kv_g, w_k, w_vt):
    c_kv = _rms(p[:, C_KV:C_KV + KV_LORA], kv_g).astype(BF16)
    k_nope = _dot(c_kv, w_k)
    v_t = _dot_nt(w_vt, c_kv)
    kr = p[:, C_KROPE:C_KROPE + LANES]
    k_pe = kr * cos + pltpu.roll(kr, ROPE, 1) * sin
    lane = lax.broadcasted_iota(jnp.int32, k_pe.shape, 1)
    k_pe = jnp.where(lane == REF_ROW - NOPE, 1.0, k_pe).astype(BF16)
    ks = []
    for h in range(HEADS):
        ks.append(k_nope[:, h * NOPE:(h + 1) * NOPE].astype(BF16))
        ks.append(k_pe)
    return ks, v_t.astype(BF16)


def _meta_kernel(x_ref, cos_ref, sin_ref, norm_g_ref, w_head_ref, w_tail_ref, kv_g_ref, w_k_ref, w_vt_ref,
                 k_out, vt_out, g_out):
    ph, p = _in_proj(x_ref[...], norm_g_ref[...], w_head_ref[...], w_tail_ref[...])
    ks, v_t = _keys_values(ph, cos_ref[...], sin_ref[...], kv_g_ref[...], w_k_ref[...], w_vt_ref[...])
    for i, kk in enumerate(ks):
        k_out[:, i * LANES:(i + 1) * LANES] = kk
    vt_out[...] = v_t
    g_out[...] = p[:, C_CONV_C:C_CONV_C + CONV_W] * p[:, C_CONV_H:C_CONV_H + CONV_W]


def _proj_kernel(x_ref, cos_ref, sin_ref, cos_t_ref, sin_t_ref, ginit_ref, norm_g_ref, w_head_ref, w_tail_ref,
                 q_g_ref, w_qt_ref, kv_g_ref, w_k_ref, w_vt_ref, conv_w_ref, conv_g_ref, gmat_ref,
                 qt_out, k_out, vt_out, ga_out, yc_out, gbuf):
    tm = x_ref.shape[1]

    @pl.when(pl.program_id(1) == 0)
    def _():
        gbuf[...] = ginit_ref[8:16, :]

    ph, p = _in_proj(x_ref[0], norm_g_ref[...], w_head_ref[...], w_tail_ref[...])

    c_q = _rms(ph[:, C_Q:C_Q + Q_LORA], q_g_ref[...] * Q_SCALE).astype(BF16)
    q_t = _dot_nt(w_qt_ref[...], c_q)
    cos_t = cos_t_ref[...]
    sin_t = sin_t_ref[...]
    for h in range(HEADS):
        r = h * HEAD_PAD
        qt_out[0, r:r + NOPE, :] = q_t[r:r + NOPE, :].astype(BF16)
        q_pe = q_t[r + NOPE:r + NOPE + ROPE, :] * cos_t + q_t[r + NOPE + ROPE:r + HEAD_PAD, :] * sin_t
        qt_out[0, r + NOPE:r + NOPE + ROPE, :] = q_pe.astype(BF16)
        qt_out[0, r + NOPE + ROPE:r + HEAD_PAD, :] = jnp.zeros((ROPE, tm), BF16)

    ks, v_t = _keys_values(ph, cos_ref[...], sin_ref[...], kv_g_ref[...], w_k_ref[...], w_vt_ref[...])
    for i, kk in enumerate(ks):
        k_out[0, :, i * LANES:(i + 1) * LANES] = kk
    vt_out[0, 0] = v_t

    ga_out[0] = _silu(p[:, C_Z_ATTN:C_Z_ATTN + HEADS * VDIM]).astype(BF16)

    g = p[:, C_CONV_C:C_CONV_C + CONV_W] * p[:, C_CONV_H:C_CONV_H + CONV_W]
    carry = gbuf[...]
    first = lax.broadcasted_iota(jnp.int32, (8, CONV_W), 0) == 0

    def shift_down(v, row_before):
        r = pltpu.roll(v, 1, 0)
        return jnp.concatenate([jnp.where(first, row_before, r[0:8, :]), r[8:, :]], axis=0)

    g1 = shift_down(g, carry[7:8, :])
    g2 = shift_down(g1, carry[6:7, :])
    gbuf[...] = g[tm - 8:tm, :]
    cw = conv_w_ref[...]
    conv = cw[0:1, :] * g2 + cw[1:2, :] * g1 + cw[2:3, :] * g
    yc = p[:, C_CONV_B:C_CONV_B + CONV_W] * conv
    ssum = _dot((yc * yc).astype(BF16), gmat_ref[...])
    scaled = yc * (conv_g_ref[...] * _silu(p[:, C_Z_CONV:C_Z_CONV + CONV_W]).astype(F32))
    yc_out[0] = (scaled * lax.rsqrt(ssum * (1.0 / CONV_GROUP) + EPS)).astype(BF16)


def _ones_rows(v_t):
    return jnp.concatenate([v_t, jnp.ones((ONES_ROWS, v_t.shape[1]), BF16)], axis=0)


def _bf16_exact(x):
    return x.astype(BF16).astype(F32)


def _attention_tile(c, qt_ref, k_ref, vt_ref, km_ref, vtm_ref, q_aug, m_scr, acc_scr, p_scr, rise_scr):
    tri_full = (lax.broadcasted_iota(jnp.int32, (HALF, TILE), 0)
                <= lax.broadcasted_iota(jnp.int32, (HALF, TILE), 1))
    tri_half = tri_full[:, :HALF]

    def k_of(h, r0, r1):
        return k_ref[0, r0:r1, h * HEAD_PAD:(h + 1) * HEAD_PAD]

    def v_of(h, j, lo, hi):
        return vt_ref[0, j, h * VDIM:(h + 1) * VDIM, lo:hi]

    tasks = []
    for h in range(HEADS):
        tasks.append((h, lambda h=h: km_ref[:, h * HEAD_PAD:(h + 1) * HEAD_PAD],
                      lambda h=h: vtm_ref[h * VDIM:(h + 1) * VDIM, :], 0, TILE, None, True))
    for j in range(c):
        for h in range(HEADS):
            tasks.append((h, lambda h=h, j=j: k_of(h, j * TILE, (j + 1) * TILE),
                          lambda h=h, j=j: v_of(h, j, 0, TILE), 0, TILE, None, False))
    for h in range(HEADS):
        tasks.append((h, lambda h=h: k_of(h, c * TILE, c * TILE + HALF),
                      lambda h=h: v_of(h, c, 0, HALF), 0, TILE, tri_full, False))
    for h in range(HEADS):
        tasks.append((h, lambda h=h: k_of(h, c * TILE + HALF, (c + 1) * TILE),
                      lambda h=h: v_of(h, c, HALF, TILE), HALF, TILE, tri_half, False))
    assert QK_AHEAD <= 2 and all(t[0] == n % HEADS for n, t in enumerate(tasks))

    for h in range(HEADS):
        q_aug[h] = qt_ref[0, h * HEAD_PAD:(h + 1) * HEAD_PAD, :]
    rise_scr[...] = jnp.zeros(rise_scr.shape, F32)

    def scores(task):
        h, k_fn, _, lo, hi, mask, _ = task
        s = _dot(k_fn(), q_aug[h, :, lo:hi])
        return s if mask is None else jnp.where(mask, s, NEG_INF)

    def softmax(n, task, s):
        h, _, _, lo, hi, _, first = task
        nkeys, ncols = s.shape
        blk_max = jnp.max(s, axis=0, keepdims=True)
        if first:
            m_new, beta = _bf16_exact(blk_max), None
            p = jnp.exp2(s - m_new)
        else:
            rise = jnp.maximum(blk_max, 0.0)
            p = jnp.exp2(s)
            m_old = m_scr[h, :, lo:hi]
            m_new = _bf16_exact(m_old + rise)
            beta = jnp.exp2(m_old - m_new)
            rise_scr[:, lo:hi] = jnp.maximum(rise_scr[:, lo:hi], rise)
        p_scr[n % 2, 0:nkeys, 0:ncols] = p.astype(BF16)
        m_scr[h, :, lo:hi] = m_new
        q_aug[h, REF_ROW:REF_ROW + REF_ROWS, lo:hi] = jnp.broadcast_to(-m_new, (REF_ROWS, ncols)).astype(BF16)
        return nkeys, ncols, beta

    def values(n, task, nkeys, ncols, beta):
        h, _, v_fn, lo, hi, _, _ = task
        pv = _dot(_ones_rows(v_fn()), p_scr[n % 2, 0:nkeys, 0:ncols])
        if beta is None:
            acc_scr[h, :, lo:hi] = pv
        else:
            acc_scr[h, :, lo:hi] = (acc_scr[h, :, lo:hi] + pv) * beta

    pending = [scores(t) for t in tasks[:QK_AHEAD]]
    prev = None
    for n, task in enumerate(tasks):
        if prev is not None:
            values(*prev)
        if n + QK_AHEAD < len(tasks):
            pending.append(scores(tasks[n + QK_AHEAD]))
        prev = (n, task) + softmax(n, task, pending.pop(0))
    values(*prev)


def _attention_tile_two_pass(i, qt_ref, k_ref, vt_ref, km_ref, vtm_ref, m_scr, acc_scr):
    tri_full = (lax.broadcasted_iota(jnp.int32, (HALF, TILE), 0)
                <= lax.broadcasted_iota(jnp.int32, (HALF, TILE), 1))
    tri_half = tri_full[:, :HALF]

    def q_of(h, lo, hi):
        return qt_ref[0, h * HEAD_PAD:(h + 1) * HEAD_PAD, lo:hi]

    def update(h, s, v_t, lo, hi):
        m_old = m_scr[h, :, lo:hi]
        m_new = jnp.maximum(m_old, jnp.max(s, axis=0, keepdims=True))
        pv = _dot(_ones_rows(v_t), jnp.exp2(s - m_new).astype(BF16))
        acc_scr[h, :, lo:hi] = jnp.exp2(m_old - m_new) * acc_scr[h, :, lo:hi] + pv
        m_scr[h, :, lo:hi] = m_new

    for h in range(HEADS):
        s = _dot(km_ref[:, h * HEAD_PAD:(h + 1) * HEAD_PAD], q_of(h, 0, TILE))
        m = jnp.max(s, axis=0, keepdims=True)
        m_scr[h] = m
        acc_scr[h] = _dot(_ones_rows(vtm_ref[h * VDIM:(h + 1) * VDIM, :]), jnp.exp2(s - m).astype(BF16))

    def full_block(j, carry):
        start = pl.multiple_of(j * TILE, TILE)
        for h in range(HEADS):
            k = k_ref[0, pl.ds(start, TILE), h * HEAD_PAD:(h + 1) * HEAD_PAD]
            update(h, _dot(k, q_of(h, 0, TILE)), vt_ref[0, j, h * VDIM:(h + 1) * VDIM, :], 0, TILE)
        return carry

    lax.fori_loop(0, i, full_block, 0)

    d0 = pl.multiple_of(i * TILE, TILE)
    d1 = pl.multiple_of(i * TILE + HALF, HALF)
    for h in range(HEADS):
        k = k_ref[0, pl.ds(d0, HALF), h * HEAD_PAD:(h + 1) * HEAD_PAD]
        s = jnp.where(tri_full, _dot(k, q_of(h, 0, TILE)), NEG_INF)
        update(h, s, vt_ref[0, i, h * VDIM:(h + 1) * VDIM, 0:HALF], 0, TILE)
    for h in range(HEADS):
        k = k_ref[0, pl.ds(d1, HALF), h * HEAD_PAD:(h + 1) * HEAD_PAD]
        s = jnp.where(tri_half, _dot(k, q_of(h, HALF, TILE)), NEG_INF)
        update(h, s, vt_ref[0, i, h * VDIM:(h + 1) * VDIM, HALF:TILE], HALF, TILE)


def _finish_tile(acc_scr, ga_ref, yc_ref, x_ref, w_out_ref, attn_g_ref, final_g_ref, o_ref, y_scr):
    for h in range(HEADS):
        a = acc_scr[h, 0:VDIM, :]
        l = acc_scr[h, VDIM:VDIM + 1, :]
        o_t = a * lax.rsqrt(jnp.mean(a * a, axis=0, keepdims=True) + EPS * (l * l))
        gain = attn_g_ref[:, h * VDIM:(h + 1) * VDIM] * ga_ref[0, :, h * VDIM:(h + 1) * VDIM].astype(F32)
        y_scr[:, h * VDIM:(h + 1) * VDIM] = (o_t.T * gain).astype(BF16)

    y_scr[:, HEADS * VDIM:] = yc_ref[0]
    for r0 in range(0, TILE, OUT_ROWS):
        mix = _dot(y_scr[r0:r0 + OUT_ROWS, :], w_out_ref[...])
        o_ref[0, r0:r0 + OUT_ROWS, :] = _rms(x_ref[0, r0:r0 + OUT_ROWS, :] + mix, final_g_ref[...])


def _attn_kernel(qt_ref, k_ref, vt_ref, km_ref, vtm_ref, ga_ref, yc_ref, x_ref, w_out_ref,
                 attn_g_ref, final_g_ref, o_ref, q_aug, m_scr, acc_scr, p_scr, rise_scr, y_scr):
    nblk = k_ref.shape[1] // TILE
    finish = functools.partial(_finish_tile, acc_scr, ga_ref, yc_ref, x_ref, w_out_ref, attn_g_ref,
                               final_g_ref, o_ref, y_scr)
    for c in range(nblk):
        @pl.when(pl.program_id(1) == c)
        def _(c=c):
            _attention_tile(c, qt_ref, k_ref, vt_ref, km_ref, vtm_ref, q_aug, m_scr, acc_scr, p_scr, rise_scr)
            finish()

    @pl.when(jnp.max(rise_scr[...]) > RISE_LIMIT)
    def _():
        _attention_tile_two_pass(pl.program_id(1), qt_ref, k_ref, vt_ref, km_ref, vtm_ref, m_scr, acc_scr)
        finish()


def _rope_angles(pos):
    half = ROPE // 2
    inv_freq = (1.0 / (ROPE_THETA ** (np.arange(half, dtype=np.float32) / half))).astype(np.float32)
    ang = pos.astype(np.float32)[:, None] * inv_freq[None, :]
    return np.cos(ang).astype(np.float32), np.sin(ang).astype(np.float32)


def _swap_halves(w):
    half = w.shape[-1] // 2
    return jnp.concatenate([w[..., half:], w[..., :half]], axis=-1)


def _full(shape):
    return pl.BlockSpec(shape, lambda *_: (0,) * len(shape))


def _layer(x, meta_tokens, norm_g, w_in, q_norm_g, w_q_up, kv_norm_g, w_kv_up, conv_w,
           attn_out_g, conv_out_g, w_out, final_norm_g):
    B, S, D = x.shape
    assert D == D_MODEL and S % TILE == 0
    assert meta_tokens.shape == (N_META, D_MODEL)
    nblk = S // TILE

    w_head = jnp.concatenate([w_in[:, :448], _swap_halves(w_in[:, 384:448])], axis=1).astype(BF16)
    w_tail = jnp.concatenate([w_in[:, 1472:1984], w_in[:, 1984:2496], w_in[:, 960:1472], w_in[:, 2496:3008],
                              w_in[:, 448:960]], axis=1).astype(BF16)
    wq = w_q_up.reshape(Q_LORA, HEADS, NOPE + ROPE)
    wq_t = jnp.concatenate([wq, _swap_halves(wq[..., NOPE:])], axis=-1).reshape(Q_LORA, HEADS * HEAD_PAD).T.astype(BF16)
    wkv = w_kv_up.reshape(KV_LORA, HEADS, NOPE + VDIM)
    w_k = wkv[..., :NOPE].reshape(KV_LORA, HEADS * NOPE).astype(BF16)
    w_vt = wkv[..., NOPE:].reshape(KV_LORA, HEADS * VDIM).T.astype(BF16)
    w_out_b = w_out.astype(BF16)
    gid = np.arange(CONV_W) // CONV_GROUP
    gmat = jnp.asarray((gid[:, None] == gid[None, :]).astype(np.float32), dtype=BF16)
    row = lambda v: v.reshape(1, -1).astype(F32)

    zpad = lambda n: np.zeros((n, LANES - ROPE), np.float32)
    c_m, s_m = _rope_angles(np.arange(N_META))
    cos_m = np.concatenate([c_m, c_m, zpad(N_META)], axis=-1)
    sin_m = np.concatenate([-s_m, s_m, zpad(N_META)], axis=-1)
    c_r, s_r = _rope_angles(N_META + np.arange(S))
    cos_r = np.concatenate([c_r, c_r, zpad(S)], axis=-1)
    sin_r = np.concatenate([-s_r, s_r, zpad(S)], axis=-1)
    cos_t = np.ascontiguousarray(np.concatenate([c_r, c_r], axis=-1).T)
    sin_t = np.ascontiguousarray(np.concatenate([-s_r, s_r], axis=-1).T)

    k_meta, vt_meta, g_meta = pl.pallas_call(
        _meta_kernel,
        out_shape=(jax.ShapeDtypeStruct((N_META, HEADS * HEAD_PAD), BF16),
                   jax.ShapeDtypeStruct((HEADS * VDIM, N_META), BF16),
                   jax.ShapeDtypeStruct((N_META, CONV_W), F32)),
        name="meta_proj",
    )(meta_tokens.astype(F32), cos_m, sin_m, row(norm_g), w_head, w_tail, row(kv_norm_g), w_k, w_vt)

    tok = lambda w: pl.BlockSpec((1, TILE, w), lambda b, t: (b, t, 0))
    qt_s, k_s, vt_s, ga_s, yc_s = pl.pallas_call(
        _proj_kernel,
        grid=(B, nblk),
        in_specs=[tok(D_MODEL),
                  pl.BlockSpec((TILE, LANES), lambda b, t: (t, 0)),
                  pl.BlockSpec((TILE, LANES), lambda b, t: (t, 0)),
                  pl.BlockSpec((ROPE, TILE), lambda b, t: (0, t)),
                  pl.BlockSpec((ROPE, TILE), lambda b, t: (0, t)),
                  _full((N_META, CONV_W)), _full((1, D_MODEL)),
                  _full((D_MODEL, HEAD_COLS)), _full((D_MODEL, TAIL_COLS)),
                  _full((1, Q_LORA)), _full((HEADS * HEAD_PAD, Q_LORA)),
                  _full((1, KV_LORA)), _full((KV_LORA, HEADS * NOPE)), _full((HEADS * VDIM, KV_LORA)),
                  _full((3, CONV_W)), _full((1, CONV_W)), _full((CONV_W, CONV_W))],
        out_specs=[pl.BlockSpec((1, HEADS * HEAD_PAD, TILE), lambda b, t: (b, 0, t)),
                   tok(HEADS * HEAD_PAD),
                   pl.BlockSpec((1, 1, HEADS * VDIM, TILE), lambda b, t: (b, t, 0, 0)),
                   tok(HEADS * VDIM), tok(CONV_W)],
        out_shape=[jax.ShapeDtypeStruct((B, HEADS * HEAD_PAD, S), BF16),
                   jax.ShapeDtypeStruct((B, S, HEADS * HEAD_PAD), BF16),
                   jax.ShapeDtypeStruct((B, nblk, HEADS * VDIM, TILE), BF16),
                   jax.ShapeDtypeStruct((B, S, HEADS * VDIM), BF16),
                   jax.ShapeDtypeStruct((B, S, CONV_W), BF16)],
        scratch_shapes=[pltpu.VMEM((8, CONV_W), F32)],
        compiler_params=pltpu.CompilerParams(dimension_semantics=("arbitrary", "arbitrary"),
                                             vmem_limit_bytes=VMEM_LIMIT_BYTES),
        name="proj",
    )(x, cos_r, sin_r, cos_t, sin_t, g_meta, row(norm_g), w_head, w_tail, row(q_norm_g), wq_t,
      row(kv_norm_g), w_k, w_vt, conv_w.astype(F32), row(conv_out_g), gmat)

    qtile = lambda w: pl.BlockSpec((1, TILE, w), lambda b, i: (b, i, 0))
    out = pl.pallas_call(
        _attn_kernel,
        grid=(B, nblk),
        in_specs=[pl.BlockSpec((1, HEADS * HEAD_PAD, TILE), lambda b, i: (b, 0, i)),
                  pl.BlockSpec((1, S, HEADS * HEAD_PAD), lambda b, i: (b, 0, 0)),
                  pl.BlockSpec((1, nblk, HEADS * VDIM, TILE), lambda b, i: (b, 0, 0, 0)),
                  _full((N_META, HEADS * HEAD_PAD)), _full((HEADS * VDIM, N_META)),
                  qtile(HEADS * VDIM), qtile(CONV_W), qtile(D_MODEL),
                  _full((D_MODEL, D_MODEL)), _full((1, HEADS * VDIM)), _full((1, D_MODEL))],
        out_specs=qtile(D_MODEL),
        out_shape=jax.ShapeDtypeStruct((B, S, D_MODEL), F32),
        scratch_shapes=[pltpu.VMEM((HEADS, HEAD_PAD, TILE), BF16),
                        pltpu.VMEM((HEADS, 1, TILE), F32),
                        pltpu.VMEM((HEADS, VDIM + ONES_ROWS, TILE), F32),
                        pltpu.VMEM((2, TILE, TILE), BF16),
                        pltpu.VMEM((1, TILE), F32),
                        pltpu.VMEM((TILE, D_MODEL), BF16)],
        compiler_params=pltpu.CompilerParams(dimension_semantics=("arbitrary", "arbitrary"),
                                             vmem_limit_bytes=VMEM_LIMIT_BYTES),
        name="attn_out",
    )(qt_s, k_s, vt_s, k_meta, vt_meta, ga_s, yc_s, x, w_out_b, row(attn_out_g), row(final_norm_g))
    return out


def kernel(x, meta_tokens, norm_g, w_in, q_norm_g, w_q_up, kv_norm_g, w_kv_up, conv_w,
           attn_out_g, conv_out_g, w_out, final_norm_g):
    assert norm_g.shape[0] == 1, "single-layer block"
    return _layer(x, meta_tokens, norm_g[0], w_in[0], q_norm_g[0], w_q_up[0], kv_norm_g[0],
                  w_kv_up[0], conv_w[0], attn_out_g[0], conv_out_g[0], w_out[0], final_norm_g)
```

```python
import functools
import math

import jax
import jax.numpy as jnp
import numpy as np
from jax import lax
from jax.experimental import pallas as pl
from jax.experimental.pallas import tpu as pltpu

F32 = jnp.float32
BF16 = jnp.bfloat16

D_MODEL = 1024
N_META = 16
HEADS = 4
NOPE = 128
ROPE = 64
VDIM = 128
Q_LORA = 256
KV_LORA = 128
CONV_W = 512
CONV_GROUP = 64
ROPE_THETA = 10000.0
ATTN_SCALE = (NOPE + ROPE) ** -0.5
NEG_INF = -1e30
EPS = 1e-6

LANES = 128
HEAD_PAD = 2 * LANES
HEAD_COLS, TAIL_COLS = 512, 2560
C_Q, C_KV, C_KROPE = 0, 256, 384
C_Z_CONV, C_CONV_C, C_CONV_H, C_CONV_B, C_Z_ATTN = 0, 512, 1024, 1536, 2048

TILE = 512
HALF = TILE // 2
QK_AHEAD = 2
ONES_ROWS = 16
REF_ROW = NOPE + ROPE
REF_ROWS = 16
OUT_ROWS = 256
RISE_LIMIT = 100.0
VMEM_LIMIT_BYTES = 56 * 2 ** 20
Q_SCALE = ATTN_SCALE * math.log2(math.e)


def _rms(x, g):
    ms = jnp.mean(x * x, axis=-1, keepdims=True)
    return x * lax.rsqrt(ms + EPS) * g


def _silu(x):
    hx = (0.5 * x).astype(BF16)
    return hx + hx * jnp.tanh(hx)


def _dot(a, b):
    return jnp.dot(a, b, preferred_element_type=F32)


def _dot_nt(a, b):
    return lax.dot_general(a, b, (((1,), (1,)), ((), ())), preferred_element_type=F32)


def _in_proj(x, norm_g, w_head, w_tail):
    u = _rms(x, norm_g).astype(BF16)
    return _dot(u, w_head), _dot(u, w_tail)


def _keys_values(p, cos, sin, kv_g, w_k, w_vt):
    c_kv = _rms(p[:, C_KV:C_KV + KV_LORA], kv_g).astype(BF16)
    k_nope = _dot(c_kv, w_k)
    v_t = _dot_nt(w_vt, c_kv)
    kr = p[:, C_KROPE:C_KROPE + LANES]
    k_pe = kr * cos + pltpu.roll(kr, ROPE, 1) * sin
    lane = lax.broadcasted_iota(jnp.int32, k_pe.shape, 1)
    k_pe = jnp.where(lane == REF_ROW - NOPE, 1.0, k_pe).astype(BF16)
    ks = []
    for h in range(HEADS):
        ks.append(k_nope[:, h * NOPE:(h + 1) * NOPE].astype(BF16))
        ks.append(k_pe)
    return ks, v_t.astype(BF16)


def _meta_kernel(x_ref, cos_ref, sin_ref, norm_g_ref, w_head_ref, w_tail_ref, kv_g_ref, w_k_ref, w_vt_ref,
                 k_out, vt_out, g_out):
    ph, p = _in_proj(x_ref[...], norm_g_ref[...], w_head_ref[...], w_tail_ref[...])
    ks, v_t = _keys_values(ph, cos_ref[...], sin_ref[...], kv_g_ref[...], w_k_ref[...], w_vt_ref[...])
    for i, kk in enumerate(ks):
        k_out[:, i * LANES:(i + 1) * LANES] = kk
    vt_out[...] = v_t
    g_out[...] = p[:, C_CONV_C:C_CONV_C + CONV_W] * p[:, C_CONV_H:C_CONV_H + CONV_W]


def _proj_kernel(x_ref, cos_ref, sin_ref, cos_t_ref, sin_t_ref, ginit_ref, norm_g_ref, w_head_ref, w_tail_ref,
                 q_g_ref, w_qt_ref, kv_g_ref, w_k_ref, w_vt_ref, conv_w_ref, conv_g_ref, gmat_ref,
                 qt_out, k_out, vt_out, ga_out, yc_out, gbuf):
    tm = x_ref.shape[1]

    @pl.when(pl.program_id(1) == 0)
    def _():
        gbuf[...] = ginit_ref[8:16, :]

    ph, p = _in_proj(x_ref[0], norm_g_ref[...], w_head_ref[...], w_tail_ref[...])

    c_q = _rms(ph[:, C_Q:C_Q + Q_LORA], q_g_ref[...] * Q_SCALE).astype(BF16)
    q_t = _dot_nt(w_qt_ref[...], c_q)
    cos_t = cos_t_ref[...]
    sin_t = sin_t_ref[...]
    for h in range(HEADS):
        r = h * HEAD_PAD
        qt_out[0, r:r + NOPE, :] = q_t[r:r + NOPE, :].astype(BF16)
        q_pe = q_t[r + NOPE:r + NOPE + ROPE, :] * cos_t + q_t[r + NOPE + ROPE:r + HEAD_PAD, :] * sin_t
        qt_out[0, r + NOPE:r + NOPE + ROPE, :] = q_pe.astype(BF16)
        qt_out[0, r + NOPE + ROPE:r + HEAD_PAD, :] = jnp.zeros((ROPE, tm), BF16)

    ks, v_t = _keys_values(ph, cos_ref[...], sin_ref[...], kv_g_ref[...], w_k_ref[...], w_vt_ref[...])
    for i, kk in enumerate(ks):
        k_out[0, :, i * LANES:(i + 1) * LANES] = kk
    vt_out[0, 0] = v_t

    ga_out[0] = _silu(p[:, C_Z_ATTN:C_Z_ATTN + HEADS * VDIM]).astype(BF16)

    g = p[:, C_CONV_C:C_CONV_C + CONV_W] * p[:, C_CONV_H:C_CONV_H + CONV_W]
    carry = gbuf[...]
    first = lax.broadcasted_iota(jnp.int32, (8, CONV_W), 0) == 0

    def shift_down(v, row_before):
        r = pltpu.roll(v, 1, 0)
        return jnp.concatenate([jnp.where(first, row_before, r[0:8, :]), r[8:, :]], axis=0)

    g1 = shift_down(g, carry[7:8, :])
    g2 = shift_down(g1, carry[6:7, :])
    gbuf[...] = g[tm - 8:tm, :]
    cw = conv_w_ref[...]
    conv = cw[0:1, :] * g2 + cw[1:2, :] * g1 + cw[2:3, :] * g
    yc = p[:, C_CONV_B:C_CONV_B + CONV_W] * conv
    ssum = _dot((yc * yc).astype(BF16), gmat_ref[...])
    scaled = yc * (conv_g_ref[...] * _silu(p[:, C_Z_CONV:C_Z_CONV + CONV_W]).astype(F32))
    yc_out[0] = (scaled * lax.rsqrt(ssum * (1.0 / CONV_GROUP) + EPS)).astype(BF16)


def _ones_rows(v_t):
    return jnp.concatenate([v_t, jnp.ones((ONES_ROWS, v_t.shape[1]), BF16)], axis=0)


def _bf16_exact(x):
    return x.astype(BF16).astype(F32)


def _attention_tile(c, qt_ref, k_ref, vt_ref, km_ref, vtm_ref, q_aug, m_scr, acc_scr, p_scr, rise_scr):
    tri_full = (lax.broadcasted_iota(jnp.int32, (HALF, TILE), 0)
                <= lax.broadcasted_iota(jnp.int32, (HALF, TILE), 1))
    tri_half = tri_full[:, :HALF]

    def k_of(h, r0, r1):
        return k_ref[0, r0:r1, h * HEAD_PAD:(h + 1) * HEAD_PAD]

    def v_of(h, j, lo, hi):
        return vt_ref[0, j, h * VDIM:(h + 1) * VDIM, lo:hi]

    tasks = []
    for h in range(HEADS):
        tasks.append((h, lambda h=h: km_ref[:, h * HEAD_PAD:(h + 1) * HEAD_PAD],
                      lambda h=h: vtm_ref[h * VDIM:(h + 1) * VDIM, :], 0, TILE, None, True))
    for j in range(c):
        for h in range(HEADS):
            tasks.append((h, lambda h=h, j=j: k_of(h, j * TILE, (j + 1) * TILE),
                          lambda h=h, j=j: v_of(h, j, 0, TILE), 0, TILE, None, False))
    for h in range(HEADS):
        tasks.append((h, lambda h=h: k_of(h, c * TILE, c * TILE + HALF),
                      lambda h=h: v_of(h, c, 0, HALF), 0, TILE, tri_full, False))
    for h in range(HEADS):
        tasks.append((h, lambda h=h: k_of(h, c * TILE + HALF, (c + 1) * TILE),
                      lambda h=h: v_of(h, c, HALF, TILE), HALF, TILE, tri_half, False))
    assert QK_AHEAD <= 2 and all(t[0] == n % HEADS for n, t in enumerate(tasks))

    for h in range(HEADS):
        q_aug[h] = qt_ref[0, h * HEAD_PAD:(h + 1) * HEAD_PAD, :]
    rise_scr[...] = jnp.zeros(rise_scr.shape, F32)

    def scores(task):
        h, k_fn, _, lo, hi, mask, _ = task
        s = _dot(k_fn(), q_aug[h, :, lo:hi])
        return s if mask is None else jnp.where(mask, s, NEG_INF)

    def softmax(n, task, s):
        h, _, _, lo, hi, _, first = task
        nkeys, ncols = s.shape
        blk_max = jnp.max(s, axis=0, keepdims=True)
        if first:
            m_new, beta = _bf16_exact(blk_max), None
            p = jnp.exp2(s - m_new)
        else:
            rise = jnp.maximum(blk_max, 0.0)
            p = jnp.exp2(s)
            m_old = m_scr[h, :, lo:hi]
            m_new = _bf16_exact(m_old + rise)
            beta = jnp.exp2(m_old - m_new)
            rise_scr[:, lo:hi] = jnp.maximum(rise_scr[:, lo:hi], rise)
        p_scr[n % 2, 0:nkeys, 0:ncols] = p.astype(BF16)
        m_scr[h, :, lo:hi] = m_new
        q_aug[h, REF_ROW:REF_ROW + REF_ROWS, lo:hi] = jnp.broadcast_to(-m_new, (REF_ROWS, ncols)).astype(BF16)
        return nkeys, ncols, beta

    def values(n, task, nkeys, ncols, beta):
        h, _, v_fn, lo, hi, _, _ = task
        pv = _dot(_ones_rows(v_fn()), p_scr[n % 2, 0:nkeys, 0:ncols])
        if beta is None:
            acc_scr[h, :, lo:hi] = pv
        else:
            acc_scr[h, :, lo:hi] = (acc_scr[h, :, lo:hi] + pv) * beta

    pending = [scores(t) for t in tasks[:QK_AHEAD]]
    prev = None
    for n, task in enumerate(tasks):
        if prev is not None:
            values(*prev)
        if n + QK_AHEAD < len(tasks):
            pending.append(scores(tasks[n + QK_AHEAD]))
        prev = (n, task) + softmax(n, task, pending.pop(0))
    values(*prev)


def _attention_tile_two_pass(i, qt_ref, k_ref, vt_ref, km_ref, vtm_ref, m_scr, acc_scr):
    tri_full = (lax.broadcasted_iota(jnp.int32, (HALF, TILE), 0)
                <= lax.broadcasted_iota(jnp.int32, (HALF, TILE), 1))
    tri_half = tri_full[:, :HALF]

    def q_of(h, lo, hi):
        return qt_ref[0, h * HEAD_PAD:(h + 1) * HEAD_PAD, lo:hi]

    def update(h, s, v_t, lo, hi):
        m_old = m_scr[h, :, lo:hi]
        m_new = jnp.maximum(m_old, jnp.max(s, axis=0, keepdims=True))
        pv = _dot(_ones_rows(v_t), jnp.exp2(s - m_new).astype(BF16))
        acc_scr[h, :, lo:hi] = jnp.exp2(m_old - m_new) * acc_scr[h, :, lo:hi] + pv
        m_scr[h, :, lo:hi] = m_new

    for h in range(HEADS):
        s = _dot(km_ref[:, h * HEAD_PAD:(h + 1) * HEAD_PAD], q_of(h, 0, TILE))
        m = jnp.max(s, axis=0, keepdims=True)
        m_scr[h] = m
        acc_scr[h] = _dot(_ones_rows(vtm_ref[h * VDIM:(h + 1) * VDIM, :]), jnp.exp2(s - m).astype(BF16))

    def full_block(j, carry):
        start = pl.multiple_of(j * TILE, TILE)
        for h in range(HEADS):
            k = k_ref[0, pl.ds(start, TILE), h * HEAD_PAD:(h + 1) * HEAD_PAD]
            update(h, _dot(k, q_of(h, 0, TILE)), vt_ref[0, j, h * VDIM:(h + 1) * VDIM, :], 0, TILE)
        return carry

    lax.fori_loop(0, i, full_block, 0)

    d0 = pl.multiple_of(i * TILE, TILE)
    d1 = pl.multiple_of(i * TILE + HALF, HALF)
    for h in range(HEADS):
        k = k_ref[0, pl.ds(d0, HALF), h * HEAD_PAD:(h + 1) * HEAD_PAD]
        s = jnp.where(tri_full, _dot(k, q_of(h, 0, TILE)), NEG_INF)
        update(h, s, vt_ref[0, i, h * VDIM:(h + 1) * VDIM, 0:HALF], 0, TILE)
    for h in range(HEADS):
        k = k_ref[0, pl.ds(d1, HALF), h * HEAD_PAD:(h + 1) * HEAD_PAD]
        s = jnp.where(tri_half, _dot(k, q_of(h, HALF, TILE)), NEG_INF)
        update(h, s, vt_ref[0, i, h * VDIM:(h + 1) * VDIM, HALF:TILE], HALF, TILE)


def _finish_tile(acc_scr, ga_ref, yc_ref, x_ref, w_out_ref, attn_g_ref, final_g_ref, o_ref, y_scr):
    for h in range(HEADS):
        a = acc_scr[h, 0:VDIM, :]
        l = acc_scr[h, VDIM:VDIM + 1, :]
        o_t = a * lax.rsqrt(jnp.mean(a * a, axis=0, keepdims=True) + EPS * (l * l))
        gain = attn_g_ref[:, h * VDIM:(h + 1) * VDIM] * ga_ref[0, :, h * VDIM:(h + 1) * VDIM].astype(F32)
        y_scr[:, h * VDIM:(h + 1) * VDIM] = (o_t.T * gain).astype(BF16)

    y_scr[:, HEADS * VDIM:] = yc_ref[0]
    for r0 in range(0, TILE, OUT_ROWS):
        mix = _dot(y_scr[r0:r0 + OUT_ROWS, :], w_out_ref[...])
        o_ref[0, r0:r0 + OUT_ROWS, :] = _rms(x_ref[0, r0:r0 + OUT_ROWS, :] + mix, final_g_ref[...])


def _attn_kernel(qt_ref, k_ref, vt_ref, km_ref, vtm_ref, ga_ref, yc_ref, x_ref, w_out_ref,
                 attn_g_ref, final_g_ref, o_ref, q_aug, m_scr, acc_scr, p_scr, rise_scr, y_scr):
    nblk = k_ref.shape[1] // TILE
    finish = functools.partial(_finish_tile, acc_scr, ga_ref, yc_ref, x_ref, w_out_ref, attn_g_ref,
                               final_g_ref, o_ref, y_scr)
    for c in range(nblk):
        @pl.when(pl.program_id(1) == c)
        def _(c=c):
            _attention_tile(c, qt_ref, k_ref, vt_ref, km_ref, vtm_ref, q_aug, m_scr, acc_scr, p_scr, rise_scr)
            finish()

    @pl.when(jnp.max(rise_scr[...]) > RISE_LIMIT)
    def _():
        _attention_tile_two_pass(pl.program_id(1), qt_ref, k_ref, vt_ref, km_ref, vtm_ref, m_scr, acc_scr)
        finish()


def _rope_angles(pos):
    half = ROPE // 2
    inv_freq = (1.0 / (ROPE_THETA ** (np.arange(half, dtype=np.float32) / half))).astype(np.float32)
    ang = pos.astype(np.float32)[:, None] * inv_freq[None, :]
    return np.cos(ang).astype(np.float32), np.sin(ang).astype(np.float32)


def _swap_halves(w):
    half = w.shape[-1] // 2
    return jnp.concatenate([w[..., half:], w[..., :half]], axis=-1)


def _full(shape):
    return pl.BlockSpec(shape, lambda *_: (0,) * len(shape))


def _layer(x, meta_tokens, norm_g, w_in, q_norm_g, w_q_up, kv_norm_g, w_kv_up, conv_w,
           attn_out_g, conv_out_g, w_out, final_norm_g):
    B, S, D = x.shape
    assert D == D_MODEL and S % TILE == 0
    assert meta_tokens.shape == (N_META, D_MODEL)
    nblk = S // TILE

    w_head = jnp.concatenate([w_in[:, :448], _swap_halves(w_in[:, 384:448])], axis=1).astype(BF16)
    w_tail = jnp.concatenate([w_in[:, 2496:3008], w_in[:, 1472:1984], w_in[:, 1984:2496], w_in[:, 960:1472],
                              w_in[:, 448:960]], axis=1).astype(BF16)
    wq = w_q_up.reshape(Q_LORA, HEADS, NOPE + ROPE)
    wq_t = jnp.concatenate([wq, _swap_halves(wq[..., NOPE:])], axis=-1).reshape(Q_LORA, HEADS * HEAD_PAD).T.astype(BF16)
    wkv = w_kv_up.reshape(KV_LORA, HEADS, NOPE + VDIM)
    w_k = wkv[..., :NOPE].reshape(KV_LORA, HEADS * NOPE).astype(BF16)
    w_vt = wkv[..., NOPE:].reshape(KV_LORA, HEADS * VDIM).T.astype(BF16)
    w_out_b = w_out.astype(BF16)
    gid = np.arange(CONV_W) // CONV_GROUP
    gmat = jnp.asarray((gid[:, None] == gid[None, :]).astype(np.float32), dtype=BF16)
    row = lambda v: v.reshape(1, -1).astype(F32)

    zpad = lambda n: np.zeros((n, LANES - ROPE), np.float32)
    c_m, s_m = _rope_angles(np.arange(N_META))
    cos_m = np.concatenate([c_m, c_m, zpad(N_META)], axis=-1)
    sin_m = np.concatenate([-s_m, s_m, zpad(N_META)], axis=-1)
    c_r, s_r = _rope_angles(N_META + np.arange(S))
    cos_r = np.concatenate([c_r, c_r, zpad(S)], axis=-1)
    sin_r = np.concatenate([-s_r, s_r, zpad(S)], axis=-1)
    cos_t = np.ascontiguousarray(np.concatenate([c_r, c_r], axis=-1).T)
    sin_t = np.ascontiguousarray(np.concatenate([-s_r, s_r], axis=-1).T)

    k_meta, vt_meta, g_meta = pl.pallas_call(
        _meta_kernel,
        out_shape=(jax.ShapeDtypeStruct((N_META, HEADS * HEAD_PAD), BF16),
                   jax.ShapeDtypeStruct((HEADS * VDIM, N_META), BF16),
                   jax.ShapeDtypeStruct((N_META, CONV_W), F32)),
        name="meta_proj",
    )(meta_tokens.astype(F32), cos_m, sin_m, row(norm_g), w_head, w_tail, row(kv_norm_g), w_k, w_vt)

    tok = lambda w: pl.BlockSpec((1, TILE, w), lambda b, t: (b, t, 0))
    qt_s, k_s, vt_s, ga_s, yc_s = pl.pallas_call(
        _proj_kernel,
        grid=(B, nblk),
        in_specs=[tok(D_MODEL),
                  pl.BlockSpec((TILE, LANES), lambda b, t: (t, 0)),
                  pl.BlockSpec((TILE, LANES), lambda b, t: (t, 0)),
                  pl.BlockSpec((ROPE, TILE), lambda b, t: (0, t)),
                  pl.BlockSpec((ROPE, TILE), lambda b, t: (0, t)),
                  _full((N_META, CONV_W)), _full((1, D_MODEL)),
                  _full((D_MODEL, HEAD_COLS)), _full((D_MODEL, TAIL_COLS)),
                  _full((1, Q_LORA)), _full((HEADS * HEAD_PAD, Q_LORA)),
                  _full((1, KV_LORA)), _full((KV_LORA, HEADS * NOPE)), _full((HEADS * VDIM, KV_LORA)),
                  _full((3, CONV_W)), _full((1, CONV_W)), _full((CONV_W, CONV_W))],
        out_specs=[pl.BlockSpec((1, HEADS * HEAD_PAD, TILE), lambda b, t: (b, 0, t)),
                   tok(HEADS * HEAD_PAD),
                   pl.BlockSpec((1, 1, HEADS * VDIM, TILE), lambda b, t: (b, t, 0, 0)),
                   tok(HEADS * VDIM), tok(CONV_W)],
        out_shape=[jax.ShapeDtypeStruct((B, HEADS * HEAD_PAD, S), BF16),
                   jax.ShapeDtypeStruct((B, S, HEADS * HEAD_PAD), BF16),
                   jax.ShapeDtypeStruct((B, nblk, HEADS * VDIM, TILE), BF16),
                   jax.ShapeDtypeStruct((B, S, HEADS * VDIM), BF16),
                   jax.ShapeDtypeStruct((B, S, CONV_W), BF16)],
        scratch_shapes=[pltpu.VMEM((8, CONV_W), F32)],
        compiler_params=pltpu.CompilerParams(dimension_semantics=("arbitrary", "arbitrary"),
                                             vmem_limit_bytes=VMEM_LIMIT_BYTES),
        name="proj",
    )(x, cos_r, sin_r, cos_t, sin_t, g_meta, row(norm_g), w_head, w_tail, row(q_norm_g), wq_t,
      row(kv_norm_g), w_k, w_vt, conv_w.astype(F32), row(conv_out_g), gmat)

    qtile = lambda w: pl.BlockSpec((1, TILE, w), lambda b, i: (b, i, 0))
    out = pl.pallas_call(
        _attn_kernel,
        grid=(B, nblk),
        in_specs=[pl.BlockSpec((1, HEADS * HEAD_PAD, TILE), lambda b, i: (b, 0, i)),
                  pl.BlockSpec((1, S, HEADS * HEAD_PAD), lambda b, i: (b, 0, 0)),
                  pl.BlockSpec((1, nblk, HEADS * VDIM, TILE), lambda b, i: (b, 0, 0, 0)),
                  _full((N_META, HEADS * HEAD_PAD)), _full((HEADS * VDIM, N_META)),
                  qtile(HEADS * VDIM), qtile(CONV_W), qtile(D_MODEL),
                  _full((D_MODEL, D_MODEL)), _full((1, HEADS * VDIM)), _full((1, D_MODEL))],
        out_specs=qtile(D_MODEL),
        out_shape=jax.ShapeDtypeStruct((B, S, D_MODEL), F32),
        scratch_shapes=[pltpu.VMEM((HEADS, HEAD_PAD, TILE), BF16),
                        pltpu.VMEM((HEADS, 1, TILE), F32),
                        pltpu.VMEM((HEADS, VDIM + ONES_ROWS, TILE), F32),
                        pltpu.VMEM((2, TILE, TILE), BF16),
                        pltpu.VMEM((1, TILE), F32),
                        pltpu.VMEM((TILE, D_MODEL), BF16)],
        compiler_params=pltpu.CompilerParams(dimension_semantics=("arbitrary", "arbitrary"),
                                             vmem_limit_bytes=VMEM_LIMIT_BYTES),
        name="attn_out",
    )(qt_s, k_s, vt_s, k_meta, vt_meta, ga_s, yc_s, x, w_out_b, row(attn_out_g), row(final_norm_g))
    return out


def kernel(x, meta_tokens, norm_g, w_in, q_norm_g, w_q_up, kv_norm_g, w_kv_up, conv_w,
           attn_out_g, conv_out_g, w_out, final_norm_g):
    assert norm_g.shape[0] == 1, "single-layer block"
    return _layer(x, meta_tokens, norm_g[0], w_in[0], q_norm_g[0], w_q_up[0], kv_norm_g[0],
                  w_kv_up[0], conv_w[0], attn_out_g[0], conv_out_g[0], w_out[0], final_norm_g)
```

```python
import functools
import math

import jax
import jax.numpy as jnp
import numpy as np
from jax import lax
from jax.experimental import pallas as pl
from jax.experimental.pallas import tpu as pltpu

F32 = jnp.float32
BF16 = jnp.bfloat16

D_MODEL = 1024
N_META = 16
HEADS = 4
NOPE = 128
ROPE = 64
VDIM = 128
Q_LORA = 256
KV_LORA = 128
CONV_W = 512
CONV_GROUP = 64
ROPE_THETA = 10000.0
ATTN_SCALE = (NOPE + ROPE) ** -0.5
NEG_INF = -1e30
EPS = 1e-6

LANES = 128
HEAD_PAD = 2 * LANES
HEAD_COLS, TAIL_COLS = 512, 2560
C_Q, C_KV, C_KROPE = 0, 256, 384
C_Z_CONV, C_CONV_C, C_CONV_H, C_CONV_B, C_Z_ATTN = 0, 512, 1024, 1536, 2048

TILE = 512
HALF = TILE // 2
QK_AHEAD = 2
ONES_ROWS = 16
REF_ROW = NOPE + ROPE
REF_ROWS = 16
OUT_ROWS = 256
RISE_LIMIT = 100.0
VMEM_LIMIT_BYTES = 56 * 2 ** 20
Q_SCALE = ATTN_SCALE * math.log2(math.e)


def _rms(x, g):
    ms = jnp.mean(x * x, axis=-1, keepdims=True)
    return x * lax.rsqrt(ms + EPS) * g


def _silu(x):
    hx = (0.5 * x).astype(BF16)
    return hx + hx * jnp.tanh(hx)


def _dot(a, b):
    return jnp.dot(a, b, preferred_element_type=F32)


def _dot_nt(a, b):
    return lax.dot_general(a, b, (((1,), (1,)), ((), ())), preferred_element_type=F32)


def _in_proj(x, norm_g, w_head, w_tail):
    u = _rms(x, norm_g).astype(BF16)
    return _dot(u, w_head), _dot(u, w_tail)


def _keys_values(p, cos, sin, kv_g, w_k, w_vt):
    c_kv = _rms(p[:, C_KV:C_KV + KV_LORA], kv_g).astype(BF16)
    k_nope = _dot(c_kv, w_k)
    v_t = _dot_nt(w_vt, c_kv)
    kr = p[:, C_KROPE:C_KROPE + LANES]
    k_pe = kr * cos + pltpu.roll(kr, ROPE, 1) * sin
    lane = lax.broadcasted_iota(jnp.int32, k_pe.shape, 1)
    k_pe = jnp.where(lane == REF_ROW - NOPE, 1.0, k_pe).astype(BF16)
    ks = []
    for h in range(HEADS):
        ks.append(k_nope[:, h * NOPE:(h + 1) * NOPE].astype(BF16))
        ks.append(k_pe)
    return ks, v_t.astype(BF16)


def _meta_kernel(x_ref, cos_ref, sin_ref, norm_g_ref, w_head_ref, w_tail_ref, kv_g_ref, w_k_ref, w_vt_ref,
                 k_out, vt_out, g_out):
    ph, p = _in_proj(x_ref[...], norm_g_ref[...], w_head_ref[...], w_tail_ref[...])
    ks, v_t = _keys_values(ph, cos_ref[...], sin_ref[...], kv_g_ref[...], w_k_ref[...], w_vt_ref[...])
    for i, kk in enumerate(ks):
        k_out[:, i * LANES:(i + 1) * LANES] = kk
    vt_out[...] = v_t
    g_out[...] = p[:, C_CONV_C:C_CONV_C + CONV_W] * p[:, C_CONV_H:C_CONV_H + CONV_W]


def _proj_kernel(x_ref, cos_ref, sin_ref, cos_t_ref, sin_t_ref, ginit_ref, norm_g_ref, w_head_ref, w_tail_ref,
                 q_g_ref, w_qt_ref, kv_g_ref, w_k_ref, w_vt_ref, conv_w_ref, conv_g_ref, gmat_ref,
                 qt_out, k_out, vt_out, ga_out, yc_out, gbuf):
    tm = x_ref.shape[1]

    @pl.when(pl.program_id(1) == 0)
    def _():
        gbuf[...] = ginit_ref[8:16, :]

    ph, p = _in_proj(x_ref[0], norm_g_ref[...], w_head_ref[...], w_tail_ref[...])

    c_q = _rms(ph[:, C_Q:C_Q + Q_LORA], q_g_ref[...] * Q_SCALE).astype(BF16)
    q_t = _dot_nt(w_qt_ref[...], c_q)
    cos_t = cos_t_ref[...]
    sin_t = sin_t_ref[...]
    for h in range(HEADS):
        r = h * HEAD_PAD
        qt_out[0, r:r + NOPE, :] = q_t[r:r + NOPE, :].astype(BF16)
        q_pe = q_t[r + NOPE:r + NOPE + ROPE, :] * cos_t + q_t[r + NOPE + ROPE:r + HEAD_PAD, :] * sin_t
        qt_out[0, r + NOPE:r + NOPE + ROPE, :] = q_pe.astype(BF16)
        qt_out[0, r + NOPE + ROPE:r + HEAD_PAD, :] = jnp.zeros((ROPE, tm), BF16)

    ks, v_t = _keys_values(ph, cos_ref[...], sin_ref[...], kv_g_ref[...], w_k_ref[...], w_vt_ref[...])
    for i, kk in enumerate(ks):
        k_out[0, :, i * LANES:(i + 1) * LANES] = kk
    vt_out[0, 0] = v_t

    ga_out[0] = _silu(p[:, C_Z_ATTN:C_Z_ATTN + HEADS * VDIM]).astype(BF16)

    g = p[:, C_CONV_C:C_CONV_C + CONV_W] * p[:, C_CONV_H:C_CONV_H + CONV_W]
    carry = gbuf[...]
    first = lax.broadcasted_iota(jnp.int32, (8, CONV_W), 0) == 0

    def shift_down(v, row_before):
        r = pltpu.roll(v, 1, 0)
        return jnp.concatenate([jnp.where(first, row_before, r[0:8, :]), r[8:, :]], axis=0)

    g1 = shift_down(g, carry[7:8, :])
    g2 = shift_down(g1, carry[6:7, :])
    gbuf[...] = g[tm - 8:tm, :]
    cw = conv_w_ref[...]
    conv = cw[0:1, :] * g2 + cw[1:2, :] * g1 + cw[2:3, :] * g
    yc = p[:, C_CONV_B:C_CONV_B + CONV_W] * conv
    ssum = _dot((yc * yc).astype(BF16), gmat_ref[...])
    scaled = yc * (conv_g_ref[...] * _silu(p[:, C_Z_CONV:C_Z_CONV + CONV_W]).astype(F32))
    yc_out[0] = (scaled * lax.rsqrt(ssum * (1.0 / CONV_GROUP) + EPS)).astype(BF16)


def _ones_rows(v_t):
    return jnp.concatenate([v_t, jnp.ones((ONES_ROWS, v_t.shape[1]), BF16)], axis=0)


def _bf16_exact(x):
    return x.astype(BF16).astype(F32)


def _attention_tile(c, qt_ref, k_ref, vt_ref, km_ref, vtm_ref, q_aug, m_scr, acc_scr, p_scr, rise_scr):
    tri_full = (lax.broadcasted_iota(jnp.int32, (HALF, TILE), 0)
                <= lax.broadcasted_iota(jnp.int32, (HALF, TILE), 1))
    tri_half = tri_full[:, :HALF]

    def k_of(h, r0, r1):
        return k_ref[0, r0:r1, h * HEAD_PAD:(h + 1) * HEAD_PAD]

    def v_of(h, j, lo, hi):
        return vt_ref[0, j, h * VDIM:(h + 1) * VDIM, lo:hi]

    tasks = []
    for h in range(HEADS):
        tasks.append((h, lambda h=h: km_ref[:, h * HEAD_PAD:(h + 1) * HEAD_PAD],
                      lambda h=h: vtm_ref[h * VDIM:(h + 1) * VDIM, :], 0, TILE, None, True))
    for j in range(c):
        for h in range(HEADS):
            tasks.append((h, lambda h=h, j=j: k_of(h, j * TILE, (j + 1) * TILE),
                          lambda h=h, j=j: v_of(h, j, 0, TILE), 0, TILE, None, False))
    for h in range(HEADS):
        tasks.append((h, lambda h=h: k_of(h, c * TILE, c * TILE + HALF),
                      lambda h=h: v_of(h, c, 0, HALF), 0, TILE, tri_full, False))
    for h in range(HEADS):
        tasks.append((h, lambda h=h: k_of(h, c * TILE + HALF, (c + 1) * TILE),
                      lambda h=h: v_of(h, c, HALF, TILE), HALF, TILE, tri_half, False))
    assert QK_AHEAD <= 2 and all(t[0] == n % HEADS for n, t in enumerate(tasks))

    for h in range(HEADS):
        q_aug[h] = qt_ref[0, h * HEAD_PAD:(h + 1) * HEAD_PAD, :]
    rise_scr[...] = jnp.zeros(rise_scr.shape, F32)

    def scores(task):
        h, k_fn, _, lo, hi, mask, _ = task
        s = _dot(k_fn(), q_aug[h, :, lo:hi])
        return s if mask is None else jnp.where(mask, s, NEG_INF)

    def softmax(n, task, s):
        h, _, _, lo, hi, _, first = task
        nkeys, ncols = s.shape
        blk_max = jnp.max(s, axis=0, keepdims=True)
        if first:
            m_new, beta = _bf16_exact(blk_max), None
            p = jnp.exp2(s - m_new)
        else:
            rise = jnp.maximum(blk_max, 0.0)
            p = jnp.exp2(s)
            m_old = m_scr[h, :, lo:hi]
            m_new = _bf16_exact(m_old + rise)
            beta = jnp.exp2(m_old - m_new)
            rise_scr[:, lo:hi] = jnp.maximum(rise_scr[:, lo:hi], rise)
        p_scr[n % 2, 0:nkeys, 0:ncols] = p.astype(BF16)
        m_scr[h, :, lo:hi] = m_new
        q_aug[h, REF_ROW:REF_ROW + REF_ROWS, lo:hi] = jnp.broadcast_to(-m_new, (REF_ROWS, ncols)).astype(BF16)
        return nkeys, ncols, beta

    def values(n, task, nkeys, ncols, beta):
        h, _, v_fn, lo, hi, _, _ = task
        pv = _dot(_ones_rows(v_fn()), p_scr[n % 2, 0:nkeys, 0:ncols])
        if beta is None:
            acc_scr[h, :, lo:hi] = pv
        else:
            acc_scr[h, :, lo:hi] = (acc_scr[h, :, lo:hi] + pv) * beta

    pending = [scores(t) for t in tasks[:QK_AHEAD]]
    prev = None
    for n, task in enumerate(tasks):
        if prev is not None:
            values(*prev)
        if n + QK_AHEAD < len(tasks):
            pending.append(scores(tasks[n + QK_AHEAD]))
        prev = (n, task) + softmax(n, task, pending.pop(0))
    values(*prev)


def _attention_tile_two_pass(i, qt_ref, k_ref, vt_ref, km_ref, vtm_ref, m_scr, acc_scr):
    tri_full = (lax.broadcasted_iota(jnp.int32, (HALF, TILE), 0)
                <= lax.broadcasted_iota(jnp.int32, (HALF, TILE), 1))
    tri_half = tri_full[:, :HALF]

    def q_of(h, lo, hi):
        return qt_ref[0, h * HEAD_PAD:(h + 1) * HEAD_PAD, lo:hi]

    def update(h, s, v_t, lo, hi):
        m_old = m_scr[h, :, lo:hi]
        m_new = jnp.maximum(m_old, jnp.max(s, axis=0, keepdims=True))
        pv = _dot(_ones_rows(v_t), jnp.exp2(s - m_new).astype(BF16))
        acc_scr[h, :, lo:hi] = jnp.exp2(m_old - m_new) * acc_scr[h, :, lo:hi] + pv
        m_scr[h, :, lo:hi] = m_new

    for h in range(HEADS):
        s = _dot(km_ref[:, h * HEAD_PAD:(h + 1) * HEAD_PAD], q_of(h, 0, TILE))
        m = jnp.max(s, axis=0, keepdims=True)
        m_scr[h] = m
        acc_scr[h] = _dot(_ones_rows(vtm_ref[h * VDIM:(h + 1) * VDIM, :]), jnp.exp2(s - m).astype(BF16))

    def full_block(j, carry):
        start = pl.multiple_of(j * TILE, TILE)
        for h in range(HEADS):
            k = k_ref[0, pl.ds(start, TILE), h * HEAD_PAD:(h + 1) * HEAD_PAD]
            update(h, _dot(k, q_of(h, 0, TILE)), vt_ref[0, j, h * VDIM:(h + 1) * VDIM, :], 0, TILE)
        return carry

    lax.fori_loop(0, i, full_block, 0)

    d0 = pl.multiple_of(i * TILE, TILE)
    d1 = pl.multiple_of(i * TILE + HALF, HALF)
    for h in range(HEADS):
        k = k_ref[0, pl.ds(d0, HALF), h * HEAD_PAD:(h + 1) * HEAD_PAD]
        s = jnp.where(tri_full, _dot(k, q_of(h, 0, TILE)), NEG_INF)
        update(h, s, vt_ref[0, i, h * VDIM:(h + 1) * VDIM, 0:HALF], 0, TILE)
    for h in range(HEADS):
        k = k_ref[0, pl.ds(d1, HALF), h * HEAD_PAD:(h + 1) * HEAD_PAD]
        s = jnp.where(tri_half, _dot(k, q_of(h, HALF, TILE)), NEG_INF)
        update(h, s, vt_ref[0, i, h * VDIM:(h + 1) * VDIM, HALF:TILE], HALF, TILE)


def _finish_tile(acc_scr, ga_ref, yc_ref, x_ref, w_out_ref, attn_g_ref, final_g_ref, o_ref, y_scr):
    for h in range(HEADS):
        a = acc_scr[h, 0:VDIM, :]
        l = acc_scr[h, VDIM:VDIM + 1, :]
        o_t = a * lax.rsqrt(jnp.mean(a * a, axis=0, keepdims=True) + EPS * (l * l))
        gain = attn_g_ref[:, h * VDIM:(h + 1) * VDIM] * ga_ref[0, :, h * VDIM:(h + 1) * VDIM].astype(F32)
        y_scr[:, h * VDIM:(h + 1) * VDIM] = (o_t.T * gain).astype(BF16)

    y_scr[:, HEADS * VDIM:] = yc_ref[0]
    for r0 in range(0, TILE, OUT_ROWS):
        mix = _dot(y_scr[r0:r0 + OUT_ROWS, :], w_out_ref[...])
        o_ref[0, r0:r0 + OUT_ROWS, :] = _rms(x_ref[0, r0:r0 + OUT_ROWS, :] + mix, final_g_ref[...])


def _attn_kernel(qt_ref, k_ref, vt_ref, km_ref, vtm_ref, ga_ref, yc_ref, x_ref, w_out_ref,
                 attn_g_ref, final_g_ref, o_ref, q_aug, m_scr, acc_scr, p_scr, rise_scr, y_scr):
    nblk = k_ref.shape[1] // TILE
    finish = functools.partial(_finish_tile, acc_scr, ga_ref, yc_ref, x_ref, w_out_ref, attn_g_ref,
                               final_g_ref, o_ref, y_scr)
    for c in range(nblk):
        @pl.when(pl.program_id(1) == c)
        def _(c=c):
            _attention_tile(c, qt_ref, k_ref, vt_ref, km_ref, vtm_ref, q_aug, m_scr, acc_scr, p_scr, rise_scr)
            finish()

    @pl.when(jnp.max(rise_scr[...]) > RISE_LIMIT)
    def _():
        _attention_tile_two_pass(pl.program_id(1), qt_ref, k_ref, vt_ref, km_ref, vtm_ref, m_scr, acc_scr)
        finish()


def _rope_angles(pos):
    half = ROPE // 2
    inv_freq = (1.0 / (ROPE_THETA ** (np.arange(half, dtype=np.float32) / half))).astype(np.float32)
    ang = pos.astype(np.float32)[:, None] * inv_freq[None, :]
    return np.cos(ang).astype(np.float32), np.sin(ang).astype(np.float32)


def _swap_halves(w):
    half = w.shape[-1] // 2
    return jnp.concatenate([w[..., half:], w[..., :half]], axis=-1)


def _full(shape):
    return pl.BlockSpec(shape, lambda *_: (0,) * len(shape))


def _layer(x, meta_tokens, norm_g, w_in, q_norm_g, w_q_up, kv_norm_g, w_kv_up, conv_w,
           attn_out_g, conv_out_g, w_out, final_norm_g):
    B, S, D = x.shape
    assert D == D_MODEL and S % TILE == 0
    assert meta_tokens.shape == (N_META, D_MODEL)
    nblk = S // TILE

    widths = dict(c_q=Q_LORA, c_kv=KV_LORA, k_rope=ROPE, z_attn=HEADS * VDIM, conv_b=CONV_W, conv_c=CONV_W,
                  conv_h=CONV_W, z_conv=CONV_W)
    starts = dict(zip(widths, np.cumsum([0] + list(widths.values())[:-1]).tolist()))
    assert starts["z_conv"] + CONV_W == w_in.shape[1]
    sec = lambda name: w_in[:, starts[name]:starts[name] + widths[name]]
    w_head = jnp.concatenate([sec("c_q"), sec("c_kv"), sec("k_rope"), _swap_halves(sec("k_rope"))],
                             axis=1).astype(BF16)
    w_tail = jnp.concatenate([sec("z_conv"), sec("conv_c"), sec("conv_h"), sec("conv_b"), sec("z_attn")],
                             axis=1).astype(BF16)
    assert w_head.shape[1] == HEAD_COLS and w_tail.shape[1] == TAIL_COLS
    wq = w_q_up.reshape(Q_LORA, HEADS, NOPE + ROPE)
    wq_t = jnp.concatenate([wq, _swap_halves(wq[..., NOPE:])], axis=-1).reshape(Q_LORA, HEADS * HEAD_PAD).T.astype(BF16)
    wkv = w_kv_up.reshape(KV_LORA, HEADS, NOPE + VDIM)
    w_k = wkv[..., :NOPE].reshape(KV_LORA, HEADS * NOPE).astype(BF16)
    w_vt = wkv[..., NOPE:].reshape(KV_LORA, HEADS * VDIM).T.astype(BF16)
    w_out_b = w_out.astype(BF16)
    gid = np.arange(CONV_W) // CONV_GROUP
    gmat = jnp.asarray((gid[:, None] == gid[None, :]).astype(np.float32), dtype=BF16)
    row = lambda v: v.reshape(1, -1).astype(F32)

    zpad = lambda n: np.zeros((n, LANES - ROPE), np.float32)
    c_m, s_m = _rope_angles(np.arange(N_META))
    cos_m = np.concatenate([c_m, c_m, zpad(N_META)], axis=-1)
    sin_m = np.concatenate([-s_m, s_m, zpad(N_META)], axis=-1)
    c_r, s_r = _rope_angles(N_META + np.arange(S))
    cos_r = np.concatenate([c_r, c_r, zpad(S)], axis=-1)
    sin_r = np.concatenate([-s_r, s_r, zpad(S)], axis=-1)
    cos_t = np.ascontiguousarray(np.concatenate([c_r, c_r], axis=-1).T)
    sin_t = np.ascontiguousarray(np.concatenate([-s_r, s_r], axis=-1).T)

    k_meta, vt_meta, g_meta = pl.pallas_call(
        _meta_kernel,
        out_shape=(jax.ShapeDtypeStruct((N_META, HEADS * HEAD_PAD), BF16),
                   jax.ShapeDtypeStruct((HEADS * VDIM, N_META), BF16),
                   jax.ShapeDtypeStruct((N_META, CONV_W), F32)),
        name="meta_proj",
    )(meta_tokens.astype(F32), cos_m, sin_m, row(norm_g), w_head, w_tail, row(kv_norm_g), w_k, w_vt)

    tok = lambda w: pl.BlockSpec((1, TILE, w), lambda b, t: (b, t, 0))
    qt_s, k_s, vt_s, ga_s, yc_s = pl.pallas_call(
        _proj_kernel,
        grid=(B, nblk),
        in_specs=[tok(D_MODEL),
                  pl.BlockSpec((TILE, LANES), lambda b, t: (t, 0)),
                  pl.BlockSpec((TILE, LANES), lambda b, t: (t, 0)),
                  pl.BlockSpec((ROPE, TILE), lambda b, t: (0, t)),
                  pl.BlockSpec((ROPE, TILE), lambda b, t: (0, t)),
                  _full((N_META, CONV_W)), _full((1, D_MODEL)),
                  _full((D_MODEL, HEAD_COLS)), _full((D_MODEL, TAIL_COLS)),
                  _full((1, Q_LORA)), _full((HEADS * HEAD_PAD, Q_LORA)),
                  _full((1, KV_LORA)), _full((KV_LORA, HEADS * NOPE)), _full((HEADS * VDIM, KV_LORA)),
                  _full((3, CONV_W)), _full((1, CONV_W)), _full((CONV_W, CONV_W))],
        out_specs=[pl.BlockSpec((1, HEADS * HEAD_PAD, TILE), lambda b, t: (b, 0, t)),
                   tok(HEADS * HEAD_PAD),
                   pl.BlockSpec((1, 1, HEADS * VDIM, TILE), lambda b, t: (b, t, 0, 0)),
                   tok(HEADS * VDIM), tok(CONV_W)],
        out_shape=[jax.ShapeDtypeStruct((B, HEADS * HEAD_PAD, S), BF16),
                   jax.ShapeDtypeStruct((B, S, HEADS * HEAD_PAD), BF16),
                   jax.ShapeDtypeStruct((B, nblk, HEADS * VDIM, TILE), BF16),
                   jax.ShapeDtypeStruct((B, S, HEADS * VDIM), BF16),
                   jax.ShapeDtypeStruct((B, S, CONV_W), BF16)],
        scratch_shapes=[pltpu.VMEM((8, CONV_W), F32)],
        compiler_params=pltpu.CompilerParams(dimension_semantics=("arbitrary", "arbitrary"),
                                             vmem_limit_bytes=VMEM_LIMIT_BYTES),
        name="proj",
    )(x, cos_r, sin_r, cos_t, sin_t, g_meta, row(norm_g), w_head, w_tail, row(q_norm_g), wq_t,
      row(kv_norm_g), w_k, w_vt, conv_w.astype(F32), row(conv_out_g), gmat)

    qtile = lambda w: pl.BlockSpec((1, TILE, w), lambda b, i: (b, i, 0))
    out = pl.pallas_call(
        _attn_kernel,
        grid=(B, nblk),
        in_specs=[pl.BlockSpec((1, HEADS * HEAD_PAD, TILE), lambda b, i: (b, 0, i)),
                  pl.BlockSpec((1, S, HEADS * HEAD_PAD), lambda b, i: (b, 0, 0)),
                  pl.BlockSpec((1, nblk, HEADS * VDIM, TILE), lambda b, i: (b, 0, 0, 0)),
                  _full((N_META, HEADS * HEAD_PAD)), _full((HEADS * VDIM, N_META)),
                  qtile(HEADS * VDIM), qtile(CONV_W), qtile(D_MODEL),
                  _full((D_MODEL, D_MODEL)), _full((1, HEADS * VDIM)), _full((1, D_MODEL))],
        out_specs=qtile(D_MODEL),
        out_shape=jax.ShapeDtypeStruct((B, S, D_MODEL), F32),
        scratch_shapes=[pltpu.VMEM((HEADS, HEAD_PAD, TILE), BF16),
                        pltpu.VMEM((HEADS, 1, TILE), F32),
                        pltpu.VMEM((HEADS, VDIM + ONES_ROWS, TILE), F32),
                        pltpu.VMEM((2, TILE, TILE), BF16),
                        pltpu.VMEM((1, TILE), F32),
                        pltpu.VMEM((TILE, D_MODEL), BF16)],
        compiler_params=pltpu.CompilerParams(dimension_semantics=("arbitrary", "arbitrary"),
                                             vmem_limit_bytes=VMEM_LIMIT_BYTES),
        name="attn_out",
    )(qt_s, k_s, vt_s, k_meta, vt_meta, ga_s, yc_s, x, w_out_b, row(attn_out_g), row(final_norm_g))
    return out


def kernel(x, meta_tokens, norm_g, w_in, q_norm_g, w_q_up, kv_norm_g, w_kv_up, conv_w,
           attn_out_g, conv_out_g, w_out, final_norm_g):
    assert norm_g.shape[0] == 1, "single-layer block"
    return _layer(x, meta_tokens, norm_g[0], w_in[0], q_norm_g[0], w_q_up[0], kv_norm_g[0],
                  w_kv_up[0], conv_w[0], attn_out_g[0], conv_out_g[0], w_out[0], final_norm_g)
```

```python
import functools
import math

import jax
import jax.numpy as jnp
import numpy as np
from jax import lax
from jax.experimental import pallas as pl
from jax.experimental.pallas import tpu as pltpu

F32 = jnp.float32
BF16 = jnp.bfloat16

D_MODEL = 1024
N_META = 16
HEADS = 4
NOPE = 128
ROPE = 64
VDIM = 128
Q_LORA = 256
KV_LORA = 128
CONV_W = 512
CONV_GROUP = 64
ROPE_THETA = 10000.0
ATTN_SCALE = (NOPE + ROPE) ** -0.5
NEG_INF = -1e30
EPS = 1e-6

LANES = 128
HEAD_PAD = 2 * LANES
HEAD_COLS, TAIL_COLS = 512, 2560
C_Q, C_KV, C_KROPE = 0, 256, 384
C_Z_CONV, C_CONV_C, C_CONV_H, C_CONV_B, C_Z_ATTN = 0, 512, 1024, 1536, 2048

TILE = 512
HALF = TILE // 2
QK_AHEAD = 2
ONES_ROWS = 16
REF_ROW = NOPE + ROPE
REF_ROWS = 16
OUT_ROWS = 256
RISE_LIMIT = 100.0
VMEM_LIMIT_BYTES = 56 * 2 ** 20
Q_SCALE = ATTN_SCALE * math.log2(math.e)


def _rms(x, g):
    ms = jnp.mean(x * x, axis=-1, keepdims=True)
    return x * lax.rsqrt(ms + EPS) * g


def _silu(x):
    hx = (0.5 * x).astype(BF16)
    return hx + hx * jnp.tanh(hx)


def _dot(a, b):
    return jnp.dot(a, b, preferred_element_type=F32)


def _dot_nt(a, b):
    return lax.dot_general(a, b, (((1,), (1,)), ((), ())), preferred_element_type=F32)


def _in_proj(x, norm_g, w_head, w_tail):
    u = _rms(x, norm_g).astype(BF16)
    return _dot(u, w_head), _dot(u, w_tail)


def _keys_values(p, cos, sin, kv_g, w_k, w_vt):
    c_kv = _rms(p[:, C_KV:C_KV + KV_LORA], kv_g).astype(BF16)
    k_nope = _dot(c_kv, w_k)
    v_t = _dot_nt(w_vt, c_kv)
    kr = p[:, C_KROPE:C_KROPE + LANES]
    k_pe = kr * cos + pltpu.roll(kr, ROPE, 1) * sin
    lane = lax.broadcasted_iota(jnp.int32, k_pe.shape, 1)
    k_pe = jnp.where(lane == REF_ROW - NOPE, 1.0, k_pe).astype(BF16)
    ks = []
    for h in range(HEADS):
        ks.append(k_nope[:, h * NOPE:(h + 1) * NOPE].astype(BF16))
        ks.append(k_pe)
    return ks, v_t.astype(BF16)


def _meta_kernel(x_ref, cos_ref, sin_ref, norm_g_ref, w_head_ref, w_tail_ref, kv_g_ref, w_k_ref, w_vt_ref,
                 k_out, vt_out, g_out):
    ph, p = _in_proj(x_ref[...], norm_g_ref[...], w_head_ref[...], w_tail_ref[...])
    ks, v_t = _keys_values(ph, cos_ref[...], sin_ref[...], kv_g_ref[...], w_k_ref[...], w_vt_ref[...])
    for i, kk in enumerate(ks):
        k_out[:, i * LANES:(i + 1) * LANES] = kk
    vt_out[...] = v_t
    g_out[...] = p[:, C_CONV_C:C_CONV_C + CONV_W] * p[:, C_CONV_H:C_CONV_H + CONV_W]


def _proj_kernel(x_ref, cos_ref, sin_ref, cos_t_ref, sin_t_ref, ginit_ref, norm_g_ref, w_head_ref, w_tail_ref,
                 q_g_ref, w_qt_ref, kv_g_ref, w_k_ref, w_vt_ref, conv_w_ref, conv_g_ref, gmat_ref,
                 qt_out, k_out, vt_out, ga_out, yc_out, gbuf):
    tm = x_ref.shape[1]

    @pl.when(pl.program_id(1) == 0)
    def _():
        gbuf[...] = ginit_ref[8:16, :]

    ph, p = _in_proj(x_ref[0], norm_g_ref[...], w_head_ref[...], w_tail_ref[...])

    c_q = _rms(ph[:, C_Q:C_Q + Q_LORA], q_g_ref[...] * Q_SCALE).astype(BF16)
    q_t = _dot_nt(w_qt_ref[...], c_q)
    cos_t = cos_t_ref[...]
    sin_t = sin_t_ref[...]
    for h in range(HEADS):
        r = h * HEAD_PAD
        qt_out[0, r:r + NOPE, :] = q_t[r:r + NOPE, :].astype(BF16)
        q_pe = q_t[r + NOPE:r + NOPE + ROPE, :] * cos_t + q_t[r + NOPE + ROPE:r + HEAD_PAD, :] * sin_t
        qt_out[0, r + NOPE:r + NOPE + ROPE, :] = q_pe.astype(BF16)
        qt_out[0, r + NOPE + ROPE:r + HEAD_PAD, :] = jnp.zeros((ROPE, tm), BF16)

    ks, v_t = _keys_values(ph, cos_ref[...], sin_ref[...], kv_g_ref[...], w_k_ref[...], w_vt_ref[...])
    for i, kk in enumerate(ks):
        k_out[0, :, i * LANES:(i + 1) * LANES] = kk
    vt_out[0, 0] = v_t

    ga_out[0] = _silu(p[:, C_Z_ATTN:C_Z_ATTN + HEADS * VDIM]).astype(BF16)

    g = p[:, C_CONV_C:C_CONV_C + CONV_W] * p[:, C_CONV_H:C_CONV_H + CONV_W]
    carry = gbuf[...]
    first = lax.broadcasted_iota(jnp.int32, (8, CONV_W), 0) == 0

    def shift_down(v, row_before):
        r = pltpu.roll(v, 1, 0)
        return jnp.concatenate([jnp.where(first, row_before, r[0:8, :]), r[8:, :]], axis=0)

    g1 = shift_down(g, carry[7:8, :])
    g2 = shift_down(g1, carry[6:7, :])
    gbuf[...] = g[tm - 8:tm, :]
    cw = conv_w_ref[...]
    conv = cw[0:1, :] * g2 + cw[1:2, :] * g1 + cw[2:3, :] * g
    yc = p[:, C_CONV_B:C_CONV_B + CONV_W] * conv
    ssum = _dot((yc * yc).astype(BF16), gmat_ref[...])
    scaled = yc * (conv_g_ref[...] * _silu(p[:, C_Z_CONV:C_Z_CONV + CONV_W]).astype(F32))
    yc_out[0] = (scaled * lax.rsqrt(ssum * (1.0 / CONV_GROUP) + EPS)).astype(BF16)


def _ones_rows(v_t):
    return jnp.concatenate([v_t, jnp.ones((ONES_ROWS, v_t.shape[1]), BF16)], axis=0)


def _bf16_exact(x):
    return x.astype(BF16).astype(F32)


def _attention_tile(c, qt_ref, k_ref, vt_ref, km_ref, vtm_ref, q_aug, m_scr, acc_scr, p_scr, rise_scr):
    tri_full = (lax.broadcasted_iota(jnp.int32, (HALF, TILE), 0)
                <= lax.broadcasted_iota(jnp.int32, (HALF, TILE), 1))
    tri_half = tri_full[:, :HALF]

    def k_of(h, r0, r1):
        return k_ref[0, r0:r1, h * HEAD_PAD:(h + 1) * HEAD_PAD]

    def v_of(h, j, lo, hi):
        return vt_ref[0, j, h * VDIM:(h + 1) * VDIM, lo:hi]

    tasks = []
    for h in range(HEADS):
        tasks.append((h, lambda h=h: km_ref[:, h * HEAD_PAD:(h + 1) * HEAD_PAD],
                      lambda h=h: vtm_ref[h * VDIM:(h + 1) * VDIM, :], 0, TILE, None, True))
    for j in range(c):
        for h in range(HEADS):
            tasks.append((h, lambda h=h, j=j: k_of(h, j * TILE, (j + 1) * TILE),
                          lambda h=h, j=j: v_of(h, j, 0, TILE), 0, TILE, None, False))
    for h in range(HEADS):
        tasks.append((h, lambda h=h: k_of(h, c * TILE, c * TILE + HALF),
                      lambda h=h: v_of(h, c, 0, HALF), 0, TILE, tri_full, False))
    for h in range(HEADS):
        tasks.append((h, lambda h=h: k_of(h, c * TILE + HALF, (c + 1) * TILE),
                      lambda h=h: v_of(h, c, HALF, TILE), HALF, TILE, tri_half, False))
    assert QK_AHEAD <= 2 and all(t[0] == n % HEADS for n, t in enumerate(tasks))

    for h in range(HEADS):
        q_aug[h] = qt_ref[0, h * HEAD_PAD:(h + 1) * HEAD_PAD, :]
    rise_scr[...] = jnp.zeros(rise_scr.shape, F32)

    def scores(task):
        h, k_fn, _, lo, hi, mask, _ = task
        s = _dot(k_fn(), q_aug[h, :, lo:hi])
        return s if mask is None else jnp.where(mask, s, NEG_INF)

    def softmax(n, task, s):
        h, _, _, lo, hi, _, first = task
        nkeys, ncols = s.shape
        blk_max = jnp.max(s, axis=0, keepdims=True)
        if first:
            m_new, beta = _bf16_exact(blk_max), None
            p = jnp.exp2(s - m_new)
        else:
            rise = jnp.maximum(blk_max, 0.0)
            p = jnp.exp2(s)
            m_old = m_scr[h, :, lo:hi]
            m_new = _bf16_exact(m_old + rise)
            beta = jnp.exp2(m_old - m_new)
            rise_scr[:, lo:hi] = jnp.maximum(rise_scr[:, lo:hi], rise)
        p_scr[n % 2, 0:nkeys, 0:ncols] = p.astype(BF16)
        m_scr[h, :, lo:hi] = m_new
        q_aug[h, REF_ROW:REF_ROW + REF_ROWS, lo:hi] = jnp.broadcast_to(-m_new, (REF_ROWS, ncols)).astype(BF16)
        return nkeys, ncols, beta

    def values(n, task, nkeys, ncols, beta):
        h, _, v_fn, lo, hi, _, _ = task
        pv = _dot(_ones_rows(v_fn()), p_scr[n % 2, 0:nkeys, 0:ncols])
        if beta is None:
            acc_scr[h, :, lo:hi] = pv
        else:
            acc_scr[h, :, lo:hi] = (acc_scr[h, :, lo:hi] + pv) * beta

    pending = [scores(t) for t in tasks[:QK_AHEAD]]
    prev = None
    for n, task in enumerate(tasks):
        if prev is not None:
            values(*prev)
        if n + QK_AHEAD < len(tasks):
            pending.append(scores(tasks[n + QK_AHEAD]))
        prev = (n, task) + softmax(n, task, pending.pop(0))
    values(*prev)


def _attention_tile_two_pass(i, qt_ref, k_ref, vt_ref, km_ref, vtm_ref, m_scr, acc_scr):
    row = lax.broadcasted_iota(jnp.int32, (TILE, TILE), 0)
    col = lax.broadcasted_iota(jnp.int32, (TILE, TILE), 1)

    def q_of(h):
        return qt_ref[0, h * HEAD_PAD:(h + 1) * HEAD_PAD, :]

    for h in range(HEADS):
        s = _dot(km_ref[:, h * HEAD_PAD:(h + 1) * HEAD_PAD], q_of(h))
        m = jnp.max(s, axis=0, keepdims=True)
        m_scr[h] = m
        acc_scr[h] = _dot(_ones_rows(vtm_ref[h * VDIM:(h + 1) * VDIM, :]), jnp.exp2(s - m).astype(BF16))

    def block(j, carry):
        start = pl.multiple_of(j * TILE, TILE)
        visible = (j - i) * TILE + row <= col
        for h in range(HEADS):
            k = k_ref[0, pl.ds(start, TILE), h * HEAD_PAD:(h + 1) * HEAD_PAD]
            s = jnp.where(visible, _dot(k, q_of(h)), NEG_INF)
            m_old = m_scr[h]
            m_new = jnp.maximum(m_old, jnp.max(s, axis=0, keepdims=True))
            pv = _dot(_ones_rows(vt_ref[0, j, h * VDIM:(h + 1) * VDIM, :]), jnp.exp2(s - m_new).astype(BF16))
            acc_scr[h] = jnp.exp2(m_old - m_new) * acc_scr[h] + pv
            m_scr[h] = m_new
        return carry

    lax.fori_loop(0, i + 1, block, 0)


def _finish_tile(acc_scr, ga_ref, yc_ref, x_ref, w_out_ref, attn_g_ref, final_g_ref, o_ref, y_scr):
    for h in range(HEADS):
        a = acc_scr[h, 0:VDIM, :]
        l = acc_scr[h, VDIM:VDIM + 1, :]
        o_t = a * lax.rsqrt(jnp.mean(a * a, axis=0, keepdims=True) + EPS * (l * l))
        gain = attn_g_ref[:, h * VDIM:(h + 1) * VDIM] * ga_ref[0, :, h * VDIM:(h + 1) * VDIM].astype(F32)
        y_scr[:, h * VDIM:(h + 1) * VDIM] = (o_t.T * gain).astype(BF16)

    y_scr[:, HEADS * VDIM:] = yc_ref[0]
    for r0 in range(0, TILE, OUT_ROWS):
        mix = _dot(y_scr[r0:r0 + OUT_ROWS, :], w_out_ref[...])
        o_ref[0, r0:r0 + OUT_ROWS, :] = _rms(x_ref[0, r0:r0 + OUT_ROWS, :] + mix, final_g_ref[...])


def _attn_kernel(qt_ref, k_ref, vt_ref, km_ref, vtm_ref, ga_ref, yc_ref, x_ref, w_out_ref,
                 attn_g_ref, final_g_ref, o_ref, q_aug, m_scr, acc_scr, p_scr, rise_scr, y_scr):
    nblk = k_ref.shape[1] // TILE
    finish = functools.partial(_finish_tile, acc_scr, ga_ref, yc_ref, x_ref, w_out_ref, attn_g_ref,
                               final_g_ref, o_ref, y_scr)
    for c in range(nblk):
        @pl.when(pl.program_id(1) == c)
        def _(c=c):
            _attention_tile(c, qt_ref, k_ref, vt_ref, km_ref, vtm_ref, q_aug, m_scr, acc_scr, p_scr, rise_scr)
            finish()

    @pl.when(jnp.max(rise_scr[...]) > RISE_LIMIT)
    def _():
        _attention_tile_two_pass(pl.program_id(1), qt_ref, k_ref, vt_ref, km_ref, vtm_ref, m_scr, acc_scr)
        finish()


def _rope_angles(pos):
    half = ROPE // 2
    inv_freq = (1.0 / (ROPE_THETA ** (np.arange(half, dtype=np.float32) / half))).astype(np.float32)
    ang = pos.astype(np.float32)[:, None] * inv_freq[None, :]
    return np.cos(ang).astype(np.float32), np.sin(ang).astype(np.float32)


def _swap_halves(w):
    half = w.shape[-1] // 2
    return jnp.concatenate([w[..., half:], w[..., :half]], axis=-1)


def _full(shape):
    return pl.BlockSpec(shape, lambda *_: (0,) * len(shape))


def _layer(x, meta_tokens, norm_g, w_in, q_norm_g, w_q_up, kv_norm_g, w_kv_up, conv_w,
           attn_out_g, conv_out_g, w_out, final_norm_g):
    B, S, D = x.shape
    assert D == D_MODEL and S % TILE == 0
    assert meta_tokens.shape == (N_META, D_MODEL)
    nblk = S // TILE

    widths = dict(c_q=Q_LORA, c_kv=KV_LORA, k_rope=ROPE, z_attn=HEADS * VDIM, conv_b=CONV_W, conv_c=CONV_W,
                  conv_h=CONV_W, z_conv=CONV_W)
    starts = dict(zip(widths, np.cumsum([0] + list(widths.values())[:-1]).tolist()))
    assert starts["z_conv"] + CONV_W == w_in.shape[1]
    sec = lambda name: w_in[:, starts[name]:starts[name] + widths[name]]
    w_head = jnp.concatenate([sec("c_q"), sec("c_kv"), sec("k_rope"), _swap_halves(sec("k_rope"))],
                             axis=1).astype(BF16)
    w_tail = jnp.concatenate([sec("z_conv"), sec("conv_c"), sec("conv_h"), sec("conv_b"), sec("z_attn")],
                             axis=1).astype(BF16)
    assert w_head.shape[1] == HEAD_COLS and w_tail.shape[1] == TAIL_COLS
    wq = w_q_up.reshape(Q_LORA, HEADS, NOPE + ROPE)
    wq_t = jnp.concatenate([wq, _swap_halves(wq[..., NOPE:])], axis=-1).reshape(Q_LORA, HEADS * HEAD_PAD).T.astype(BF16)
    wkv = w_kv_up.reshape(KV_LORA, HEADS, NOPE + VDIM)
    w_k = wkv[..., :NOPE].reshape(KV_LORA, HEADS * NOPE).astype(BF16)
    w_vt = wkv[..., NOPE:].reshape(KV_LORA, HEADS * VDIM).T.astype(BF16)
    w_out_b = w_out.astype(BF16)
    gid = np.arange(CONV_W) // CONV_GROUP
    gmat = jnp.asarray((gid[:, None] == gid[None, :]).astype(np.float32), dtype=BF16)
    row = lambda v: v.reshape(1, -1).astype(F32)

    zpad = lambda n: np.zeros((n, LANES - ROPE), np.float32)
    c_m, s_m = _rope_angles(np.arange(N_META))
    cos_m = np.concatenate([c_m, c_m, zpad(N_META)], axis=-1)
    sin_m = np.concatenate([-s_m, s_m, zpad(N_META)], axis=-1)
    c_r, s_r = _rope_angles(N_META + np.arange(S))
    cos_r = np.concatenate([c_r, c_r, zpad(S)], axis=-1)
    sin_r = np.concatenate([-s_r, s_r, zpad(S)], axis=-1)
    cos_t = np.ascontiguousarray(np.concatenate([c_r, c_r], axis=-1).T)
    sin_t = np.ascontiguousarray(np.concatenate([-s_r, s_r], axis=-1).T)

    k_meta, vt_meta, g_meta = pl.pallas_call(
        _meta_kernel,
        out_shape=(jax.ShapeDtypeStruct((N_META, HEADS * HEAD_PAD), BF16),
                   jax.ShapeDtypeStruct((HEADS * VDIM, N_META), BF16),
                   jax.ShapeDtypeStruct((N_META, CONV_W), F32)),
        name="meta_proj",
    )(meta_tokens.astype(F32), cos_m, sin_m, row(norm_g), w_head, w_tail, row(kv_norm_g), w_k, w_vt)

    tok = lambda w: pl.BlockSpec((1, TILE, w), lambda b, t: (b, t, 0))
    qt_s, k_s, vt_s, ga_s, yc_s = pl.pallas_call(
        _proj_kernel,
        grid=(B, nblk),
        in_specs=[tok(D_MODEL),
                  pl.BlockSpec((TILE, LANES), lambda b, t: (t, 0)),
                  pl.BlockSpec((TILE, LANES), lambda b, t: (t, 0)),
                  pl.BlockSpec((ROPE, TILE), lambda b, t: (0, t)),
                  pl.BlockSpec((ROPE, TILE), lambda b, t: (0, t)),
                  _full((N_META, CONV_W)), _full((1, D_MODEL)),
                  _full((D_MODEL, HEAD_COLS)), _full((D_MODEL, TAIL_COLS)),
                  _full((1, Q_LORA)), _full((HEADS * HEAD_PAD, Q_LORA)),
                  _full((1, KV_LORA)), _full((KV_LORA, HEADS * NOPE)), _full((HEADS * VDIM, KV_LORA)),
                  _full((3, CONV_W)), _full((1, CONV_W)), _full((CONV_W, CONV_W))],
        out_specs=[pl.BlockSpec((1, HEADS * HEAD_PAD, TILE), lambda b, t: (b, 0, t)),
                   tok(HEADS * HEAD_PAD),
                   pl.BlockSpec((1, 1, HEADS * VDIM, TILE), lambda b, t: (b, t, 0, 0)),
                   tok(HEADS * VDIM), tok(CONV_W)],
        out_shape=[jax.ShapeDtypeStruct((B, HEADS * HEAD_PAD, S), BF16),
                   jax.ShapeDtypeStruct((B, S, HEADS * HEAD_PAD), BF16),
                   jax.ShapeDtypeStruct((B, nblk, HEADS * VDIM, TILE), BF16),
                   jax.ShapeDtypeStruct((B, S, HEADS * VDIM), BF16),
                   jax.ShapeDtypeStruct((B, S, CONV_W), BF16)],
        scratch_shapes=[pltpu.VMEM((8, CONV_W), F32)],
        compiler_params=pltpu.CompilerParams(dimension_semantics=("arbitrary", "arbitrary"),
                                             vmem_limit_bytes=VMEM_LIMIT_BYTES),
        name="proj",
    )(x, cos_r, sin_r, cos_t, sin_t, g_meta, row(norm_g), w_head, w_tail, row(q_norm_g), wq_t,
      row(kv_norm_g), w_k, w_vt, conv_w.astype(F32), row(conv_out_g), gmat)

    qtile = lambda w: pl.BlockSpec((1, TILE, w), lambda b, i: (b, i, 0))
    out = pl.pallas_call(
        _attn_kernel,
        grid=(B, nblk),
        in_specs=[pl.BlockSpec((1, HEADS * HEAD_PAD, TILE), lambda b, i: (b, 0, i)),
                  pl.BlockSpec((1, S, HEADS * HEAD_PAD), lambda b, i: (b, 0, 0)),
                  pl.BlockSpec((1, nblk, HEADS * VDIM, TILE), lambda b, i: (b, 0, 0, 0)),
                  _full((N_META, HEADS * HEAD_PAD)), _full((HEADS * VDIM, N_META)),
                  qtile(HEADS * VDIM), qtile(CONV_W), qtile(D_MODEL),
                  _full((D_MODEL, D_MODEL)), _full((1, HEADS * VDIM)), _full((1, D_MODEL))],
        out_specs=qtile(D_MODEL),
        out_shape=jax.ShapeDtypeStruct((B, S, D_MODEL), F32),
        scratch_shapes=[pltpu.VMEM((HEADS, HEAD_PAD, TILE), BF16),
                        pltpu.VMEM((HEADS, 1, TILE), F32),
                        pltpu.VMEM((HEADS, VDIM + ONES_ROWS, TILE), F32),
                        pltpu.VMEM((2, TILE, TILE), BF16),
                        pltpu.VMEM((1, TILE), F32),
                        pltpu.VMEM((TILE, D_MODEL), BF16)],
        compiler_params=pltpu.CompilerParams(dimension_semantics=("arbitrary", "arbitrary"),
                                             vmem_limit_bytes=VMEM_LIMIT_BYTES),
        name="attn_out",
    )(qt_s, k_s, vt_s, k_meta, vt_meta, ga_s, yc_s, x, w_out_b, row(attn_out_g), row(final_norm_g))
    return out


def kernel(x, meta_tokens, norm_g, w_in, q_norm_g, w_q_up, kv_norm_g, w_kv_up, conv_w,
           attn_out_g, conv_out_g, w_out, final_norm_g):
    assert norm_g.shape[0] == 1, "single-layer block"
    return _layer(x, meta_tokens, norm_g[0], w_in[0], q_norm_g[0], w_q_up[0], kv_norm_g[0],
                  w_kv_up[0], conv_w[0], attn_out_g[0], conv_out_g[0], w_out[0], final_norm_g)
```

```python
import functools
import math

import jax
import jax.numpy as jnp
import numpy as np
from jax import lax
from jax.experimental import pallas as pl
from jax.experimental.pallas import tpu as pltpu

F32 = jnp.float32
BF16 = jnp.bfloat16

D_MODEL = 1024
N_META = 16
HEADS = 4
NOPE = 128
ROPE = 64
VDIM = 128
Q_LORA = 256
KV_LORA = 128
CONV_W = 512
CONV_GROUP = 64
ROPE_THETA = 10000.0
ATTN_SCALE = (NOPE + ROPE) ** -0.5
NEG_INF = -1e30
EPS = 1e-6

LANES = 128
HEAD_PAD = 2 * LANES
HEAD_COLS, TAIL_COLS = 512, 2560
C_Q, C_KV, C_KROPE = 0, 256, 384
C_Z_CONV, C_CONV_C, C_CONV_H, C_CONV_B, C_Z_ATTN = 0, 512, 1024, 1536, 2048

TILE = 512
HALF = TILE // 2
QK_AHEAD = 2
ONES_ROWS = 16
REF_ROW = NOPE + ROPE
REF_ROWS = 16
OUT_ROWS = 256
RISE_LIMIT = 100.0
VMEM_LIMIT_BYTES = 56 * 2 ** 20
Q_SCALE = ATTN_SCALE * math.log2(math.e)


def _rms(x, g):
    ms = jnp.mean(x * x, axis=-1, keepdims=True)
    return x * lax.rsqrt(ms + EPS) * g


def _silu(x):
    hx = (0.5 * x).astype(BF16)
    return hx + hx * jnp.tanh(hx)


def _dot(a, b):
    return jnp.dot(a, b, preferred_element_type=F32)


def _dot_nt(a, b):
    return lax.dot_general(a, b, (((1,), (1,)), ((), ())), preferred_element_type=F32)


def _in_proj(x, norm_g, w_head, w_tail):
    u = _rms(x, norm_g).astype(BF16)
    return _dot(u, w_head), _dot(u, w_tail)


def _keys_values(p, cos, sin, kv_g, w_k, w_vt):
    c_kv = _rms(p[:, C_KV:C_KV + KV_LORA], kv_g).astype(BF16)
    k_nope = _dot(c_kv, w_k)
    v_t = _dot_nt(w_vt, c_kv)
    kr = p[:, C_KROPE:C_KROPE + LANES]
    k_pe = kr * cos + pltpu.roll(kr, ROPE, 1) * sin
    lane = lax.broadcasted_iota(jnp.int32, k_pe.shape, 1)
    k_pe = jnp.where(lane == REF_ROW - NOPE, 1.0, k_pe).astype(BF16)
    ks = []
    for h in range(HEADS):
        ks.append(k_nope[:, h * NOPE:(h + 1) * NOPE].astype(BF16))
        ks.append(k_pe)
    return ks, v_t.astype(BF16)


def _meta_kernel(x_ref, cos_ref, sin_ref, norm_g_ref, w_head_ref, w_tail_ref, kv_g_ref, w_k_ref, w_vt_ref,
                 k_out, vt_out, g_out):
    ph, p = _in_proj(x_ref[...], norm_g_ref[...], w_head_ref[...], w_tail_ref[...])
    ks, v_t = _keys_values(ph, cos_ref[...], sin_ref[...], kv_g_ref[...], w_k_ref[...], w_vt_ref[...])
    for i, kk in enumerate(ks):
        k_out[:, i * LANES:(i + 1) * LANES] = kk
    vt_out[...] = v_t
    g_out[...] = p[:, C_CONV_C:C_CONV_C + CONV_W] * p[:, C_CONV_H:C_CONV_H + CONV_W]


def _proj_kernel(x_ref, cos_ref, sin_ref, cos_t_ref, sin_t_ref, ginit_ref, norm_g_ref, w_head_ref, w_tail_ref,
                 q_g_ref, w_qt_ref, kv_g_ref, w_k_ref, w_vt_ref, conv_w_ref, conv_g_ref, gmat_ref,
                 qt_out, k_out, vt_out, ga_out, yc_out, gbuf):
    tm = x_ref.shape[1]

    @pl.when(pl.program_id(1) == 0)
    def _():
        gbuf[...] = ginit_ref[8:16, :]

    ph, p = _in_proj(x_ref[0], norm_g_ref[...], w_head_ref[...], w_tail_ref[...])

    c_q = _rms(ph[:, C_Q:C_Q + Q_LORA], q_g_ref[...] * Q_SCALE).astype(BF16)
    q_t = _dot_nt(w_qt_ref[...], c_q)
    cos_t = cos_t_ref[...]
    sin_t = sin_t_ref[...]
    for h in range(HEADS):
        r = h * HEAD_PAD
        qt_out[0, r:r + NOPE, :] = q_t[r:r + NOPE, :].astype(BF16)
        q_pe = q_t[r + NOPE:r + NOPE + ROPE, :] * cos_t + q_t[r + NOPE + ROPE:r + HEAD_PAD, :] * sin_t
        qt_out[0, r + NOPE:r + NOPE + ROPE, :] = q_pe.astype(BF16)
        qt_out[0, r + NOPE + ROPE:r + HEAD_PAD, :] = jnp.zeros((ROPE, tm), BF16)

    ks, v_t = _keys_values(ph, cos_ref[...], sin_ref[...], kv_g_ref[...], w_k_ref[...], w_vt_ref[...])
    for i, kk in enumerate(ks):
        k_out[0, :, i * LANES:(i + 1) * LANES] = kk
    vt_out[0, 0] = v_t

    ga_out[0] = _silu(p[:, C_Z_ATTN:C_Z_ATTN + HEADS * VDIM]).astype(BF16)

    g = p[:, C_CONV_C:C_CONV_C + CONV_W] * p[:, C_CONV_H:C_CONV_H + CONV_W]
    carry = gbuf[...]
    first = lax.broadcasted_iota(jnp.int32, (8, CONV_W), 0) == 0

    def shift_down(v, row_before):
        r = pltpu.roll(v, 1, 0)
        return jnp.concatenate([jnp.where(first, row_before, r[0:8, :]), r[8:, :]], axis=0)

    g1 = shift_down(g, carry[7:8, :])
    g2 = shift_down(g1, carry[6:7, :])
    gbuf[...] = g[tm - 8:tm, :]
    cw = conv_w_ref[...]
    conv = cw[0:1, :] * g2 + cw[1:2, :] * g1 + cw[2:3, :] * g
    yc = p[:, C_CONV_B:C_CONV_B + CONV_W] * conv
    ssum = _dot((yc * yc).astype(BF16), gmat_ref[...])
    scaled = yc * (conv_g_ref[...] * _silu(p[:, C_Z_CONV:C_Z_CONV + CONV_W]).astype(F32))
    yc_out[0] = (scaled * lax.rsqrt(ssum * (1.0 / CONV_GROUP) + EPS)).astype(BF16)


def _ones_rows(v_t):
    return jnp.concatenate([v_t, jnp.ones((ONES_ROWS, v_t.shape[1]), BF16)], axis=0)


def _bf16_exact(x):
    return x.astype(BF16).astype(F32)


def _attention_tile(c, qt_ref, k_ref, vt_ref, km_ref, vtm_ref, q_aug, m_scr, acc_scr, p_scr, rise_scr):
    tri_full = (lax.broadcasted_iota(jnp.int32, (HALF, TILE), 0)
                <= lax.broadcasted_iota(jnp.int32, (HALF, TILE), 1))
    tri_half = tri_full[:, :HALF]

    def k_of(h, r0, r1):
        return k_ref[0, r0:r1, h * HEAD_PAD:(h + 1) * HEAD_PAD]

    def v_of(h, j, lo, hi):
        return vt_ref[0, j, h * VDIM:(h + 1) * VDIM, lo:hi]

    tasks = []
    for h in range(HEADS):
        tasks.append((h, lambda h=h: km_ref[:, h * HEAD_PAD:(h + 1) * HEAD_PAD],
                      lambda h=h: vtm_ref[h * VDIM:(h + 1) * VDIM, :], 0, TILE, None, True))
    for j in range(c):
        for h in range(HEADS):
            tasks.append((h, lambda h=h, j=j: k_of(h, j * TILE, (j + 1) * TILE),
                          lambda h=h, j=j: v_of(h, j, 0, TILE), 0, TILE, None, False))
    for h in range(HEADS):
        tasks.append((h, lambda h=h: k_of(h, c * TILE, c * TILE + HALF),
                      lambda h=h: v_of(h, c, 0, HALF), 0, TILE, tri_full, False))
    for h in range(HEADS):
        tasks.append((h, lambda h=h: k_of(h, c * TILE + HALF, (c + 1) * TILE),
                      lambda h=h: v_of(h, c, HALF, TILE), HALF, TILE, tri_half, False))
    assert QK_AHEAD <= 2 and all(t[0] == n % HEADS for n, t in enumerate(tasks))

    for h in range(HEADS):
        q_aug[h] = qt_ref[0, h * HEAD_PAD:(h + 1) * HEAD_PAD, :]
    rise_scr[...] = jnp.zeros(rise_scr.shape, F32)

    def scores(task):
        h, k_fn, _, lo, hi, mask, _ = task
        s = _dot(k_fn(), q_aug[h, :, lo:hi])
        return s if mask is None else jnp.where(mask, s, NEG_INF)

    def softmax(n, task, s):
        h, _, _, lo, hi, _, first = task
        nkeys, ncols = s.shape
        blk_max = jnp.max(s, axis=0, keepdims=True)
        if first:
            m_new, beta = _bf16_exact(blk_max), None
            p = jnp.exp2(s - m_new)
        else:
            rise = jnp.maximum(blk_max, 0.0)
            p = jnp.exp2(s)
            m_old = m_scr[h, :, lo:hi]
            m_new = _bf16_exact(m_old + rise)
            beta = jnp.exp2(m_old - m_new)
            rise_scr[:, lo:hi] = jnp.maximum(rise_scr[:, lo:hi], rise)
        p_scr[n % 2, 0:nkeys, 0:ncols] = p.astype(BF16)
        m_scr[h, :, lo:hi] = m_new
        q_aug[h, REF_ROW:REF_ROW + REF_ROWS, lo:hi] = jnp.broadcast_to(-m_new, (REF_ROWS, ncols)).astype(BF16)
        return nkeys, ncols, beta

    def values(n, task, nkeys, ncols, beta):
        h, _, v_fn, lo, hi, _, _ = task
        pv = _dot(_ones_rows(v_fn()), p_scr[n % 2, 0:nkeys, 0:ncols])
        if beta is None:
            acc_scr[h, :, lo:hi] = pv
        else:
            acc_scr[h, :, lo:hi] = (acc_scr[h, :, lo:hi] + pv) * beta

    pending = [scores(t) for t in tasks[:QK_AHEAD]]
    prev = None
    for n, task in enumerate(tasks):
        if prev is not None:
            values(*prev)
        if n + QK_AHEAD < len(tasks):
            pending.append(scores(tasks[n + QK_AHEAD]))
        prev = (n, task) + softmax(n, task, pending.pop(0))
    values(*prev)


def _attention_tile_two_pass(i, qt_ref, k_ref, vt_ref, km_ref, vtm_ref, m_scr, acc_scr):
    row = lax.broadcasted_iota(jnp.int32, (TILE, TILE), 0)
    col = lax.broadcasted_iota(jnp.int32, (TILE, TILE), 1)

    def q_of(h):
        return qt_ref[0, h * HEAD_PAD:(h + 1) * HEAD_PAD, :]

    for h in range(HEADS):
        s = _dot(km_ref[:, h * HEAD_PAD:(h + 1) * HEAD_PAD], q_of(h))
        m = jnp.max(s, axis=0, keepdims=True)
        m_scr[h] = m
        acc_scr[h] = _dot(_ones_rows(vtm_ref[h * VDIM:(h + 1) * VDIM, :]), jnp.exp2(s - m).astype(BF16))

    def block(j, carry):
        start = pl.multiple_of(j * TILE, TILE)
        visible = (j - i) * TILE + row <= col
        for h in range(HEADS):
            k = k_ref[0, pl.ds(start, TILE), h * HEAD_PAD:(h + 1) * HEAD_PAD]
            s = jnp.where(visible, _dot(k, q_of(h)), NEG_INF)
            m_old = m_scr[h]
            m_new = jnp.maximum(m_old, jnp.max(s, axis=0, keepdims=True))
            pv = _dot(_ones_rows(vt_ref[0, j, h * VDIM:(h + 1) * VDIM, :]), jnp.exp2(s - m_new).astype(BF16))
            acc_scr[h] = jnp.exp2(m_old - m_new) * acc_scr[h] + pv
            m_scr[h] = m_new
        return carry

    lax.fori_loop(0, i + 1, block, 0)


def _finish_tile(acc_scr, ga_ref, yc_ref, x_ref, w_out_ref, attn_g_ref, final_g_ref, o_ref, y_scr):
    for h in range(HEADS):
        a = acc_scr[h, 0:VDIM, :]
        l = acc_scr[h, VDIM:VDIM + 1, :]
        o_t = a * lax.rsqrt(jnp.mean(a * a, axis=0, keepdims=True) + EPS * (l * l))
        gain = attn_g_ref[:, h * VDIM:(h + 1) * VDIM] * ga_ref[0, :, h * VDIM:(h + 1) * VDIM].astype(F32)
        y_scr[:, h * VDIM:(h + 1) * VDIM] = (o_t.T * gain).astype(BF16)

    y_scr[:, HEADS * VDIM:] = yc_ref[0]
    for r0 in range(0, TILE, OUT_ROWS):
        mix = _dot(y_scr[r0:r0 + OUT_ROWS, :], w_out_ref[...])
        o_ref[0, r0:r0 + OUT_ROWS, :] = _rms(x_ref[0, r0:r0 + OUT_ROWS, :] + mix, final_g_ref[...])


def _attn_kernel(qt_ref, k_ref, vt_ref, km_ref, vtm_ref, ga_ref, yc_ref, x_ref, w_out_ref,
                 attn_g_ref, final_g_ref, o_ref, q_aug, m_scr, acc_scr, p_scr, rise_scr, y_scr, redo_flag):
    nblk = k_ref.shape[1] // TILE
    finish = functools.partial(_finish_tile, acc_scr, ga_ref, yc_ref, x_ref, w_out_ref, attn_g_ref,
                               final_g_ref, o_ref, y_scr)
    for c in range(nblk):
        @pl.when(pl.program_id(1) == c)
        def _(c=c):
            _attention_tile(c, qt_ref, k_ref, vt_ref, km_ref, vtm_ref, q_aug, m_scr, acc_scr, p_scr, rise_scr)
            redo_flag[0] = (jnp.max(rise_scr[...]) > RISE_LIMIT).astype(jnp.int32)
            finish()

    @pl.when(redo_flag[0] != 0)
    def _():
        _attention_tile_two_pass(pl.program_id(1), qt_ref, k_ref, vt_ref, km_ref, vtm_ref, m_scr, acc_scr)
        finish()


def _rope_angles(pos):
    half = ROPE // 2
    inv_freq = (1.0 / (ROPE_THETA ** (np.arange(half, dtype=np.float32) / half))).astype(np.float32)
    ang = pos.astype(np.float32)[:, None] * inv_freq[None, :]
    return np.cos(ang).astype(np.float32), np.sin(ang).astype(np.float32)


def _swap_halves(w):
    half = w.shape[-1] // 2
    return jnp.concatenate([w[..., half:], w[..., :half]], axis=-1)


def _full(shape):
    return pl.BlockSpec(shape, lambda *_: (0,) * len(shape))


def _layer(x, meta_tokens, norm_g, w_in, q_norm_g, w_q_up, kv_norm_g, w_kv_up, conv_w,
           attn_out_g, conv_out_g, w_out, final_norm_g):
    B, S, D = x.shape
    assert D == D_MODEL and S % TILE == 0
    assert meta_tokens.shape == (N_META, D_MODEL)
    nblk = S // TILE

    widths = dict(c_q=Q_LORA, c_kv=KV_LORA, k_rope=ROPE, z_attn=HEADS * VDIM, conv_b=CONV_W, conv_c=CONV_W,
                  conv_h=CONV_W, z_conv=CONV_W)
    starts = dict(zip(widths, np.cumsum([0] + list(widths.values())[:-1]).tolist()))
    assert starts["z_conv"] + CONV_W == w_in.shape[1]
    sec = lambda name: w_in[:, starts[name]:starts[name] + widths[name]]
    w_head = jnp.concatenate([sec("c_q"), sec("c_kv"), sec("k_rope"), _swap_halves(sec("k_rope"))],
                             axis=1).astype(BF16)
    w_tail = jnp.concatenate([sec("z_conv"), sec("conv_c"), sec("conv_h"), sec("conv_b"), sec("z_attn")],
                             axis=1).astype(BF16)
    assert w_head.shape[1] == HEAD_COLS and w_tail.shape[1] == TAIL_COLS
    wq = w_q_up.reshape(Q_LORA, HEADS, NOPE + ROPE)
    wq_t = jnp.concatenate([wq, _swap_halves(wq[..., NOPE:])], axis=-1).reshape(Q_LORA, HEADS * HEAD_PAD).T.astype(BF16)
    wkv = w_kv_up.reshape(KV_LORA, HEADS, NOPE + VDIM)
    w_k = wkv[..., :NOPE].reshape(KV_LORA, HEADS * NOPE).astype(BF16)
    w_vt = wkv[..., NOPE:].reshape(KV_LORA, HEADS * VDIM).T.astype(BF16)
    w_out_b = w_out.astype(BF16)
    gid = np.arange(CONV_W) // CONV_GROUP
    gmat = jnp.asarray((gid[:, None] == gid[None, :]).astype(np.float32), dtype=BF16)
    row = lambda v: v.reshape(1, -1).astype(F32)

    zpad = lambda n: np.zeros((n, LANES - ROPE), np.float32)
    c_m, s_m = _rope_angles(np.arange(N_META))
    cos_m = np.concatenate([c_m, c_m, zpad(N_META)], axis=-1)
    sin_m = np.concatenate([-s_m, s_m, zpad(N_META)], axis=-1)
    c_r, s_r = _rope_angles(N_META + np.arange(S))
    cos_r = np.concatenate([c_r, c_r, zpad(S)], axis=-1)
    sin_r = np.concatenate([-s_r, s_r, zpad(S)], axis=-1)
    cos_t = np.ascontiguousarray(np.concatenate([c_r, c_r], axis=-1).T)
    sin_t = np.ascontiguousarray(np.concatenate([-s_r, s_r], axis=-1).T)

    k_meta, vt_meta, g_meta = pl.pallas_call(
        _meta_kernel,
        out_shape=(jax.ShapeDtypeStruct((N_META, HEADS * HEAD_PAD), BF16),
                   jax.ShapeDtypeStruct((HEADS * VDIM, N_META), BF16),
                   jax.ShapeDtypeStruct((N_META, CONV_W), F32)),
        name="meta_proj",
    )(meta_tokens.astype(F32), cos_m, sin_m, row(norm_g), w_head, w_tail, row(kv_norm_g), w_k, w_vt)

    tok = lambda w: pl.BlockSpec((1, TILE, w), lambda b, t: (b, t, 0))
    qt_s, k_s, vt_s, ga_s, yc_s = pl.pallas_call(
        _proj_kernel,
        grid=(B, nblk),
        in_specs=[tok(D_MODEL),
                  pl.BlockSpec((TILE, LANES), lambda b, t: (t, 0)),
                  pl.BlockSpec((TILE, LANES), lambda b, t: (t, 0)),
                  pl.BlockSpec((ROPE, TILE), lambda b, t: (0, t)),
                  pl.BlockSpec((ROPE, TILE), lambda b, t: (0, t)),
                  _full((N_META, CONV_W)), _full((1, D_MODEL)),
                  _full((D_MODEL, HEAD_COLS)), _full((D_MODEL, TAIL_COLS)),
                  _full((1, Q_LORA)), _full((HEADS * HEAD_PAD, Q_LORA)),
                  _full((1, KV_LORA)), _full((KV_LORA, HEADS * NOPE)), _full((HEADS * VDIM, KV_LORA)),
                  _full((3, CONV_W)), _full((1, CONV_W)), _full((CONV_W, CONV_W))],
        out_specs=[pl.BlockSpec((1, HEADS * HEAD_PAD, TILE), lambda b, t: (b, 0, t)),
                   tok(HEADS * HEAD_PAD),
                   pl.BlockSpec((1, 1, HEADS * VDIM, TILE), lambda b, t: (b, t, 0, 0)),
                   tok(HEADS * VDIM), tok(CONV_W)],
        out_shape=[jax.ShapeDtypeStruct((B, HEADS * HEAD_PAD, S), BF16),
                   jax.ShapeDtypeStruct((B, S, HEADS * HEAD_PAD), BF16),
                   jax.ShapeDtypeStruct((B, nblk, HEADS * VDIM, TILE), BF16),
                   jax.ShapeDtypeStruct((B, S, HEADS * VDIM), BF16),
                   jax.ShapeDtypeStruct((B, S, CONV_W), BF16)],
        scratch_shapes=[pltpu.VMEM((8, CONV_W), F32)],
        compiler_params=pltpu.CompilerParams(dimension_semantics=("arbitrary", "arbitrary"),
                                             vmem_limit_bytes=VMEM_LIMIT_BYTES),
        name="proj",
    )(x, cos_r, sin_r, cos_t, sin_t, g_meta, row(norm_g), w_head, w_tail, row(q_norm_g), wq_t,
      row(kv_norm_g), w_k, w_vt, conv_w.astype(F32), row(conv_out_g), gmat)

    qtile = lambda w: pl.BlockSpec((1, TILE, w), lambda b, i: (b, i, 0))
    out = pl.pallas_call(
        _attn_kernel,
        grid=(B, nblk),
        in_specs=[pl.BlockSpec((1, HEADS * HEAD_PAD, TILE), lambda b, i: (b, 0, i)),
                  pl.BlockSpec((1, S, HEADS * HEAD_PAD), lambda b, i: (b, 0, 0)),
                  pl.BlockSpec((1, nblk, HEADS * VDIM, TILE), lambda b, i: (b, 0, 0, 0)),
                  _full((N_META, HEADS * HEAD_PAD)), _full((HEADS * VDIM, N_META)),
                  qtile(HEADS * VDIM), qtile(CONV_W), qtile(D_MODEL),
                  _full((D_MODEL, D_MODEL)), _full((1, HEADS * VDIM)), _full((1, D_MODEL))],
        out_specs=qtile(D_MODEL),
        out_shape=jax.ShapeDtypeStruct((B, S, D_MODEL), F32),
        scratch_shapes=[pltpu.VMEM((HEADS, HEAD_PAD, TILE), BF16),
                        pltpu.VMEM((HEADS, 1, TILE), F32),
                        pltpu.VMEM((HEADS, VDIM + ONES_ROWS, TILE), F32),
                        pltpu.VMEM((2, TILE, TILE), BF16),
                        pltpu.VMEM((1, TILE), F32),
                        pltpu.VMEM((TILE, D_MODEL), BF16),
                        pltpu.SMEM((1,), jnp.int32)],
        compiler_params=pltpu.CompilerParams(dimension_semantics=("arbitrary", "arbitrary"),
                                             vmem_limit_bytes=VMEM_LIMIT_BYTES),
        name="attn_out",
    )(qt_s, k_s, vt_s, k_meta, vt_meta, ga_s, yc_s, x, w_out_b, row(attn_out_g), row(final_norm_g))
    return out


def kernel(x, meta_tokens, norm_g, w_in, q_norm_g, w_q_up, kv_norm_g, w_kv_up, conv_w,
           attn_out_g, conv_out_g, w_out, final_norm_g):
    assert norm_g.shape[0] == 1, "single-layer block"
    return _layer(x, meta_tokens, norm_g[0], w_in[0], q_norm_g[0], w_q_up[0], kv_norm_g[0],
                  w_kv_up[0], conv_w[0], attn_out_g[0], conv_out_g[0], w_out[0], final_norm_g)
```

```python
import functools
import math

import jax
import jax.numpy as jnp
import numpy as np
from jax import lax
from jax.experimental import pallas as pl
from jax.experimental.pallas import tpu as pltpu

F32 = jnp.float32
BF16 = jnp.bfloat16

D_MODEL = 1024
N_META = 16
HEADS = 4
NOPE = 128
ROPE = 64
VDIM = 128
Q_LORA = 256
KV_LORA = 128
CONV_W = 512
CONV_GROUP = 64
ROPE_THETA = 10000.0
ATTN_SCALE = (NOPE + ROPE) ** -0.5
NEG_INF = -1e30
EPS = 1e-6

LANES = 128
HEAD_PAD = 2 * LANES
HEAD_COLS, TAIL_COLS = 512, 2560
C_Q, C_KV, C_KROPE = 0, 256, 384
C_Z_CONV, C_CONV_C, C_CONV_H, C_CONV_B, C_Z_ATTN = 0, 512, 1024, 1536, 2048

TILE = 512
HALF = TILE // 2
QK_AHEAD = 2
ONES_ROWS = 16
REF_ROW = NOPE + ROPE
REF_ROWS = 16
OUT_ROWS = 256
RISE_LIMIT = 100.0
VMEM_LIMIT_BYTES = 56 * 2 ** 20
Q_SCALE = ATTN_SCALE * math.log2(math.e)


def _rms(x, g):
    ms = jnp.mean(x * x, axis=-1, keepdims=True)
    return x * lax.rsqrt(ms + EPS) * g


def _silu(x):
    hx = (0.5 * x).astype(BF16)
    return hx + hx * jnp.tanh(hx)


def _dot(a, b):
    return jnp.dot(a, b, preferred_element_type=F32)


def _dot_nt(a, b):
    return lax.dot_general(a, b, (((1,), (1,)), ((), ())), preferred_element_type=F32)


def _in_proj(x, norm_g, w_head, w_tail):
    u = _rms(x, norm_g).astype(BF16)
    return _dot(u, w_head), _dot(u, w_tail)


def _keys_values(p, cos, sin, kv_g, w_k, w_vt):
    c_kv = _rms(p[:, C_KV:C_KV + KV_LORA], kv_g).astype(BF16)
    k_nope = _dot(c_kv, w_k)
    v_t = _dot_nt(w_vt, c_kv)
    kr = p[:, C_KROPE:C_KROPE + LANES]
    k_pe = kr * cos + pltpu.roll(kr, ROPE, 1) * sin
    lane = lax.broadcasted_iota(jnp.int32, k_pe.shape, 1)
    k_pe = jnp.where(lane == REF_ROW - NOPE, 1.0, k_pe).astype(BF16)
    ks = []
    for h in range(HEADS):
        ks.append(k_nope[:, h * NOPE:(h + 1) * NOPE].astype(BF16))
        ks.append(k_pe)
    return ks, v_t.astype(BF16)


def _meta_kernel(x_ref, cos_ref, sin_ref, norm_g_ref, w_head_ref, w_tail_ref, kv_g_ref, w_k_ref, w_vt_ref,
                 k_out, vt_out, g_out):
    ph, p = _in_proj(x_ref[...], norm_g_ref[...], w_head_ref[...], w_tail_ref[...])
    ks, v_t = _keys_values(ph, cos_ref[...], sin_ref[...], kv_g_ref[...], w_k_ref[...], w_vt_ref[...])
    for i, kk in enumerate(ks):
        k_out[:, i * LANES:(i + 1) * LANES] = kk
    vt_out[...] = v_t
    g_out[...] = p[:, C_CONV_C:C_CONV_C + CONV_W] * p[:, C_CONV_H:C_CONV_H + CONV_W]


def _proj_kernel(x_ref, cos_ref, sin_ref, cos_t_ref, sin_t_ref, ginit_ref, norm_g_ref, w_head_ref, w_tail_ref,
                 q_g_ref, w_qt_ref, kv_g_ref, w_k_ref, w_vt_ref, conv_w_ref, conv_g_ref, gmat_ref,
                 qt_out, k_out, vt_out, ga_out, yc_out, gbuf):
    tm = x_ref.shape[1]

    @pl.when(pl.program_id(1) == 0)
    def _():
        gbuf[...] = ginit_ref[8:16, :]

    ph, p = _in_proj(x_ref[0], norm_g_ref[...], w_head_ref[...], w_tail_ref[...])

    c_q = _rms(ph[:, C_Q:C_Q + Q_LORA], q_g_ref[...] * Q_SCALE).astype(BF16)
    q_t = _dot_nt(w_qt_ref[...], c_q)
    cos_t = cos_t_ref[...]
    sin_t = sin_t_ref[...]
    for h in range(HEADS):
        r = h * HEAD_PAD
        qt_out[0, r:r + NOPE, :] = q_t[r:r + NOPE, :].astype(BF16)
        q_pe = q_t[r + NOPE:r + NOPE + ROPE, :] * cos_t + q_t[r + NOPE + ROPE:r + HEAD_PAD, :] * sin_t
        qt_out[0, r + NOPE:r + NOPE + ROPE, :] = q_pe.astype(BF16)
        qt_out[0, r + NOPE + ROPE:r + HEAD_PAD, :] = jnp.zeros((ROPE, tm), BF16)

    ks, v_t = _keys_values(ph, cos_ref[...], sin_ref[...], kv_g_ref[...], w_k_ref[...], w_vt_ref[...])
    for i, kk in enumerate(ks):
        k_out[0, :, i * LANES:(i + 1) * LANES] = kk
    vt_out[0, 0] = v_t

    ga_out[0] = _silu(p[:, C_Z_ATTN:C_Z_ATTN + HEADS * VDIM]).astype(BF16)

    g = p[:, C_CONV_C:C_CONV_C + CONV_W] * p[:, C_CONV_H:C_CONV_H + CONV_W]
    carry = gbuf[...]
    first = lax.broadcasted_iota(jnp.int32, (8, CONV_W), 0) == 0

    def shift_down(v, row_before):
        r = pltpu.roll(v, 1, 0)
        return jnp.concatenate([jnp.where(first, row_before, r[0:8, :]), r[8:, :]], axis=0)

    g1 = shift_down(g, carry[7:8, :])
    g2 = shift_down(g1, carry[6:7, :])
    gbuf[...] = g[tm - 8:tm, :]
    cw = conv_w_ref[...]
    conv = cw[0:1, :] * g2 + cw[1:2, :] * g1 + cw[2:3, :] * g
    yc = p[:, C_CONV_B:C_CONV_B + CONV_W] * conv
    ssum = _dot((yc * yc).astype(BF16), gmat_ref[...])
    scaled = yc * (conv_g_ref[...] * _silu(p[:, C_Z_CONV:C_Z_CONV + CONV_W]).astype(F32))
    yc_out[0] = (scaled * lax.rsqrt(ssum * (1.0 / CONV_GROUP) + EPS)).astype(BF16)


def _ones_rows(v_t):
    return jnp.concatenate([v_t, jnp.ones((ONES_ROWS, v_t.shape[1]), BF16)], axis=0)


def _bf16_exact(x):
    return x.astype(BF16).astype(F32)


def _attention_tile(c, qt_ref, k_ref, vt_ref, km_ref, vtm_ref, q_aug, m_scr, acc_scr, p_scr, rise_scr):
    tri_full = (lax.broadcasted_iota(jnp.int32, (HALF, TILE), 0)
                <= lax.broadcasted_iota(jnp.int32, (HALF, TILE), 1))
    tri_half = tri_full[:, :HALF]

    def k_of(h, r0, r1):
        return k_ref[0, r0:r1, h * HEAD_PAD:(h + 1) * HEAD_PAD]

    def v_of(h, j, lo, hi):
        return vt_ref[0, j, h * VDIM:(h + 1) * VDIM, lo:hi]

    tasks = []
    for h in range(HEADS):
        tasks.append((h, lambda h=h: km_ref[:, h * HEAD_PAD:(h + 1) * HEAD_PAD],
                      lambda h=h: vtm_ref[h * VDIM:(h + 1) * VDIM, :], 0, TILE, None, True))
    for j in range(c):
        for h in range(HEADS):
            tasks.append((h, lambda h=h, j=j: k_of(h, j * TILE, (j + 1) * TILE),
                          lambda h=h, j=j: v_of(h, j, 0, TILE), 0, TILE, None, False))
    for h in range(HEADS):
        tasks.append((h, lambda h=h: k_of(h, c * TILE, c * TILE + HALF),
                      lambda h=h: v_of(h, c, 0, HALF), 0, TILE, tri_full, False))
    for h in range(HEADS):
        tasks.append((h, lambda h=h: k_of(h, c * TILE + HALF, (c + 1) * TILE),
                      lambda h=h: v_of(h, c, HALF, TILE), HALF, TILE, tri_half, False))
    assert QK_AHEAD <= 2 and all(t[0] == n % HEADS for n, t in enumerate(tasks))

    for h in range(HEADS):
        q_aug[h] = qt_ref[0, h * HEAD_PAD:(h + 1) * HEAD_PAD, :]
    rise_scr[...] = jnp.zeros(rise_scr.shape, F32)

    def scores(task):
        h, k_fn, _, lo, hi, mask, _ = task
        s = _dot(k_fn(), q_aug[h, :, lo:hi])
        return s if mask is None else jnp.where(mask, s, NEG_INF)

    def softmax(n, task, s):
        h, _, _, lo, hi, _, first = task
        nkeys, ncols = s.shape
        blk_max = jnp.max(s, axis=0, keepdims=True)
        if first:
            m_new, beta = _bf16_exact(blk_max), None
            p = jnp.exp2(s - m_new)
        else:
            rise = jnp.maximum(blk_max, 0.0)
            p = jnp.exp2(s)
            m_old = m_scr[h, :, lo:hi]
            m_new = _bf16_exact(m_old + rise)
            beta = jnp.exp2(m_old - m_new)
            rise_scr[:, lo:hi] = jnp.maximum(rise_scr[:, lo:hi], rise)
        p_scr[n % 2, 0:nkeys, 0:ncols] = p.astype(BF16)
        m_scr[h, :, lo:hi] = m_new
        q_aug[h, REF_ROW:REF_ROW + REF_ROWS, lo:hi] = jnp.broadcast_to(-m_new, (REF_ROWS, ncols)).astype(BF16)
        return nkeys, ncols, beta

    def values(n, task, nkeys, ncols, beta):
        h, _, v_fn, lo, hi, _, _ = task
        pv = _dot(_ones_rows(v_fn()), p_scr[n % 2, 0:nkeys, 0:ncols])
        if beta is None:
            acc_scr[h, :, lo:hi] = pv
        else:
            acc_scr[h, :, lo:hi] = (acc_scr[h, :, lo:hi] + pv) * beta

    pending = [scores(t) for t in tasks[:QK_AHEAD]]
    prev = None
    for n, task in enumerate(tasks):
        if prev is not None:
            values(*prev)
        if n + QK_AHEAD < len(tasks):
            pending.append(scores(tasks[n + QK_AHEAD]))
        prev = (n, task) + softmax(n, task, pending.pop(0))
    values(*prev)


def _attention_tile_two_pass(i, qt_ref, k_ref, vt_ref, km_ref, vtm_ref, m_scr, acc_scr):
    row = lax.broadcasted_iota(jnp.int32, (TILE, TILE), 0)
    col = lax.broadcasted_iota(jnp.int32, (TILE, TILE), 1)

    def q_of(h):
        return qt_ref[0, h * HEAD_PAD:(h + 1) * HEAD_PAD, :]

    for h in range(HEADS):
        s = _dot(km_ref[:, h * HEAD_PAD:(h + 1) * HEAD_PAD], q_of(h))
        m = jnp.max(s, axis=0, keepdims=True)
        m_scr[h] = m
        acc_scr[h] = _dot(_ones_rows(vtm_ref[h * VDIM:(h + 1) * VDIM, :]), jnp.exp2(s - m).astype(BF16))

    def block(j, carry):
        start = pl.multiple_of(j * TILE, TILE)
        visible = (j - i) * TILE + row <= col
        for h in range(HEADS):
            k = k_ref[0, pl.ds(start, TILE), h * HEAD_PAD:(h + 1) * HEAD_PAD]
            s = jnp.where(visible, _dot(k, q_of(h)), NEG_INF)
            m_old = m_scr[h]
            m_new = jnp.maximum(m_old, jnp.max(s, axis=0, keepdims=True))
            pv = _dot(_ones_rows(vt_ref[0, j, h * VDIM:(h + 1) * VDIM, :]), jnp.exp2(s - m_new).astype(BF16))
            acc_scr[h] = jnp.exp2(m_old - m_new) * acc_scr[h] + pv
            m_scr[h] = m_new
        return carry

    lax.fori_loop(0, i + 1, block, 0)


def _finish_tile(acc_scr, ga_ref, yc_ref, x_ref, w_out_ref, attn_g_ref, final_g_ref, o_ref, y_scr):
    for h in range(HEADS):
        a = acc_scr[h, 0:VDIM, :]
        l = acc_scr[h, VDIM:VDIM + 1, :]
        o_t = a * lax.rsqrt(jnp.mean(a * a, axis=0, keepdims=True) + EPS * (l * l))
        gain = attn_g_ref[:, h * VDIM:(h + 1) * VDIM] * ga_ref[0, :, h * VDIM:(h + 1) * VDIM].astype(F32)
        y_scr[:, h * VDIM:(h + 1) * VDIM] = (o_t.T * gain).astype(BF16)

    y_scr[:, HEADS * VDIM:] = yc_ref[0]
    for r0 in range(0, TILE, OUT_ROWS):
        mix = _dot(y_scr[r0:r0 + OUT_ROWS, :], w_out_ref[...])
        o_ref[0, r0:r0 + OUT_ROWS, :] = _rms(x_ref[0, r0:r0 + OUT_ROWS, :] + mix, final_g_ref[...])


def _attn_kernel(qt_ref, k_hbm, vt_hbm, km_ref, vtm_ref, ga_ref, yc_ref, x_ref, w_out_ref,
                 attn_g_ref, final_g_ref, o_ref, q_aug, m_scr, acc_scr, p_scr, rise_scr, y_scr, redo_flag,
                 k_buf, vt_buf, kv_sem):
    nblk = k_buf.shape[1] // TILE
    b = pl.program_id(0)
    nbatch = pl.num_programs(0)
    slot = lax.rem(b, 2)

    def kv_copies(batch, dst_slot):
        return (pltpu.make_async_copy(k_hbm.at[batch], k_buf.at[dst_slot], kv_sem.at[0, dst_slot]),
                pltpu.make_async_copy(vt_hbm.at[batch], vt_buf.at[dst_slot], kv_sem.at[1, dst_slot]))

    @pl.when(jnp.logical_and(pl.program_id(1) == 0, b == 0))
    def _():
        for cp in kv_copies(0, 0):
            cp.start()

    @pl.when(pl.program_id(1) == 0)
    def _():
        for cp in kv_copies(b, slot):
            cp.wait()

    @pl.when(jnp.logical_and(pl.program_id(1) == 0, b + 1 < nbatch))
    def _():
        for cp in kv_copies(b + 1, 1 - slot):
            cp.start()

    k_ref = k_buf.at[pl.ds(slot, 1)]
    vt_ref = vt_buf.at[pl.ds(slot, 1)]
    finish = functools.partial(_finish_tile, acc_scr, ga_ref, yc_ref, x_ref, w_out_ref, attn_g_ref,
                               final_g_ref, o_ref, y_scr)
    for c in range(nblk):
        @pl.when(pl.program_id(1) == c)
        def _(c=c):
            _attention_tile(c, qt_ref, k_ref, vt_ref, km_ref, vtm_ref, q_aug, m_scr, acc_scr, p_scr, rise_scr)
            redo_flag[0] = (jnp.max(rise_scr[...]) > RISE_LIMIT).astype(jnp.int32)
            finish()

    @pl.when(redo_flag[0] != 0)
    def _():
        _attention_tile_two_pass(pl.program_id(1), qt_ref, k_ref, vt_ref, km_ref, vtm_ref, m_scr, acc_scr)
        finish()


def _rope_angles(pos):
    half = ROPE // 2
    inv_freq = (1.0 / (ROPE_THETA ** (np.arange(half, dtype=np.float32) / half))).astype(np.float32)
    ang = pos.astype(np.float32)[:, None] * inv_freq[None, :]
    return np.cos(ang).astype(np.float32), np.sin(ang).astype(np.float32)


def _swap_halves(w):
    half = w.shape[-1] // 2
    return jnp.concatenate([w[..., half:], w[..., :half]], axis=-1)


def _full(shape):
    return pl.BlockSpec(shape, lambda *_: (0,) * len(shape))


def _layer(x, meta_tokens, norm_g, w_in, q_norm_g, w_q_up, kv_norm_g, w_kv_up, conv_w,
           attn_out_g, conv_out_g, w_out, final_norm_g):
    B, S, D = x.shape
    assert D == D_MODEL and S % TILE == 0
    assert meta_tokens.shape == (N_META, D_MODEL)
    nblk = S // TILE

    widths = dict(c_q=Q_LORA, c_kv=KV_LORA, k_rope=ROPE, z_attn=HEADS * VDIM, conv_b=CONV_W, conv_c=CONV_W,
                  conv_h=CONV_W, z_conv=CONV_W)
    starts = dict(zip(widths, np.cumsum([0] + list(widths.values())[:-1]).tolist()))
    assert starts["z_conv"] + CONV_W == w_in.shape[1]
    sec = lambda name: w_in[:, starts[name]:starts[name] + widths[name]]
    w_head = jnp.concatenate([sec("c_q"), sec("c_kv"), sec("k_rope"), _swap_halves(sec("k_rope"))],
                             axis=1).astype(BF16)
    w_tail = jnp.concatenate([sec("z_conv"), sec("conv_c"), sec("conv_h"), sec("conv_b"), sec("z_attn")],
                             axis=1).astype(BF16)
    assert w_head.shape[1] == HEAD_COLS and w_tail.shape[1] == TAIL_COLS
    wq = w_q_up.reshape(Q_LORA, HEADS, NOPE + ROPE)
    wq_t = jnp.concatenate([wq, _swap_halves(wq[..., NOPE:])], axis=-1).reshape(Q_LORA, HEADS * HEAD_PAD).T.astype(BF16)
    wkv = w_kv_up.reshape(KV_LORA, HEADS, NOPE + VDIM)
    w_k = wkv[..., :NOPE].reshape(KV_LORA, HEADS * NOPE).astype(BF16)
    w_vt = wkv[..., NOPE:].reshape(KV_LORA, HEADS * VDIM).T.astype(BF16)
    w_out_b = w_out.astype(BF16)
    gid = np.arange(CONV_W) // CONV_GROUP
    gmat = jnp.asarray((gid[:, None] == gid[None, :]).astype(np.float32), dtype=BF16)
    row = lambda v: v.reshape(1, -1).astype(F32)

    zpad = lambda n: np.zeros((n, LANES - ROPE), np.float32)
    c_m, s_m = _rope_angles(np.arange(N_META))
    cos_m = np.concatenate([c_m, c_m, zpad(N_META)], axis=-1)
    sin_m = np.concatenate([-s_m, s_m, zpad(N_META)], axis=-1)
    c_r, s_r = _rope_angles(N_META + np.arange(S))
    cos_r = np.concatenate([c_r, c_r, zpad(S)], axis=-1)
    sin_r = np.concatenate([-s_r, s_r, zpad(S)], axis=-1)
    cos_t = np.ascontiguousarray(np.concatenate([c_r, c_r], axis=-1).T)
    sin_t = np.ascontiguousarray(np.concatenate([-s_r, s_r], axis=-1).T)

    k_meta, vt_meta, g_meta = pl.pallas_call(
        _meta_kernel,
        out_shape=(jax.ShapeDtypeStruct((N_META, HEADS * HEAD_PAD), BF16),
                   jax.ShapeDtypeStruct((HEADS * VDIM, N_META), BF16),
                   jax.ShapeDtypeStruct((N_META, CONV_W), F32)),
        name="meta_proj",
    )(meta_tokens.astype(F32), cos_m, sin_m, row(norm_g), w_head, w_tail, row(kv_norm_g), w_k, w_vt)

    tok = lambda w: pl.BlockSpec((1, TILE, w), lambda b, t: (b, t, 0))
    qt_s, k_s, vt_s, ga_s, yc_s = pl.pallas_call(
        _proj_kernel,
        grid=(B, nblk),
        in_specs=[tok(D_MODEL),
                  pl.BlockSpec((TILE, LANES), lambda b, t: (t, 0)),
                  pl.BlockSpec((TILE, LANES), lambda b, t: (t, 0)),
                  pl.BlockSpec((ROPE, TILE), lambda b, t: (0, t)),
                  pl.BlockSpec((ROPE, TILE), lambda b, t: (0, t)),
                  _full((N_META, CONV_W)), _full((1, D_MODEL)),
                  _full((D_MODEL, HEAD_COLS)), _full((D_MODEL, TAIL_COLS)),
                  _full((1, Q_LORA)), _full((HEADS * HEAD_PAD, Q_LORA)),
                  _full((1, KV_LORA)), _full((KV_LORA, HEADS * NOPE)), _full((HEADS * VDIM, KV_LORA)),
                  _full((3, CONV_W)), _full((1, CONV_W)), _full((CONV_W, CONV_W))],
        out_specs=[pl.BlockSpec((1, HEADS * HEAD_PAD, TILE), lambda b, t: (b, 0, t)),
                   tok(HEADS * HEAD_PAD),
                   pl.BlockSpec((1, 1, HEADS * VDIM, TILE), lambda b, t: (b, t, 0, 0)),
                   tok(HEADS * VDIM), tok(CONV_W)],
        out_shape=[jax.ShapeDtypeStruct((B, HEADS * HEAD_PAD, S), BF16),
                   jax.ShapeDtypeStruct((B, S, HEADS * HEAD_PAD), BF16),
                   jax.ShapeDtypeStruct((B, nblk, HEADS * VDIM, TILE), BF16),
                   jax.ShapeDtypeStruct((B, S, HEADS * VDIM), BF16),
                   jax.ShapeDtypeStruct((B, S, CONV_W), BF16)],
        scratch_shapes=[pltpu.VMEM((8, CONV_W), F32)],
        compiler_params=pltpu.CompilerParams(dimension_semantics=("arbitrary", "arbitrary"),
                                             vmem_limit_bytes=VMEM_LIMIT_BYTES),
        name="proj",
    )(x, cos_r, sin_r, cos_t, sin_t, g_meta, row(norm_g), w_head, w_tail, row(q_norm_g), wq_t,
      row(kv_norm_g), w_k, w_vt, conv_w.astype(F32), row(conv_out_g), gmat)

    qtile = lambda w: pl.BlockSpec((1, TILE, w), lambda b, i: (b, i, 0))
    out = pl.pallas_call(
        _attn_kernel,
        grid=(B, nblk),
        in_specs=[pl.BlockSpec((1, HEADS * HEAD_PAD, TILE), lambda b, i: (b, 0, i)),
                  pl.BlockSpec(memory_space=pl.ANY), pl.BlockSpec(memory_space=pl.ANY),
                  _full((N_META, HEADS * HEAD_PAD)), _full((HEADS * VDIM, N_META)),
                  qtile(HEADS * VDIM), qtile(CONV_W), qtile(D_MODEL),
                  _full((D_MODEL, D_MODEL)), _full((1, HEADS * VDIM)), _full((1, D_MODEL))],
        out_specs=qtile(D_MODEL),
        out_shape=jax.ShapeDtypeStruct((B, S, D_MODEL), F32),
        scratch_shapes=[pltpu.VMEM((HEADS, HEAD_PAD, TILE), BF16),
                        pltpu.VMEM((HEADS, 1, TILE), F32),
                        pltpu.VMEM((HEADS, VDIM + ONES_ROWS, TILE), F32),
                        pltpu.VMEM((2, TILE, TILE), BF16),
                        pltpu.VMEM((1, TILE), F32),
                        pltpu.VMEM((TILE, D_MODEL), BF16),
                        pltpu.SMEM((1,), jnp.int32),
                        pltpu.VMEM((2, S, HEADS * HEAD_PAD), BF16),
                        pltpu.VMEM((2, nblk, HEADS * VDIM, TILE), BF16),
                        pltpu.SemaphoreType.DMA((2, 2))],
        compiler_params=pltpu.CompilerParams(dimension_semantics=("arbitrary", "arbitrary"),
                                             vmem_limit_bytes=VMEM_LIMIT_BYTES),
        name="attn_out",
    )(qt_s, k_s, vt_s, k_meta, vt_meta, ga_s, yc_s, x, w_out_b, row(attn_out_g), row(final_norm_g))
    return out


def kernel(x, meta_tokens, norm_g, w_in, q_norm_g, w_q_up, kv_norm_g, w_kv_up, conv_w,
           attn_out_g, conv_out_g, w_out, final_norm_g):
    assert norm_g.shape[0] == 1, "single-layer block"
    return _layer(x, meta_tokens, norm_g[0], w_in[0], q_norm_g[0], w_q_up[0], kv_norm_g[0],
                  w_kv_up[0], conv_w[0], attn_out_g[0], conv_out_g[0], w_out[0], final_norm_g)
```

```python
import functools
import math

import jax
import jax.numpy as jnp
import numpy as np
from jax import lax
from jax.experimental import pallas as pl
from jax.experimental.pallas import tpu as pltpu

F32 = jnp.float32
BF16 = jnp.bfloat16

D_MODEL = 1024
N_META = 16
HEADS = 4
NOPE = 128
ROPE = 64
VDIM = 128
Q_LORA = 256
KV_LORA = 128
CONV_W = 512
CONV_GROUP = 64
ROPE_THETA = 10000.0
ATTN_SCALE = (NOPE + ROPE) ** -0.5
NEG_INF = -1e30
EPS = 1e-6

LANES = 128
HEAD_PAD = 2 * LANES
HEAD_COLS, TAIL_COLS = 512, 2560
C_Q, C_KV, C_KROPE = 0, 256, 384
C_Z_CONV, C_CONV_C, C_CONV_H, C_CONV_B, C_Z_ATTN = 0, 512, 1024, 1536, 2048

TILE = 512
HALF = TILE // 2
QK_AHEAD = 2
ONES_ROWS = 16
REF_ROW = NOPE + ROPE
REF_ROWS = 16
OUT_ROWS = 256
RISE_LIMIT = 100.0
VMEM_LIMIT_BYTES = 56 * 2 ** 20
Q_SCALE = ATTN_SCALE * math.log2(math.e)


def _rms(x, g):
    ms = jnp.mean(x * x, axis=-1, keepdims=True)
    return x * lax.rsqrt(ms + EPS) * g


def _silu(x):
    hx = (0.5 * x).astype(BF16)
    return hx + hx * jnp.tanh(hx)


def _dot(a, b):
    return jnp.dot(a, b, preferred_element_type=F32)


def _dot_nt(a, b):
    return lax.dot_general(a, b, (((1,), (1,)), ((), ())), preferred_element_type=F32)


def _in_proj(x, norm_g, w_head, w_tail):
    u = _rms(x, norm_g).astype(BF16)
    return _dot(u, w_head), _dot(u, w_tail)


def _keys_values(p, cos, sin, kv_g, w_k, w_vt):
    c_kv = _rms(p[:, C_KV:C_KV + KV_LORA], kv_g).astype(BF16)
    k_nope = _dot(c_kv, w_k)
    v_t = _dot_nt(w_vt, c_kv)
    kr = p[:, C_KROPE:C_KROPE + LANES]
    k_pe = kr * cos + pltpu.roll(kr, ROPE, 1) * sin
    lane = lax.broadcasted_iota(jnp.int32, k_pe.shape, 1)
    k_pe = jnp.where(lane == REF_ROW - NOPE, 1.0, k_pe).astype(BF16)
    ks = []
    for h in range(HEADS):
        ks.append(k_nope[:, h * NOPE:(h + 1) * NOPE].astype(BF16))
        ks.append(k_pe)
    return ks, v_t.astype(BF16)


def _meta_kernel(x_ref, cos_ref, sin_ref, norm_g_ref, w_head_ref, w_tail_ref, kv_g_ref, w_k_ref, w_vt_ref,
                 k_out, vt_out, g_out):
    ph, p = _in_proj(x_ref[...], norm_g_ref[...], w_head_ref[...], w_tail_ref[...])
    ks, v_t = _keys_values(ph, cos_ref[...], sin_ref[...], kv_g_ref[...], w_k_ref[...], w_vt_ref[...])
    for i, kk in enumerate(ks):
        k_out[:, i * LANES:(i + 1) * LANES] = kk
    vt_out[...] = v_t
    g_out[...] = p[:, C_CONV_C:C_CONV_C + CONV_W] * p[:, C_CONV_H:C_CONV_H + CONV_W]


def _proj_kernel(x_ref, cos_ref, sin_ref, cos_t_ref, sin_t_ref, ginit_ref, norm_g_ref, w_head_ref, w_tail_ref,
                 q_g_ref, w_qt_ref, kv_g_ref, w_k_ref, w_vt_ref, conv_w_ref, conv_g_ref, gmat_ref,
                 qt_out, k_out, vt_out, ga_out, yc_out, gbuf):
    tm = x_ref.shape[1]

    @pl.when(pl.program_id(1) == 0)
    def _():
        gbuf[...] = ginit_ref[8:16, :]

    ph, p = _in_proj(x_ref[0], norm_g_ref[...], w_head_ref[...], w_tail_ref[...])

    c_q = _rms(ph[:, C_Q:C_Q + Q_LORA], q_g_ref[...] * Q_SCALE).astype(BF16)
    q_t = _dot_nt(w_qt_ref[...], c_q)
    cos_t = cos_t_ref[...]
    sin_t = sin_t_ref[...]
    for h in range(HEADS):
        r = h * HEAD_PAD
        qt_out[0, r:r + NOPE, :] = q_t[r:r + NOPE, :].astype(BF16)
        q_pe = q_t[r + NOPE:r + NOPE + ROPE, :] * cos_t + q_t[r + NOPE + ROPE:r + HEAD_PAD, :] * sin_t
        qt_out[0, r + NOPE:r + NOPE + ROPE, :] = q_pe.astype(BF16)
        qt_out[0, r + NOPE + ROPE:r + HEAD_PAD, :] = jnp.zeros((ROPE, tm), BF16)

    ks, v_t = _keys_values(ph, cos_ref[...], sin_ref[...], kv_g_ref[...], w_k_ref[...], w_vt_ref[...])
    for i, kk in enumerate(ks):
        k_out[0, :, i * LANES:(i + 1) * LANES] = kk
    vt_out[0, 0] = v_t

    ga_out[0] = _silu(p[:, C_Z_ATTN:C_Z_ATTN + HEADS * VDIM]).astype(BF16)

    g = p[:, C_CONV_C:C_CONV_C + CONV_W] * p[:, C_CONV_H:C_CONV_H + CONV_W]
    carry = gbuf[...]
    first = lax.broadcasted_iota(jnp.int32, (8, CONV_W), 0) == 0

    def shift_down(v, row_before):
        r = pltpu.roll(v, 1, 0)
        return jnp.concatenate([jnp.where(first, row_before, r[0:8, :]), r[8:, :]], axis=0)

    g1 = shift_down(g, carry[7:8, :])
    g2 = shift_down(g1, carry[6:7, :])
    gbuf[...] = g[tm - 8:tm, :]
    cw = conv_w_ref[...]
    conv = cw[0:1, :] * g2 + cw[1:2, :] * g1 + cw[2:3, :] * g
    yc = p[:, C_CONV_B:C_CONV_B + CONV_W] * conv
    sq = (yc * yc).astype(BF16)
    half = CONV_W // 2
    ssum = jnp.concatenate([_dot(sq[:, :half], gmat_ref[:half, :half]), _dot(sq[:, half:], gmat_ref[half:, half:])],
                           axis=1)
    scaled = yc * (conv_g_ref[...] * _silu(p[:, C_Z_CONV:C_Z_CONV + CONV_W]).astype(F32))
    yc_out[0] = (scaled * lax.rsqrt(ssum * (1.0 / CONV_GROUP) + EPS)).astype(BF16)


def _ones_rows(v_t):
    return jnp.concatenate([v_t, jnp.ones((ONES_ROWS, v_t.shape[1]), BF16)], axis=0)


def _bf16_exact(x):
    return x.astype(BF16).astype(F32)


def _attention_tile(c, qt_ref, k_ref, vt_ref, km_ref, vtm_ref, q_aug, m_scr, acc_scr, p_scr, rise_scr):
    tri_full = (lax.broadcasted_iota(jnp.int32, (HALF, TILE), 0)
                <= lax.broadcasted_iota(jnp.int32, (HALF, TILE), 1))
    tri_half = tri_full[:, :HALF]

    def k_of(h, r0, r1):
        return k_ref[0, r0:r1, h * HEAD_PAD:(h + 1) * HEAD_PAD]

    def v_of(h, j, lo, hi):
        return vt_ref[0, j, h * VDIM:(h + 1) * VDIM, lo:hi]

    tasks = []
    for h in range(HEADS):
        tasks.append((h, lambda h=h: km_ref[:, h * HEAD_PAD:(h + 1) * HEAD_PAD],
                      lambda h=h: vtm_ref[h * VDIM:(h + 1) * VDIM, :], 0, TILE, None, True))
    for j in range(c):
        for h in range(HEADS):
            tasks.append((h, lambda h=h, j=j: k_of(h, j * TILE, (j + 1) * TILE),
                          lambda h=h, j=j: v_of(h, j, 0, TILE), 0, TILE, None, False))
    for h in range(HEADS):
        tasks.append((h, lambda h=h: k_of(h, c * TILE, c * TILE + HALF),
                      lambda h=h: v_of(h, c, 0, HALF), 0, TILE, tri_full, False))
    for h in range(HEADS):
        tasks.append((h, lambda h=h: k_of(h, c * TILE + HALF, (c + 1) * TILE),
                      lambda h=h: v_of(h, c, HALF, TILE), HALF, TILE, tri_half, False))
    assert QK_AHEAD <= 2 and all(t[0] == n % HEADS for n, t in enumerate(tasks))

    for h in range(HEADS):
        q_aug[h] = qt_ref[0, h * HEAD_PAD:(h + 1) * HEAD_PAD, :]
    rise_scr[...] = jnp.zeros(rise_scr.shape, F32)

    def scores(task):
        h, k_fn, _, lo, hi, mask, _ = task
        s = _dot(k_fn(), q_aug[h, :, lo:hi])
        return s if mask is None else jnp.where(mask, s, NEG_INF)

    def softmax(n, task, s):
        h, _, _, lo, hi, _, first = task
        nkeys, ncols = s.shape
        blk_max = jnp.max(s, axis=0, keepdims=True)
        if first:
            m_new, beta = _bf16_exact(blk_max), None
            p = jnp.exp2(s - m_new)
        else:
            rise = jnp.maximum(blk_max, 0.0)
            p = jnp.exp2(s)
            m_old = m_scr[h, :, lo:hi]
            m_new = _bf16_exact(m_old + rise)
            beta = jnp.exp2(m_old - m_new)
            rise_scr[:, lo:hi] = jnp.maximum(rise_scr[:, lo:hi], rise)
        p_scr[n % 2, 0:nkeys, 0:ncols] = p.astype(BF16)
        m_scr[h, :, lo:hi] = m_new
        q_aug[h, REF_ROW:REF_ROW + REF_ROWS, lo:hi] = jnp.broadcast_to(-m_new, (REF_ROWS, ncols)).astype(BF16)
        return nkeys, ncols, beta

    def values(n, task, nkeys, ncols, beta):
        h, _, v_fn, lo, hi, _, _ = task
        pv = _dot(_ones_rows(v_fn()), p_scr[n % 2, 0:nkeys, 0:ncols])
        if beta is None:
            acc_scr[h, :, lo:hi] = pv
        else:
            acc_scr[h, :, lo:hi] = (acc_scr[h, :, lo:hi] + pv) * beta

    pending = [scores(t) for t in tasks[:QK_AHEAD]]
    prev = None
    for n, task in enumerate(tasks):
        if prev is not None:
            values(*prev)
        if n + QK_AHEAD < len(tasks):
            pending.append(scores(tasks[n + QK_AHEAD]))
        prev = (n, task) + softmax(n, task, pending.pop(0))
    values(*prev)


def _attention_tile_two_pass(i, qt_ref, k_ref, vt_ref, km_ref, vtm_ref, m_scr, acc_scr):
    row = lax.broadcasted_iota(jnp.int32, (TILE, TILE), 0)
    col = lax.broadcasted_iota(jnp.int32, (TILE, TILE), 1)

    def q_of(h):
        return qt_ref[0, h * HEAD_PAD:(h + 1) * HEAD_PAD, :]

    for h in range(HEADS):
        s = _dot(km_ref[:, h * HEAD_PAD:(h + 1) * HEAD_PAD], q_of(h))
        m = jnp.max(s, axis=0, keepdims=True)
        m_scr[h] = m
        acc_scr[h] = _dot(_ones_rows(vtm_ref[h * VDIM:(h + 1) * VDIM, :]), jnp.exp2(s - m).astype(BF16))

    def block(j, carry):
        start = pl.multiple_of(j * TILE, TILE)
        visible = (j - i) * TILE + row <= col
        for h in range(HEADS):
            k = k_ref[0, pl.ds(start, TILE), h * HEAD_PAD:(h + 1) * HEAD_PAD]
            s = jnp.where(visible, _dot(k, q_of(h)), NEG_INF)
            m_old = m_scr[h]
            m_new = jnp.maximum(m_old, jnp.max(s, axis=0, keepdims=True))
            pv = _dot(_ones_rows(vt_ref[0, j, h * VDIM:(h + 1) * VDIM, :]), jnp.exp2(s - m_new).astype(BF16))
            acc_scr[h] = jnp.exp2(m_old - m_new) * acc_scr[h] + pv
            m_scr[h] = m_new
        return carry

    lax.fori_loop(0, i + 1, block, 0)


def _finish_tile(acc_scr, ga_ref, yc_ref, x_ref, w_out_ref, attn_g_ref, final_g_ref, o_ref, y_scr):
    for h in range(HEADS):
        a = acc_scr[h, 0:VDIM, :]
        l = acc_scr[h, VDIM:VDIM + 1, :]
        o_t = a * lax.rsqrt(jnp.mean(a * a, axis=0, keepdims=True) + EPS * (l * l))
        gain = attn_g_ref[:, h * VDIM:(h + 1) * VDIM] * ga_ref[0, :, h * VDIM:(h + 1) * VDIM].astype(F32)
        y_scr[:, h * VDIM:(h + 1) * VDIM] = (o_t.T * gain).astype(BF16)

    y_scr[:, HEADS * VDIM:] = yc_ref[0]
    for r0 in range(0, TILE, OUT_ROWS):
        mix = _dot(y_scr[r0:r0 + OUT_ROWS, :], w_out_ref[...])
        o_ref[0, r0:r0 + OUT_ROWS, :] = _rms(x_ref[0, r0:r0 + OUT_ROWS, :] + mix, final_g_ref[...])


def _attn_kernel(qt_ref, k_ref, vt_ref, km_ref, vtm_ref, ga_ref, yc_ref, x_ref, w_out_ref,
                 attn_g_ref, final_g_ref, o_ref, q_aug, m_scr, acc_scr, p_scr, rise_scr, y_scr, redo_flag):
    nblk = k_ref.shape[1] // TILE
    finish = functools.partial(_finish_tile, acc_scr, ga_ref, yc_ref, x_ref, w_out_ref, attn_g_ref,
                               final_g_ref, o_ref, y_scr)
    for c in range(nblk):
        @pl.when(pl.program_id(1) == c)
        def _(c=c):
            _attention_tile(c, qt_ref, k_ref, vt_ref, km_ref, vtm_ref, q_aug, m_scr, acc_scr, p_scr, rise_scr)
            redo_flag[0] = (jnp.max(rise_scr[...]) > RISE_LIMIT).astype(jnp.int32)
            finish()

    @pl.when(redo_flag[0] != 0)
    def _():
        _attention_tile_two_pass(pl.program_id(1), qt_ref, k_ref, vt_ref, km_ref, vtm_ref, m_scr, acc_scr)
        finish()


def _rope_angles(pos):
    half = ROPE // 2
    inv_freq = (1.0 / (ROPE_THETA ** (np.arange(half, dtype=np.float32) / half))).astype(np.float32)
    ang = pos.astype(np.float32)[:, None] * inv_freq[None, :]
    return np.cos(ang).astype(np.float32), np.sin(ang).astype(np.float32)


def _swap_halves(w):
    half = w.shape[-1] // 2
    return jnp.concatenate([w[..., half:], w[..., :half]], axis=-1)


def _full(shape):
    return pl.BlockSpec(shape, lambda *_: (0,) * len(shape))


def _layer(x, meta_tokens, norm_g, w_in, q_norm_g, w_q_up, kv_norm_g, w_kv_up, conv_w,
           attn_out_g, conv_out_g, w_out, final_norm_g):
    B, S, D = x.shape
    assert D == D_MODEL and S % TILE == 0
    assert meta_tokens.shape == (N_META, D_MODEL)
    nblk = S // TILE

    widths = dict(c_q=Q_LORA, c_kv=KV_LORA, k_rope=ROPE, z_attn=HEADS * VDIM, conv_b=CONV_W, conv_c=CONV_W,
                  conv_h=CONV_W, z_conv=CONV_W)
    starts = dict(zip(widths, np.cumsum([0] + list(widths.values())[:-1]).tolist()))
    assert starts["z_conv"] + CONV_W == w_in.shape[1]
    sec = lambda name: w_in[:, starts[name]:starts[name] + widths[name]]
    w_head = jnp.concatenate([sec("c_q"), sec("c_kv"), sec("k_rope"), _swap_halves(sec("k_rope"))],
                             axis=1).astype(BF16)
    w_tail = jnp.concatenate([sec("z_conv"), sec("conv_c"), sec("conv_h"), sec("conv_b"), sec("z_attn")],
                             axis=1).astype(BF16)
    assert w_head.shape[1] == HEAD_COLS and w_tail.shape[1] == TAIL_COLS
    wq = w_q_up.reshape(Q_LORA, HEADS, NOPE + ROPE)
    wq_t = jnp.concatenate([wq, _swap_halves(wq[..., NOPE:])], axis=-1).reshape(Q_LORA, HEADS * HEAD_PAD).T.astype(BF16)
    wkv = w_kv_up.reshape(KV_LORA, HEADS, NOPE + VDIM)
    w_k = wkv[..., :NOPE].reshape(KV_LORA, HEADS * NOPE).astype(BF16)
    w_vt = wkv[..., NOPE:].reshape(KV_LORA, HEADS * VDIM).T.astype(BF16)
    w_out_b = w_out.astype(BF16)
    gid = np.arange(CONV_W) // CONV_GROUP
    gmat = jnp.asarray((gid[:, None] == gid[None, :]).astype(np.float32), dtype=BF16)
    row = lambda v: v.reshape(1, -1).astype(F32)

    zpad = lambda n: np.zeros((n, LANES - ROPE), np.float32)
    c_m, s_m = _rope_angles(np.arange(N_META))
    cos_m = np.concatenate([c_m, c_m, zpad(N_META)], axis=-1)
    sin_m = np.concatenate([-s_m, s_m, zpad(N_META)], axis=-1)
    c_r, s_r = _rope_angles(N_META + np.arange(S))
    cos_r = np.concatenate([c_r, c_r, zpad(S)], axis=-1)
    sin_r = np.concatenate([-s_r, s_r, zpad(S)], axis=-1)
    cos_t = np.ascontiguousarray(np.concatenate([c_r, c_r], axis=-1).T)
    sin_t = np.ascontiguousarray(np.concatenate([-s_r, s_r], axis=-1).T)

    k_meta, vt_meta, g_meta = pl.pallas_call(
        _meta_kernel,
        out_shape=(jax.ShapeDtypeStruct((N_META, HEADS * HEAD_PAD), BF16),
                   jax.ShapeDtypeStruct((HEADS * VDIM, N_META), BF16),
                   jax.ShapeDtypeStruct((N_META, CONV_W), F32)),
        name="meta_proj",
    )(meta_tokens.astype(F32), cos_m, sin_m, row(norm_g), w_head, w_tail, row(kv_norm_g), w_k, w_vt)

    tok = lambda w: pl.BlockSpec((1, TILE, w), lambda b, t: (b, t, 0))
    qt_s, k_s, vt_s, ga_s, yc_s = pl.pallas_call(
        _proj_kernel,
        grid=(B, nblk),
        in_specs=[tok(D_MODEL),
                  pl.BlockSpec((TILE, LANES), lambda b, t: (t, 0)),
                  pl.BlockSpec((TILE, LANES), lambda b, t: (t, 0)),
                  pl.BlockSpec((ROPE, TILE), lambda b, t: (0, t)),
                  pl.BlockSpec((ROPE, TILE), lambda b, t: (0, t)),
                  _full((N_META, CONV_W)), _full((1, D_MODEL)),
                  _full((D_MODEL, HEAD_COLS)), _full((D_MODEL, TAIL_COLS)),
                  _full((1, Q_LORA)), _full((HEADS * HEAD_PAD, Q_LORA)),
                  _full((1, KV_LORA)), _full((KV_LORA, HEADS * NOPE)), _full((HEADS * VDIM, KV_LORA)),
                  _full((3, CONV_W)), _full((1, CONV_W)), _full((CONV_W, CONV_W))],
        out_specs=[pl.BlockSpec((1, HEADS * HEAD_PAD, TILE), lambda b, t: (b, 0, t)),
                   tok(HEADS * HEAD_PAD),
                   pl.BlockSpec((1, 1, HEADS * VDIM, TILE), lambda b, t: (b, t, 0, 0)),
                   tok(HEADS * VDIM), tok(CONV_W)],
        out_shape=[jax.ShapeDtypeStruct((B, HEADS * HEAD_PAD, S), BF16),
                   jax.ShapeDtypeStruct((B, S, HEADS * HEAD_PAD), BF16),
                   jax.ShapeDtypeStruct((B, nblk, HEADS * VDIM, TILE), BF16),
                   jax.ShapeDtypeStruct((B, S, HEADS * VDIM), BF16),
                   jax.ShapeDtypeStruct((B, S, CONV_W), BF16)],
        scratch_shapes=[pltpu.VMEM((8, CONV_W), F32)],
        compiler_params=pltpu.CompilerParams(dimension_semantics=("arbitrary", "arbitrary"),
                                             vmem_limit_bytes=VMEM_LIMIT_BYTES),
        name="proj",
    )(x, cos_r, sin_r, cos_t, sin_t, g_meta, row(norm_g), w_head, w_tail, row(q_norm_g), wq_t,
      row(kv_norm_g), w_k, w_vt, conv_w.astype(F32), row(conv_out_g), gmat)

    qtile = lambda w: pl.BlockSpec((1, TILE, w), lambda b, i: (b, i, 0))
    out = pl.pallas_call(
        _attn_kernel,
        grid=(B, nblk),
        in_specs=[pl.BlockSpec((1, HEADS * HEAD_PAD, TILE), lambda b, i: (b, 0, i)),
                  pl.BlockSpec((1, S, HEADS * HEAD_PAD), lambda b, i: (b, 0, 0)),
                  pl.BlockSpec((1, nblk, HEADS * VDIM, TILE), lambda b, i: (b, 0, 0, 0)),
                  _full((N_META, HEADS * HEAD_PAD)), _full((HEADS * VDIM, N_META)),
                  qtile(HEADS * VDIM), qtile(CONV_W), qtile(D_MODEL),
                  _full((D_MODEL, D_MODEL)), _full((1, HEADS * VDIM)), _full((1, D_MODEL))],
        out_specs=qtile(D_MODEL),
        out_shape=jax.ShapeDtypeStruct((B, S, D_MODEL), F32),
        scratch_shapes=[pltpu.VMEM((HEADS, HEAD_PAD, TILE), BF16),
                        pltpu.VMEM((HEADS, 1, TILE), F32),
                        pltpu.VMEM((HEADS, VDIM + ONES_ROWS, TILE), F32),
                        pltpu.VMEM((2, TILE, TILE), BF16),
                        pltpu.VMEM((1, TILE), F32),
                        pltpu.VMEM((TILE, D_MODEL), BF16),
                        pltpu.SMEM((1,), jnp.int32)],
        compiler_params=pltpu.CompilerParams(dimension_semantics=("arbitrary", "arbitrary"),
                                             vmem_limit_bytes=VMEM_LIMIT_BYTES),
        name="attn_out",
    )(qt_s, k_s, vt_s, k_meta, vt_meta, ga_s, yc_s, x, w_out_b, row(attn_out_g), row(final_norm_g))
    return out


def kernel(x, meta_tokens, norm_g, w_in, q_norm_g, w_q_up, kv_norm_g, w_kv_up, conv_w,
           attn_out_g, conv_out_g, w_out, final_norm_g):
    assert norm_g.shape[0] == 1, "single-layer block"
    return _layer(x, meta_tokens, norm_g[0], w_in[0], q_norm_g[0], w_q_up[0], kv_norm_g[0],
                  w_kv_up[0], conv_w[0], attn_out_g[0], conv_out_g[0], w_out[0], final_norm_g)
```
